```python
import math
import jax, jax.numpy as jnp
from jax import lax
import numpy as np

D_MODEL = 1024
BATCH = 8
SEQ = 4096
DEPTH = 4
DEC_BATCH = 32
DEC_SEQ = 64
PAST_LEN = 4096

CHUNK = 64
RET_HEADS = 4
RET_DK = D_MODEL // 8
RET_DV = D_MODEL // 8
RET_WIDTH = RET_HEADS * RET_DV
ATT_HEAD_DIM = 64
ATT_Q_HEADS = (D_MODEL // 2) // ATT_HEAD_DIM
ATT_KV_HEADS = 2
ATT_GROUP = ATT_Q_HEADS // ATT_KV_HEADS
ATT_WIDTH = ATT_Q_HEADS * ATT_HEAD_DIM
WINDOW = 128
MIX_WIDTH = RET_WIDTH + ATT_WIDTH
IN_WIDTH = (2 * RET_HEADS * RET_DK + RET_HEADS * RET_DV + RET_WIDTH
            + ATT_WIDTH + 2 * ATT_KV_HEADS * ATT_HEAD_DIM + ATT_WIDTH)
ROPE_BASE = 10000.0
NORM_EPS = 1e-6
GN_EPS = 1e-5

kernel_name = "hymba_retention_swa_sink_stream_step"


def _rmsnorm(x, g):
    xf = x.astype(jnp.float32)
    y = xf * lax.rsqrt(jnp.mean(xf * xf, axis=-1, keepdims=True) + NORM_EPS)
    return (y * g.astype(jnp.float32)).astype(x.dtype)


def _head_norm(o):
    mu = jnp.mean(o, axis=-1, keepdims=True)
    var = jnp.mean(jnp.square(o - mu), axis=-1, keepdims=True)
    return (o - mu) * lax.rsqrt(var + GN_EPS)


def _rope(x, pos):
    d = x.shape[-1]
    inv = 1.0 / (ROPE_BASE ** (jnp.arange(0, d, 2, dtype=jnp.float32) / d))
    ang = pos.astype(jnp.float32)[:, None] * inv[None, :]
    cos = jnp.cos(ang)[None, :, None, :]
    sin = jnp.sin(ang)[None, :, None, :]
    x1, x2 = x[..., : d // 2], x[..., d // 2:]
    return jnp.concatenate([x1 * cos - x2 * sin, x1 * sin + x2 * cos], axis=-1)


def _log_gamma():
    return jnp.log(1.0 - 2.0 ** (-5.0 - jnp.arange(RET_HEADS, dtype=jnp.float32)))


def _retention_chunk(R, q, k, v):
    lg = _log_gamma()
    L = q.shape[1]
    idx = jnp.arange(L, dtype=jnp.float32)
    diff = idx[:, None] - idx[None, :]
    decay = jnp.where(diff[None] >= 0, jnp.exp(jnp.maximum(diff, 0.0)[None] * lg[:, None, None]), 0.0)
    s = jnp.einsum('bqhd,bkhd->bhqk', q, k) * decay[None]
    inner = jnp.einsum('bhqk,bkhv->bqhv', s, v)
    q_dec = jnp.exp((idx + 1.0)[:, None] * lg[None, :])[None, :, :, None]
    cross = jnp.einsum('bqhd,bhdv->bqhv', q, R) * q_dec
    k_dec = jnp.exp((L - 1.0 - idx)[:, None] * lg[None, :])[None, :, :, None]
    R_new = jnp.exp(L * lg)[None, :, None, None] * R + jnp.einsum('bkhd,bkhv->bhdv', k * k_dec, v)
    return R_new, inner + cross


def _retention_prompt(q, k, v):
    B, S, H, _ = q.shape
    nc = S // CHUNK

    def to_chunks(t):
        return t.reshape(B, nc, CHUNK, H, t.shape[-1]).transpose(1, 0, 2, 3, 4)

    R0 = jnp.zeros((B, H, RET_DK, RET_DV), jnp.float32)

    def step(R, inp):
        qc, kc, vc = inp
        return _retention_chunk(R, qc, kc, vc)

    R, o = lax.scan(step, R0, (to_chunks(q), to_chunks(k), to_chunks(v)))
    o = o.transpose(1, 0, 2, 3, 4).reshape(B, S, H, RET_DV)
    return o, R


def _sink_attend(q, k, v, sink, mask):
    s = jnp.einsum('...qhgd,...khd->...hgqk', q, k).astype(jnp.float32) * (ATT_HEAD_DIM ** -0.5)
    if mask is not None:
        s = jnp.where(mask, s, -jnp.inf)
    sk = sink.astype(jnp.float32).reshape(ATT_KV_HEADS, ATT_GROUP)[:, :, None, None]
    m = jnp.maximum(jnp.max(s, axis=-1, keepdims=True), sk)
    p = jnp.exp(s - m)
    denom = jnp.sum(p, axis=-1, keepdims=True) + jnp.exp(sk - m)
    return jnp.einsum('...hgqk,...khd->...qhgd', (p / denom).astype(v.dtype), v)


def _swa_prompt(q, k, v, sink):
    B, S = k.shape[0], k.shape[1]
    nc = S // CHUNK
    nwc = WINDOW // CHUNK
    qb = q.reshape(B, nc, CHUNK, ATT_KV_HEADS, ATT_GROUP, ATT_HEAD_DIM)
    pad = ((0, 0), (WINDOW, 0), (0, 0), (0, 0))
    kp = jnp.pad(k, pad).reshape(B, nc + nwc, CHUNK, ATT_KV_HEADS, ATT_HEAD_DIM)
    vp = jnp.pad(v, pad).reshape(B, nc + nwc, CHUNK, ATT_KV_HEADS, ATT_HEAD_DIM)
    kb = jnp.concatenate([kp[:, j:j + nc] for j in range(nwc + 1)], axis=2)
    vb = jnp.concatenate([vp[:, j:j + nc] for j in range(nwc + 1)], axis=2)
    kpos = jnp.arange(nc)[:, None] * CHUNK - WINDOW + jnp.arange(WINDOW + CHUNK)[None, :]
    mask = (kpos >= 0)[:, None, None, None, :]
    o = _sink_attend(qb, kb, vb, sink, mask)
    return o.reshape(B, S, ATT_WIDTH)


def _in_proj(h, w_in):
    z = h @ w_in
    B, L, _ = z.shape
    sizes = [RET_HEADS * RET_DK, RET_HEADS * RET_DK, RET_HEADS * RET_DV, RET_WIDTH,
             ATT_WIDTH, ATT_KV_HEADS * ATT_HEAD_DIM, ATT_KV_HEADS * ATT_HEAD_DIM, ATT_WIDTH]
    offs = list(np.cumsum(sizes)[:-1])
    rq, rk, rv, rg, aq, ak, av, ag = jnp.split(z, offs, axis=-1)
    rq = rq.reshape(B, L, RET_HEADS, RET_DK)
    rk = rk.reshape(B, L, RET_HEADS, RET_DK)
    rv = rv.reshape(B, L, RET_HEADS, RET_DV)
    aq = aq.reshape(B, L, ATT_KV_HEADS, ATT_GROUP, ATT_HEAD_DIM)
    ak = ak.reshape(B, L, ATT_KV_HEADS, ATT_HEAD_DIM)
    av = av.reshape(B, L, ATT_KV_HEADS, ATT_HEAD_DIM)
    return rq, rk, rv, rg, aq, ak, av, ag


def _layer(x, c, pos, ret_state, k_buf, v_buf, norm_g, w_ada, b_ada, w_in, sink, w_out):
    B, L, _ = x.shape
    mod = jax.nn.silu(c) @ w_ada + b_ada
    shift, scale, gate = jnp.split(mod[:, None, :], 3, axis=-1)
    h = _rmsnorm(x, norm_g) * (1.0 + scale) + shift
    rq, rk, rv, rg, aq, ak, av, ag = _in_proj(h, w_in)
    rq = _rope(rq.astype(jnp.float32), pos)
    rk = _rope(rk.astype(jnp.float32), pos) * (RET_DK ** -0.5)
    rv = rv.astype(jnp.float32)
    if ret_state is None:
        r_out, r_state = _retention_prompt(rq, rk, rv)
    else:
        r_state, r_out = _retention_chunk(ret_state.astype(jnp.float32), rq, rk, rv)
    r_out = _head_norm(r_out).reshape(B, L, RET_WIDTH).astype(x.dtype) * jax.nn.silu(rg)
    if k_buf is None:
        a_out = _swa_prompt(aq, ak, av, sink)
        new_k, new_v = ak[:, -WINDOW:], av[:, -WINDOW:]
    else:
        kf = jnp.concatenate([k_buf.astype(ak.dtype), ak], axis=1)
        vf = jnp.concatenate([v_buf.astype(av.dtype), av], axis=1)
        a_out = _sink_attend(aq, kf, vf, sink, None).reshape(B, L, ATT_WIDTH)
        new_k, new_v = kf[:, -WINDOW:], vf[:, -WINDOW:]
    a_out = a_out * jax.nn.silu(ag)
    y = jnp.concatenate([r_out, a_out], axis=-1) @ w_out
    x = x + gate * y
    return x, r_state, new_k, new_v


def setup_inputs(seed: int = 0) -> dict:
    key = jax.random.key(seed)
    ks = jax.random.split(key, 16)
    f = jnp.float32
    D = D_MODEL
    x_prompt = jax.random.normal(ks[0], (BATCH, SEQ, D), f)
    x_sample = jax.random.normal(ks[1], (DEC_BATCH, DEC_SEQ, D), f)
    c_prompt = jax.random.normal(ks[2], (BATCH, D), f)
    c_sample = jax.random.normal(ks[3], (DEC_BATCH, D), f)
    state_ret = 0.5 * jax.random.normal(ks[4], (DEPTH, DEC_BATCH, RET_HEADS, RET_DK, RET_DV), f)
    cache_k = jax.random.normal(ks[5], (DEPTH, DEC_BATCH, WINDOW, ATT_KV_HEADS, ATT_HEAD_DIM), f)
    cache_v = jax.random.normal(ks[6], (DEPTH, DEC_BATCH, WINDOW, ATT_KV_HEADS, ATT_HEAD_DIM), f)
    norm_g = 1.0 + 0.02 * jax.random.normal(ks[7], (DEPTH, D), f)
    w_ada = 0.5 * jax.random.normal(ks[8], (DEPTH, D, 3 * D), f) * D ** -0.5
    b_ada = 0.01 * jax.random.normal(ks[9], (DEPTH, 3 * D), f)
    w_in = jax.random.normal(ks[10], (DEPTH, D, IN_WIDTH), f) * D ** -0.5
    sink = 0.5 * jax.random.normal(ks[11], (DEPTH, ATT_Q_HEADS), f)
    w_out = jax.random.normal(ks[12], (DEPTH, MIX_WIDTH, D), f) * MIX_WIDTH ** -0.5
    final_g = 1.0 + 0.02 * jax.random.normal(ks[13], (D,), f)
    return {"x_prompt": x_prompt, "x_sample": x_sample, "c_prompt": c_prompt, "c_sample": c_sample,
            "state_ret": state_ret, "cache_k": cache_k, "cache_v": cache_v,
            "norm_g": norm_g, "w_ada": w_ada, "b_ada": b_ada, "w_in": w_in, "sink": sink,
            "w_out": w_out, "final_g": final_g}


def reference(x_prompt, x_sample, c_prompt, c_sample, state_ret, cache_k, cache_v,
              norm_g, w_ada, b_ada, w_in, sink, w_out, final_g):
    pos_p = jnp.arange(x_prompt.shape[1])
    pos_s = PAST_LEN + jnp.arange(x_sample.shape[1])
    xp, xs = x_prompt, x_sample
    rp, kp, vp, rs, ks_, vs = [], [], [], [], [], []
    for l in range(DEPTH):
        xp, r1, k1, v1 = _layer(xp, c_prompt, pos_p, None, None, None,
                                norm_g[l], w_ada[l], b_ada[l], w_in[l], sink[l], w_out[l])
        xs, r2, k2, v2 = _layer(xs, c_sample, pos_s, state_ret[l], cache_k[l], cache_v[l],
                                norm_g[l], w_ada[l], b_ada[l], w_in[l], sink[l], w_out[l])
        rp.append(r1.astype(x_prompt.dtype)); kp.append(k1); vp.append(v1)
        rs.append(r2.astype(state_ret.dtype)); ks_.append(k2); vs.append(v2)
    y_prompt = _rmsnorm(xp, final_g)
    y_sample = _rmsnorm(xs, final_g)
    return (y_prompt, y_sample, jnp.stack(rp), jnp.stack(kp), jnp.stack(vp),
            jnp.stack(rs), jnp.stack(ks_), jnp.stack(vs))
```

```python
import functools
from typing import NamedTuple

import jax
import jax.numpy as jnp
from jax import lax
from jax.experimental import pallas as pl
from jax.experimental.pallas import tpu as pltpu

D_MODEL = 1024
DEPTH = 4
CHUNK = 64
PAST_LEN = 4096
RET_HEADS = 4
RET_DK = 128
RET_DV = 128
RET_WIDTH = RET_HEADS * RET_DV
ATT_HEAD_DIM = 64
ATT_Q_HEADS = 8
ATT_KV_HEADS = 2
ATT_WIDTH = ATT_Q_HEADS * ATT_HEAD_DIM
KV_WIDTH = ATT_KV_HEADS * ATT_HEAD_DIM
WINDOW = 128
MIX_WIDTH = RET_WIDTH + ATT_WIDTH
ROPE_BASE = 10000.0
NORM_EPS = 1e-6
GN_EPS = 1e-5

OFF_RQ = 0
OFF_RK = OFF_RQ + RET_HEADS * RET_DK
OFF_RV = OFF_RK + RET_HEADS * RET_DK
OFF_RG = OFF_RV + RET_WIDTH
OFF_AQ = OFF_RG + RET_WIDTH
OFF_AK = OFF_AQ + ATT_WIDTH
OFF_AV = OFF_AK + KV_WIDTH
OFF_AG = OFF_AV + KV_WIDTH
IN_WIDTH = OFF_AG + ATT_WIDTH

LANES = 128
ATT_SLABS = ATT_WIDTH // LANES
MASKED = -1e30
VMEM_LIMIT_BYTES = 56 * 1024 * 1024

BF16 = jnp.bfloat16
F32 = jnp.float32


class _Cfg(NamedTuple):
    rows: int
    nseq: int
    blk: int
    lk: int
    carried: bool
    last: bool
    layer: int


def _silu(g):
    return g * (1.0 / (1.0 + jnp.exp(-g)))


def _dot(a, b):
    return jnp.dot(a, b, preferred_element_type=F32)


def _dot_nt(a, b):
    return lax.dot_general(a, b, (((1,), (1,)), ((), ())), preferred_element_type=F32)


def _adaln_kernel(c_ref, w_ref, b_ref, o_ref):
    s = _silu(c_ref[...]).astype(BF16)
    o_ref[0] = _dot(s, w_ref[0].astype(BF16)) + b_ref[0]


def _adaln(c_all, w_ada, b_ada):
    n = c_all.shape[0]
    tn = D_MODEL
    return pl.pallas_call(
        _adaln_kernel,
        grid=(DEPTH, 3 * D_MODEL // tn),
        in_specs=[
            pl.BlockSpec((n, D_MODEL), lambda l, j: (0, 0)),
            pl.BlockSpec((1, D_MODEL, tn), lambda l, j: (l, 0, j)),
            pl.BlockSpec((1, 1, tn), lambda l, j: (l, 0, j)),
        ],
        out_specs=pl.BlockSpec((1, n, tn), lambda l, j: (l, 0, j)),
        out_shape=jax.ShapeDtypeStruct((DEPTH, n, 3 * D_MODEL), F32),
        compiler_params=pltpu.CompilerParams(
            dimension_semantics=("arbitrary", "arbitrary"),
            vmem_limit_bytes=VMEM_LIMIT_BYTES),
        name="adaln",
    )(c_all, w_ada, b_ada.reshape(DEPTH, 1, 3 * D_MODEL))


def _layer_kernel(*refs, cfg: _Cfg):
    n_in = 18 if not cfg.carried else 15
    (x_ref, mod_ref, g_ref, win_ref, wout_ref, cq_ref, sq_ref, ck_ref, sk_ref,
     decay_ref, qdec_ref, kdec_ref, gl_ref, sink_ref, fg_ref) = refs[:15]
    if not cfg.carried:
        st_ref, cachek_ref, cachev_ref = refs[15:18]
    xo_ref, ro_ref, ko_ref, vo_ref = refs[n_in:n_in + 4]
    scratch = refs[n_in + 4:]
    z_ref, h_ref, mix_ref, kb_ref, kr_ref, va_ref, vb_ref, vc_ref, vd_ref = scratch[:9]
    if cfg.carried:
        r_scr = scratch[9]

    rows, nseq, blk, lk = cfg.rows, cfg.nseq, cfg.blk, cfg.lk
    seg = rows // nseq
    nblk = rows // blk
    layer = cfg.layer
    t = pl.program_id(1) if cfg.carried else None

    lane = lax.broadcasted_iota(jnp.int32, (1, LANES), 1)
    low = lane < ATT_HEAD_DIM

    g_row = g_ref[...]
    nb = 32
    for s in range(nseq):
        m = mod_ref[s]
        shift = m[:, 0:D_MODEL]
        scale1 = 1.0 + m[:, D_MODEL:2 * D_MODEL]

        def norm_body(i, carry, s=s, shift=shift, scale1=scale1):
            r = pl.ds(pl.multiple_of(s * seg + i * nb, nb), nb)
            xb = x_ref[r, :]
            ms = jnp.mean(xb * xb, axis=-1, keepdims=True)
            y = xb * lax.rsqrt(ms + NORM_EPS) * g_row
            h_ref[r, :] = (y * scale1 + shift).astype(BF16)
            return carry

        lax.fori_loop(0, seg // nb, norm_body, 0)

    tn = 512
    for c0 in range(0, IN_WIDTH, tn):
        c1 = min(c0 + tn, IN_WIDTH)
        z_ref[:, c0:c1] = _dot(h_ref[...], win_ref[:, c0:c1])

    def fill(dst0, n, k, v):
        kr = pltpu.roll(k, ATT_HEAD_DIM, 1)
        vr = pltpu.roll(v, ATT_HEAD_DIM, 1)
        d = pl.ds(dst0, n)
        kb_ref[d, :] = k.astype(BF16)
        kr_ref[d, :] = kr.astype(BF16)
        one_lo = jnp.broadcast_to(jnp.where(low, 1.0, 0.0), (n, LANES))
        one_hi = 1.0 - one_lo
        va_ref[d, :] = jnp.concatenate([jnp.where(low, v, 0.0), one_lo], axis=1).astype(BF16)
        vb_ref[d, :] = jnp.concatenate([jnp.where(low, 0.0, vr), one_hi], axis=1).astype(BF16)
        vc_ref[d, :] = jnp.concatenate([jnp.where(low, vr, 0.0), one_lo], axis=1).astype(BF16)
        vd_ref[d, :] = jnp.concatenate([jnp.where(low, 0.0, v), one_hi], axis=1).astype(BF16)

    kv_bufs = (kb_ref, kr_ref, va_ref, vb_ref, vc_ref, vd_ref)
    if cfg.carried:
        @pl.when(t == 0)
        def _():
            for b in kv_bufs:
                b[0:WINDOW, :] = jnp.zeros((WINDOW, b.shape[1]), BF16)
            r_scr[...] = jnp.zeros(r_scr.shape, F32)

        fb = 64
        for r0 in range(0, rows, fb):
            fill(WINDOW + r0, fb, z_ref[r0:r0 + fb, OFF_AK:OFF_AK + KV_WIDTH],
                 z_ref[r0:r0 + fb, OFF_AV:OFF_AV + KV_WIDTH])
    else:
        for s in range(nseq):
            fill(s * lk, WINDOW, cachek_ref[s], cachev_ref[s])
            fill(s * lk + WINDOW, seg, z_ref[s * seg:(s + 1) * seg, OFF_AK:OFF_AK + KV_WIDTH],
                 z_ref[s * seg:(s + 1) * seg, OFF_AV:OFF_AV + KV_WIDTH])

    if cfg.carried:
        row_i = lax.broadcasted_iota(jnp.int32, (blk, lk), 0)
        col_i = lax.broadcasted_iota(jnp.int32, (blk, lk), 1)
        first_key = jnp.where(row_i < CHUNK, 0, CHUNK)
        band = (col_i >= first_key) & (col_i < first_key + WINDOW + CHUNK)

    def block(j, carry):
        r0 = pl.multiple_of(j * blk, blk)
        rws = pl.ds(r0, blk)
        if cfg.carried:
            win = pl.ds(r0, lk)
            trows = rws
        else:
            win = pl.ds(pl.multiple_of(j * lk, CHUNK), lk)
            trows = pl.ds(0, blk)

        cq, sq, ck, sk = cq_ref[trows, :], sq_ref[trows, :], ck_ref[trows, :], sk_ref[trows, :]
        for hd in range(RET_HEADS):
            c = hd * RET_DK
            q = z_ref[rws, OFF_RQ + c:OFF_RQ + c + RET_DK]
            k = z_ref[rws, OFF_RK + c:OFF_RK + c + RET_DK]
            v = z_ref[rws, OFF_RV + c:OFF_RV + c + RET_DV].astype(BF16)
            q = q * cq + pltpu.roll(q, RET_DK // 2, 1) * sq
            k = k * ck + pltpu.roll(k, RET_DK // 2, 1) * sk
            s_in = _dot_nt(q.astype(BF16), k.astype(BF16)) * decay_ref[hd]
            r_old = r_scr[hd] if cfg.carried else st_ref[j, hd]
            qd = (q * qdec_ref[hd]).astype(BF16)
            o = _dot(s_in.astype(BF16), v) + _dot(qd, r_old.astype(BF16))
            kd_t = (k * kdec_ref[hd]).T.astype(BF16)
            r_new = gl_ref[hd] * r_old + _dot(kd_t, v)
            if cfg.carried:
                r_scr[hd] = r_new
            else:
                ro_ref[j, hd] = r_new
            mu = jnp.mean(o, axis=-1, keepdims=True)
            oc = o - mu
            var = jnp.mean(oc * oc, axis=-1, keepdims=True)
            on = oc * lax.rsqrt(var + GN_EPS)
            gate = _silu(z_ref[rws, OFF_RG + c:OFF_RG + c + RET_DV])
            mix_ref[rws, c:c + RET_DV] = (on * gate).astype(BF16)

        if cfg.carried:
            min_col = jnp.where((t == 0) & (j == 0), WINDOW, 0)
            valid = band & (col_i >= min_col)
        for sl in range(ATT_SLABS):
            c = sl * LANES
            kv = (2 * sl) // (ATT_Q_HEADS // ATT_KV_HEADS)
            qf = z_ref[rws, OFF_AQ + c:OFF_AQ + c + LANES] * (ATT_HEAD_DIM ** -0.5)
            q_lo = jnp.where(low, qf, 0.0).astype(BF16)
            q_hi = jnp.where(low, 0.0, qf).astype(BF16)
            if kv == 0:
                k_lo, k_hi, v_lo, v_hi = kb_ref, kr_ref, va_ref, vb_ref
            else:
                k_lo, k_hi, v_lo, v_hi = kr_ref, kb_ref, vc_ref, vd_ref
            s0 = _dot_nt(q_lo, k_lo[win, :])
            s1 = _dot_nt(q_hi, k_hi[win, :])
            if cfg.carried:
                s0 = jnp.where(valid, s0, MASKED)
                s1 = jnp.where(valid, s1, MASKED)
            sink0 = sink_ref[layer, 2 * sl]
            sink1 = sink_ref[layer, 2 * sl + 1]
            m0 = jnp.maximum(jnp.max(s0, axis=-1, keepdims=True), sink0)
            m1 = jnp.maximum(jnp.max(s1, axis=-1, keepdims=True), sink1)
            p0 = jnp.exp(s0 - m0).astype(BF16)
            p1 = jnp.exp(s1 - m1).astype(BF16)
            acc = _dot(p0, v_lo[win, :]) + _dot(p1, v_hi[win, :])
            den = acc[:, LANES:] + jnp.where(low, jnp.exp(sink0 - m0), jnp.exp(sink1 - m1))
            gate = _silu(z_ref[rws, OFF_AG + c:OFF_AG + c + LANES])
            mix_ref[rws, RET_WIDTH + c:RET_WIDTH + c + LANES] = (acc[:, :LANES] / den * gate).astype(BF16)
        return carry

    lax.fori_loop(0, nblk, block, 0)

    xo_ref[...] = _dot(mix_ref[...], wout_ref[...])
    fg_row = fg_ref[...]
    for s in range(nseq):
        gate_row = mod_ref[s][:, 2 * D_MODEL:3 * D_MODEL]

        def res_body(i, carry, s=s, gate_row=gate_row):
            r = pl.ds(pl.multiple_of(s * seg + i * nb, nb), nb)
            xn = x_ref[r, :] + gate_row * xo_ref[r, :]
            if cfg.last:
                ms = jnp.mean(xn * xn, axis=-1, keepdims=True)
                xn = xn * lax.rsqrt(ms + NORM_EPS) * fg_row
            xo_ref[r, :] = xn
            return carry

        lax.fori_loop(0, seg // nb, res_body, 0)

    if cfg.carried:
        for b in kv_bufs:
            b[0:WINDOW, :] = b[rows:rows + WINDOW, :]

        @pl.when(t == pl.num_programs(1) - 1)
        def _():
            ro_ref[...] = r_scr[...]
            ko_ref[...] = z_ref[rows - WINDOW:rows, OFF_AK:OFF_AK + KV_WIDTH]
            vo_ref[...] = z_ref[rows - WINDOW:rows, OFF_AV:OFF_AV + KV_WIDTH]
    else:
        for s in range(nseq):
            ko_ref[s, 0:WINDOW - seg, :] = cachek_ref[s, seg:WINDOW, :]
            vo_ref[s, 0:WINDOW - seg, :] = cachev_ref[s, seg:WINDOW, :]
            ko_ref[s, WINDOW - seg:WINDOW, :] = z_ref[s * seg:(s + 1) * seg, OFF_AK:OFF_AK + KV_WIDTH]
            vo_ref[s, WINDOW - seg:WINDOW, :] = z_ref[s * seg:(s + 1) * seg, OFF_AV:OFF_AV + KV_WIDTH]


def _scratch(cfg: _Cfg):
    krows = WINDOW + cfg.rows if cfg.carried else cfg.nseq * cfg.lk
    sc = [
        pltpu.VMEM((cfg.rows, IN_WIDTH), F32),
        pltpu.VMEM((cfg.rows, D_MODEL), BF16),
        pltpu.VMEM((cfg.rows, MIX_WIDTH), BF16),
        pltpu.VMEM((krows, LANES), BF16),
        pltpu.VMEM((krows, LANES), BF16),
        pltpu.VMEM((krows, 2 * LANES), BF16),
        pltpu.VMEM((krows, 2 * LANES), BF16),
        pltpu.VMEM((krows, 2 * LANES), BF16),
        pltpu.VMEM((krows, 2 * LANES), BF16),
    ]
    if cfg.carried:
        sc.append(pltpu.VMEM((RET_HEADS, RET_DK, RET_DV), F32))
    return sc


def _const_spec(shape, ngrid):
    nd = len(shape)
    if ngrid == 2:
        return pl.BlockSpec(shape, lambda b, t: (0,) * nd)
    return pl.BlockSpec(shape, lambda i: (0,) * nd)


def _smem_spec():
    return pl.BlockSpec(memory_space=pltpu.SMEM)


def _prompt_layer(layer, last, x2d, mod, g, w_in, w_out, tabs, dec, sink, final_g, batch, seq, rows):
    cfg = _Cfg(rows=rows, nseq=1, blk=WINDOW, lk=2 * WINDOW, carried=True, last=last, layer=layer)
    nt = seq // rows
    cq, sq, ck, sk = tabs
    decay, qdec, kdec, gl = dec
    tab_spec = pl.BlockSpec((rows, LANES), lambda b, t: (t, 0))
    in_specs = [
        pl.BlockSpec((rows, D_MODEL), lambda b, t: (b * nt + t, 0)),
        pl.BlockSpec((1, 1, 3 * D_MODEL), lambda b, t: (b, 0, 0)),
        _const_spec((1, D_MODEL), 2),
        _const_spec((D_MODEL, IN_WIDTH), 2),
        _const_spec((MIX_WIDTH, D_MODEL), 2),
        tab_spec, tab_spec, tab_spec, tab_spec,
        _const_spec(decay.shape, 2), _const_spec(qdec.shape, 2), _const_spec(kdec.shape, 2),
        _smem_spec(), _smem_spec(),
        _const_spec((1, D_MODEL), 2),
    ]
    out_specs = [
        pl.BlockSpec((rows, D_MODEL), lambda b, t: (b * nt + t, 0)),
        pl.BlockSpec((None, RET_HEADS, RET_DK, RET_DV), lambda b, t: (b, 0, 0, 0)),
        pl.BlockSpec((None, WINDOW, KV_WIDTH), lambda b, t: (b, 0, 0)),
        pl.BlockSpec((None, WINDOW, KV_WIDTH), lambda b, t: (b, 0, 0)),
    ]
    out_shape = [
        jax.ShapeDtypeStruct((batch * seq, D_MODEL), F32),
        jax.ShapeDtypeStruct((batch, RET_HEADS, RET_DK, RET_DV), F32),
        jax.ShapeDtypeStruct((batch, WINDOW, KV_WIDTH), F32),
        jax.ShapeDtypeStruct((batch, WINDOW, KV_WIDTH), F32),
    ]
    return pl.pallas_call(
        functools.partial(_layer_kernel, cfg=cfg),
        grid=(batch, nt),
        in_specs=in_specs, out_specs=out_specs, out_shape=out_shape,
        scratch_shapes=_scratch(cfg),
        compiler_params=pltpu.CompilerParams(
            dimension_semantics=("arbitrary", "arbitrary"),
            vmem_limit_bytes=VMEM_LIMIT_BYTES),
        name=f"prompt_layer{layer}",
    )(x2d, mod, g, w_in, w_out, cq, sq, ck, sk, decay, qdec, kdec, gl, sink, final_g)


def _sample_layer(layer, last, x2d, mod, g, w_in, w_out, tabs, dec, sink, final_g,
                  state, cache_k, cache_v, batch, seq, nseq, mod_row0):
    cfg = _Cfg(rows=nseq * seq, nseq=nseq, blk=seq, lk=WINDOW + seq, carried=False, last=last, layer=layer)
    cq, sq, ck, sk = tabs
    decay, qdec, kdec, gl = dec
    mod_blk0 = mod_row0 // nseq
    in_specs = [
        pl.BlockSpec((cfg.rows, D_MODEL), lambda i: (i, 0)),
        pl.BlockSpec((nseq, 1, 3 * D_MODEL), lambda i: (mod_blk0 + i, 0, 0)),
        _const_spec((1, D_MODEL), 1),
        _const_spec((D_MODEL, IN_WIDTH), 1),
        _const_spec((MIX_WIDTH, D_MODEL), 1),
        _const_spec(cq.shape, 1), _const_spec(sq.shape, 1), _const_spec(ck.shape, 1), _const_spec(sk.shape, 1),
        _const_spec(decay.shape, 1), _const_spec(qdec.shape, 1), _const_spec(kdec.shape, 1),
        _smem_spec(), _smem_spec(),
        _const_spec((1, D_MODEL), 1),
        pl.BlockSpec((None, nseq, RET_HEADS, RET_DK, RET_DV), lambda i: (layer, i, 0, 0, 0)),
        pl.BlockSpec((None, nseq, WINDOW, KV_WIDTH), lambda i: (layer, i, 0, 0)),
        pl.BlockSpec((None, nseq, WINDOW, KV_WIDTH), lambda i: (layer, i, 0, 0)),
    ]
    out_specs = [
        pl.BlockSpec((cfg.rows, D_MODEL), lambda i: (i, 0)),
        pl.BlockSpec((nseq, RET_HEADS, RET_DK, RET_DV), lambda i: (i, 0, 0, 0)),
        pl.BlockSpec((nseq, WINDOW, KV_WIDTH), lambda i: (i, 0, 0)),
        pl.BlockSpec((nseq, WINDOW, KV_WIDTH), lambda i: (i, 0, 0)),
    ]
    out_shape = [
        jax.ShapeDtypeStruct((batch * seq, D_MODEL), F32),
        jax.ShapeDtypeStruct((batch, RET_HEADS, RET_DK, RET_DV), F32),
        jax.ShapeDtypeStruct((batch, WINDOW, KV_WIDTH), F32),
        jax.ShapeDtypeStruct((batch, WINDOW, KV_WIDTH), F32),
    ]
    return pl.pallas_call(
        functools.partial(_layer_kernel, cfg=cfg),
        grid=(batch // nseq,),
        in_specs=in_specs, out_specs=out_specs, out_shape=out_shape,
        scratch_shapes=_scratch(cfg),
        compiler_params=pltpu.CompilerParams(
            dimension_semantics=("arbitrary",),
            vmem_limit_bytes=VMEM_LIMIT_BYTES),
        name=f"sample_layer{layer}",
    )(x2d, mod, g, w_in, w_out, cq, sq, ck, sk, decay, qdec, kdec, gl, sink, final_g,
      state, cache_k, cache_v)


def _rope_tables(pos):
    d = RET_DK
    inv = 1.0 / (ROPE_BASE ** (jnp.arange(0, d, 2, dtype=F32) / d))
    ang = pos.astype(F32)[:, None] * inv[None, :]
    cos, sin = jnp.cos(ang), jnp.sin(ang)
    cos2 = jnp.concatenate([cos, cos], axis=-1)
    sin2 = jnp.concatenate([-sin, sin], axis=-1)
    kscale = RET_DK ** -0.5
    return cos2, sin2, cos2 * kscale, sin2 * kscale


def _decay_tables(n):
    lg = jnp.log(1.0 - 2.0 ** (-5.0 - jnp.arange(RET_HEADS, dtype=F32)))
    idx = jnp.arange(n, dtype=F32)
    diff = idx[:, None] - idx[None, :]
    decay = jnp.where(diff[None] >= 0, jnp.exp(jnp.maximum(diff, 0.0)[None] * lg[:, None, None]), 0.0)
    qdec = jnp.exp((idx + 1.0)[None, :] * lg[:, None])
    kdec = jnp.exp((n - 1.0 - idx)[None, :] * lg[:, None])
    gl = jnp.exp(n * lg)
    bcast = lambda a: jnp.broadcast_to(a[:, :, None], (RET_HEADS, n, LANES))
    return decay, bcast(qdec), bcast(kdec), gl


def kernel(x_prompt, x_sample, c_prompt, c_sample, state_ret, cache_k, cache_v,
           norm_g, w_ada, b_ada, w_in, sink, w_out, final_g):
    batch, seq, _ = x_prompt.shape
    dbatch, dseq, _ = x_sample.shape

    c_all = jnp.concatenate([c_prompt, c_sample], axis=0)
    mod = _adaln(c_all, w_ada, b_ada).reshape(DEPTH, batch + dbatch, 1, 3 * D_MODEL)
    w_in_b = w_in.astype(BF16)
    w_out_b = w_out.astype(BF16)

    tabs_p = _rope_tables(jnp.arange(seq))
    tabs_s = _rope_tables(PAST_LEN + jnp.arange(dseq))
    dec_p = _decay_tables(WINDOW)
    dec_s = _decay_tables(dseq)
    ck4 = cache_k.reshape(DEPTH, dbatch, WINDOW, KV_WIDTH)
    cv4 = cache_v.reshape(DEPTH, dbatch, WINDOW, KV_WIDTH)
    fg = final_g.reshape(1, D_MODEL)

    xp = x_prompt.reshape(batch * seq, D_MODEL)
    xs = x_sample.reshape(dbatch * dseq, D_MODEL)
    rp, kp, vp, rs, ks, vs = [], [], [], [], [], []
    for l in range(DEPTH):
        last = l == DEPTH - 1
        g = norm_g[l].reshape(1, D_MODEL)
        xp, r1, k1, v1 = _prompt_layer(l, last, xp, mod[l], g, w_in_b[l], w_out_b[l], tabs_p, dec_p,
                                       sink, fg, batch, seq, rows=256)
        xs, r2, k2, v2 = _sample_layer(l, last, xs, mod[l], g, w_in_b[l], w_out_b[l], tabs_s, dec_s,
                                       sink, fg, state_ret, ck4, cv4, dbatch, dseq, nseq=4, mod_row0=batch)
        rp.append(r1); kp.append(k1); vp.append(v1)
        rs.append(r2); ks.append(k2); vs.append(v2)

    kv_shape_p = (DEPTH, batch, WINDOW, ATT_KV_HEADS, ATT_HEAD_DIM)
    kv_shape_s = (DEPTH, dbatch, WINDOW, ATT_KV_HEADS, ATT_HEAD_DIM)
    return (xp.reshape(batch, seq, D_MODEL), xs.reshape(dbatch, dseq, D_MODEL),
            jnp.stack(rp), jnp.stack(kp).reshape(kv_shape_p), jnp.stack(vp).reshape(kv_shape_p),
            jnp.stack(rs), jnp.stack(ks).reshape(kv_shape_s), jnp.stack(vs).reshape(kv_shape_s))
```

```python
import functools
from typing import NamedTuple

import jax
import jax.numpy as jnp
from jax import lax
from jax.experimental import pallas as pl
from jax.experimental.pallas import tpu as pltpu

D_MODEL = 1024
DEPTH = 4
CHUNK = 64
PAST_LEN = 4096
RET_HEADS = 4
RET_DK = 128
RET_DV = 128
RET_WIDTH = RET_HEADS * RET_DV
ATT_HEAD_DIM = 64
ATT_Q_HEADS = 8
ATT_KV_HEADS = 2
ATT_WIDTH = ATT_Q_HEADS * ATT_HEAD_DIM
KV_WIDTH = ATT_KV_HEADS * ATT_HEAD_DIM
WINDOW = 128
MIX_WIDTH = RET_WIDTH + ATT_WIDTH
ROPE_BASE = 10000.0
NORM_EPS = 1e-6
GN_EPS = 1e-5

OFF_RQ = 0
OFF_RK = OFF_RQ + RET_HEADS * RET_DK
OFF_RV = OFF_RK + RET_HEADS * RET_DK
OFF_RG = OFF_RV + RET_WIDTH
OFF_AQ = OFF_RG + RET_WIDTH
OFF_AK = OFF_AQ + ATT_WIDTH
OFF_AV = OFF_AK + KV_WIDTH
OFF_AG = OFF_AV + KV_WIDTH
IN_WIDTH = OFF_AG + ATT_WIDTH

LANES = 128
ATT_SLABS = ATT_WIDTH // LANES
MASKED = -1e30
VMEM_LIMIT_BYTES = 56 * 1024 * 1024
NORM_ROWS = 32
PROJ_COLS = 512

PROMPT_ROWS = 256
SAMPLE_SEQS = 4

BF16 = jnp.bfloat16
F32 = jnp.float32


def _silu(g):
    return g * (1.0 / (1.0 + jnp.exp(-g)))


def _dot(a, b):
    return jnp.dot(a, b, preferred_element_type=F32)


def _dot_nt(a, b):
    return lax.dot_general(a, b, (((1,), (1,)), ((), ())), preferred_element_type=F32)


def _low_lanes():
    lane = lax.broadcasted_iota(jnp.int32, (1, LANES), 1)
    return lane < ATT_HEAD_DIM


def _adaln_kernel(c_ref, w_ref, b_ref, o_ref):
    s = _silu(c_ref[...]).astype(BF16)
    o_ref[0] = _dot(s, w_ref[0].astype(BF16)) + b_ref[0]


def _adaln(c_all, w_ada, b_ada):
    n = c_all.shape[0]
    tn = D_MODEL
    return pl.pallas_call(
        _adaln_kernel,
        grid=(DEPTH, 3 * D_MODEL // tn),
        in_specs=[
            pl.BlockSpec((n, D_MODEL), lambda l, j: (0, 0)),
            pl.BlockSpec((1, D_MODEL, tn), lambda l, j: (l, 0, j)),
            pl.BlockSpec((1, 1, tn), lambda l, j: (l, 0, j)),
        ],
        out_specs=pl.BlockSpec((1, n, tn), lambda l, j: (l, 0, j)),
        out_shape=jax.ShapeDtypeStruct((DEPTH, n, 3 * D_MODEL), F32),
        compiler_params=pltpu.CompilerParams(
            dimension_semantics=("arbitrary", "arbitrary"),
            vmem_limit_bytes=VMEM_LIMIT_BYTES),
        name="adaln",
    )(c_all, w_ada, b_ada.reshape(DEPTH, 1, 3 * D_MODEL))


def _emit(items):
    for it in items:
        it()


def _interleave(a, b):
    out, nb = [], 0
    for i, it in enumerate(a):
        out.append(it)
        want = (i + 1) * len(b) // len(a)
        out.extend(b[nb:want])
        nb = want
    return out


def _norm_items(x_ref, h_ref, r0, n, g_ref, mod_ref, s):
    def item(a):
        mod_row = mod_ref[s]
        shift = mod_row[:, 0:D_MODEL]
        scale1 = 1.0 + mod_row[:, D_MODEL:2 * D_MODEL]
        xb = x_ref[a:a + NORM_ROWS, :]
        ms = jnp.mean(xb * xb, axis=-1, keepdims=True)
        y = xb * lax.rsqrt(ms + NORM_EPS) * g_ref[...]
        h_ref[a:a + NORM_ROWS, :] = (y * scale1 + shift).astype(BF16)
    return [functools.partial(item, a) for a in range(r0, r0 + n, NORM_ROWS)]


def _in_proj_items(h_ref, win_ref, z_ref):
    def item(c0):
        c1 = min(c0 + PROJ_COLS, IN_WIDTH)
        z_ref[:, c0:c1] = _dot(h_ref[...], win_ref[:, c0:c1])
    return [functools.partial(item, c0) for c0 in range(0, IN_WIDTH, PROJ_COLS)]


def _fill_kv(kv_bufs, dst0, n, k, v):
    kb_ref, kr_ref, va_ref, vb_ref, vc_ref, vd_ref = kv_bufs
    low = _low_lanes()
    kr = pltpu.roll(k, ATT_HEAD_DIM, 1)
    vr = pltpu.roll(v, ATT_HEAD_DIM, 1)
    d = slice(dst0, dst0 + n)
    kb_ref[d, :] = k.astype(BF16)
    kr_ref[d, :] = kr.astype(BF16)
    one_lo = jnp.broadcast_to(jnp.where(low, 1.0, 0.0), (n, LANES))
    one_hi = 1.0 - one_lo
    va_ref[d, :] = jnp.concatenate([jnp.where(low, v, 0.0), one_lo], axis=1).astype(BF16)
    vb_ref[d, :] = jnp.concatenate([jnp.where(low, 0.0, vr), one_hi], axis=1).astype(BF16)
    vc_ref[d, :] = jnp.concatenate([jnp.where(low, vr, 0.0), one_lo], axis=1).astype(BF16)
    vd_ref[d, :] = jnp.concatenate([jnp.where(low, 0.0, v), one_hi], axis=1).astype(BF16)


class _MixRefs(NamedTuple):
    z: object
    mix: object
    tabs: tuple
    dec: tuple
    sink: object
    kv_bufs: tuple


def _mix_items(m: _MixRefs, layer, r0, blk, trow0, win0, lk, r_get, r_set, valid_fn):
    z_ref, mix_ref = m.z, m.mix
    cq_ref, sq_ref, ck_ref, sk_ref = m.tabs
    decay_ref, qdec_ref, kdec_ref, gl_ref = m.dec
    kb_ref, kr_ref, va_ref, vb_ref, vc_ref, vd_ref = m.kv_bufs
    rws = slice(r0, r0 + blk)
    trs = slice(trow0, trow0 + blk)
    win = slice(win0, win0 + lk)

    def ret_head(hd):
        cq, sq, ck, sk = cq_ref[trs, :], sq_ref[trs, :], ck_ref[trs, :], sk_ref[trs, :]
        c = hd * RET_DK
        q = z_ref[rws, OFF_RQ + c:OFF_RQ + c + RET_DK]
        k = z_ref[rws, OFF_RK + c:OFF_RK + c + RET_DK]
        v = z_ref[rws, OFF_RV + c:OFF_RV + c + RET_DV].astype(BF16)
        q = q * cq + pltpu.roll(q, RET_DK // 2, 1) * sq
        k = k * ck + pltpu.roll(k, RET_DK // 2, 1) * sk
        s_in = _dot_nt(q.astype(BF16), k.astype(BF16)) * decay_ref[hd]
        r_old = r_get(hd)
        qd = (q * qdec_ref[hd]).astype(BF16)
        o = _dot(s_in.astype(BF16), v) + _dot(qd, r_old.astype(BF16))
        kd_t = (k * kdec_ref[hd]).T.astype(BF16)
        r_set(hd, gl_ref[hd] * r_old + _dot(kd_t, v))
        mu = jnp.mean(o, axis=-1, keepdims=True)
        oc = o - mu
        var = jnp.mean(oc * oc, axis=-1, keepdims=True)
        on = oc * lax.rsqrt(var + GN_EPS)
        gate = _silu(z_ref[rws, OFF_RG + c:OFF_RG + c + RET_DV])
        mix_ref[rws, c:c + RET_DV] = (on * gate).astype(BF16)

    def att_slab(sl):
        low = _low_lanes()
        valid = None if valid_fn is None else valid_fn()
        c = sl * LANES
        kv = (2 * sl) // (ATT_Q_HEADS // ATT_KV_HEADS)
        qf = z_ref[rws, OFF_AQ + c:OFF_AQ + c + LANES] * (ATT_HEAD_DIM ** -0.5)
        q_lo = jnp.where(low, qf, 0.0).astype(BF16)
        q_hi = jnp.where(low, 0.0, qf).astype(BF16)
        if kv == 0:
            k_lo, k_hi, v_lo, v_hi = kb_ref, kr_ref, va_ref, vb_ref
        else:
            k_lo, k_hi, v_lo, v_hi = kr_ref, kb_ref, vc_ref, vd_ref
        s0 = _dot_nt(q_lo, k_lo[win, :])
        s1 = _dot_nt(q_hi, k_hi[win, :])
        if valid is not None:
            s0 = jnp.where(valid, s0, MASKED)
            s1 = jnp.where(valid, s1, MASKED)
        sink0 = m.sink[layer, 2 * sl]
        sink1 = m.sink[layer, 2 * sl + 1]
        m0 = jnp.maximum(jnp.max(s0, axis=-1, keepdims=True), sink0)
        m1 = jnp.maximum(jnp.max(s1, axis=-1, keepdims=True), sink1)
        p0 = jnp.exp(s0 - m0).astype(BF16)
        p1 = jnp.exp(s1 - m1).astype(BF16)
        acc = _dot(p0, v_lo[win, :]) + _dot(p1, v_hi[win, :])
        den = acc[:, LANES:] + jnp.where(low, jnp.exp(sink0 - m0), jnp.exp(sink1 - m1))
        gate = _silu(z_ref[rws, OFF_AG + c:OFF_AG + c + LANES])
        mix_ref[rws, RET_WIDTH + c:RET_WIDTH + c + LANES] = (acc[:, :LANES] / den * gate).astype(BF16)

    return ([functools.partial(ret_head, hd) for hd in range(RET_HEADS)]
            + [functools.partial(att_slab, sl) for sl in range(ATT_SLABS)])


def _final_norm_rows(xo_ref, r0, n, fg_row):
    for a in range(r0, r0 + n, NORM_ROWS):
        xn = xo_ref[a:a + NORM_ROWS, :]
        ms = jnp.mean(xn * xn, axis=-1, keepdims=True)
        xo_ref[a:a + NORM_ROWS, :] = xn * lax.rsqrt(ms + NORM_EPS) * fg_row


class _PromptCfg(NamedTuple):
    rows: int
    nt: int
    last: bool
    layer: int


def _prompt_kernel(xn_ref, xc_ref, modn_ref, modc_ref, g_ref, win_ref, wout_ref,
                   cq_ref, sq_ref, ck_ref, sk_ref, decay_ref, qdec_ref, kdec_ref, gl_ref, sink_ref, fg_ref,
                   xo_ref, ro_ref, ko_ref, vo_ref,
                   za_ref, zb_ref, h_ref, mix_ref, kb_ref, kr_ref, va_ref, vb_ref, vc_ref, vd_ref, r_scr,
                   *, cfg: _PromptCfg):
    rows, nt = cfg.rows, cfg.nt
    blk, lk = WINDOW, 2 * WINDOW
    g = pl.program_id(0)
    t = jnp.maximum(g - 1, 0) % nt
    kv_bufs = (kb_ref, kr_ref, va_ref, vb_ref, vc_ref, vd_ref)

    @pl.when(g == 0)
    def _():
        zb_ref[...] = jnp.zeros(zb_ref.shape, F32)

    @pl.when(t == 0)
    def _():
        for b in kv_bufs:
            b[0:WINDOW, :] = jnp.zeros((WINDOW, b.shape[1]), BF16)
        r_scr[...] = jnp.zeros(r_scr.shape, F32)

    def r_set(hd, val):
        r_scr[hd] = val

    def band(first_block):
        row_i = lax.broadcasted_iota(jnp.int32, (blk, lk), 0)
        col_i = lax.broadcasted_iota(jnp.int32, (blk, lk), 1)
        first_key = jnp.where(row_i < CHUNK, 0, CHUNK)
        ok = (col_i >= first_key) & (col_i < first_key + WINDOW + CHUNK)
        if first_block:
            ok = ok & (col_i >= jnp.where(t == 0, WINDOW, 0))
        return ok

    def step(zw_ref, zr_ref):
        norm = _norm_items(xn_ref, h_ref, 0, rows, g_ref, modn_ref, 0)
        proj = _in_proj_items(h_ref, win_ref, zw_ref)

        for r0 in range(0, rows, CHUNK):
            _fill_kv(kv_bufs, WINDOW + r0, CHUNK, zr_ref[r0:r0 + CHUNK, OFF_AK:OFF_AK + KV_WIDTH],
                     zr_ref[r0:r0 + CHUNK, OFF_AV:OFF_AV + KV_WIDTH])
        m = _MixRefs(zr_ref, mix_ref, (cq_ref, sq_ref, ck_ref, sk_ref),
                     (decay_ref, qdec_ref, kdec_ref, gl_ref), sink_ref, kv_bufs)
        mix = []
        for j in range(rows // blk):
            mix += _mix_items(m, cfg.layer, j * blk, blk, j * blk, j * blk, lk,
                              lambda hd: r_scr[hd], r_set, functools.partial(band, j == 0))
        n_head = 2
        _emit(_interleave(mix[:n_head], norm))
        _emit(_interleave(mix[n_head:], proj))
        gate_row = modc_ref[0][:, 2 * D_MODEL:3 * D_MODEL]
        xo_ref[...] = xc_ref[...] + gate_row * _dot(mix_ref[...], wout_ref[...])
        if cfg.last:
            _final_norm_rows(xo_ref, 0, rows, fg_ref[...])
        @pl.when(t != nt - 1)
        def _():
            for b in kv_bufs:
                b[0:WINDOW, :] = b[rows:rows + WINDOW, :]

        @pl.when(t == nt - 1)
        def _():
            ro_ref[...] = r_scr[...]
            ko_ref[...] = zr_ref[rows - WINDOW:rows, OFF_AK:OFF_AK + KV_WIDTH]
            vo_ref[...] = zr_ref[rows - WINDOW:rows, OFF_AV:OFF_AV + KV_WIDTH]

    @pl.when(g % 2 == 0)
    def _():
        step(za_ref, zb_ref)

    @pl.when(g % 2 == 1)
    def _():
        step(zb_ref, za_ref)


def _kv_scratch(krows):
    return [
        pltpu.VMEM((krows, LANES), BF16),
        pltpu.VMEM((krows, LANES), BF16),
        pltpu.VMEM((krows, 2 * LANES), BF16),
        pltpu.VMEM((krows, 2 * LANES), BF16),
        pltpu.VMEM((krows, 2 * LANES), BF16),
        pltpu.VMEM((krows, 2 * LANES), BF16),
    ]


def _const_spec(shape):
    nd = len(shape)
    return pl.BlockSpec(shape, lambda g: (0,) * nd)


def _smem_spec():
    return pl.BlockSpec(memory_space=pltpu.SMEM)


def _prompt_layer(layer, last, x2d, mod, g, w_in, w_out, tabs, dec, sink, final_g, batch, seq):
    rows = PROMPT_ROWS
    nt = seq // rows
    ntiles = batch * nt
    cfg = _PromptCfg(rows=rows, nt=nt, last=last, layer=layer)
    cq, sq, ck, sk = tabs
    decay, qdec, kdec, gl = dec

    nxt = lambda g: jnp.minimum(g, ntiles - 1)
    cur = lambda g: jnp.maximum(g - 1, 0)
    tab_spec = pl.BlockSpec((rows, LANES), lambda g: (cur(g) % nt, 0))
    in_specs = [
        pl.BlockSpec((rows, D_MODEL), lambda g: (nxt(g), 0)),
        pl.BlockSpec((rows, D_MODEL), lambda g: (cur(g), 0)),
        pl.BlockSpec((1, 1, 3 * D_MODEL), lambda g: (nxt(g) // nt, 0, 0)),
        pl.BlockSpec((1, 1, 3 * D_MODEL), lambda g: (cur(g) // nt, 0, 0)),
        _const_spec((1, D_MODEL)),
        _const_spec((D_MODEL, IN_WIDTH)),
        _const_spec((MIX_WIDTH, D_MODEL)),
        tab_spec, tab_spec, tab_spec, tab_spec,
        _const_spec(decay.shape), _const_spec(qdec.shape), _const_spec(kdec.shape),
        _smem_spec(), _smem_spec(),
        _const_spec((1, D_MODEL)),
    ]
    out_specs = [
        pl.BlockSpec((rows, D_MODEL), lambda g: (cur(g), 0)),
        pl.BlockSpec((None, RET_HEADS, RET_DK, RET_DV), lambda g: (cur(g) // nt, 0, 0, 0)),
        pl.BlockSpec((None, WINDOW, KV_WIDTH), lambda g: (cur(g) // nt, 0, 0)),
        pl.BlockSpec((None, WINDOW, KV_WIDTH), lambda g: (cur(g) // nt, 0, 0)),
    ]
    out_shape = [
        jax.ShapeDtypeStruct((batch * seq, D_MODEL), F32),
        jax.ShapeDtypeStruct((batch, RET_HEADS, RET_DK, RET_DV), F32),
        jax.ShapeDtypeStruct((batch, WINDOW, KV_WIDTH), F32),
        jax.ShapeDtypeStruct((batch, WINDOW, KV_WIDTH), F32),
    ]
    scratch = [
        pltpu.VMEM((rows, IN_WIDTH), F32),
        pltpu.VMEM((rows, IN_WIDTH), F32),
        pltpu.VMEM((rows, D_MODEL), BF16),
        pltpu.VMEM((rows, MIX_WIDTH), BF16),
        *_kv_scratch(WINDOW + rows),
        pltpu.VMEM((RET_HEADS, RET_DK, RET_DV), F32),
    ]
    return pl.pallas_call(
        functools.partial(_prompt_kernel, cfg=cfg),
        grid=(ntiles + 1,),
        in_specs=in_specs, out_specs=out_specs, out_shape=out_shape,
        scratch_shapes=scratch,
        compiler_params=pltpu.CompilerParams(
            dimension_semantics=("arbitrary",),
            vmem_limit_bytes=VMEM_LIMIT_BYTES),
        name=f"prompt_layer{layer}",
    )(x2d, x2d, mod, mod, g, w_in, w_out, cq, sq, ck, sk, decay, qdec, kdec, gl, sink, final_g)


class _SampleCfg(NamedTuple):
    nseq: int
    seq: int
    last: bool
    layer: int


def _sample_kernel(x_ref, mod_ref, g_ref, win_ref, wout_ref,
                   cq_ref, sq_ref, ck_ref, sk_ref, decay_ref, qdec_ref, kdec_ref, gl_ref, sink_ref, fg_ref,
                   st_ref, cachek_ref, cachev_ref,
                   xo_ref, ro_ref, ko_ref, vo_ref,
                   z_ref, h_ref, mix_ref, kb_ref, kr_ref, va_ref, vb_ref, vc_ref, vd_ref,
                   *, cfg: _SampleCfg):
    nseq, seq = cfg.nseq, cfg.seq
    lk = WINDOW + seq
    kv_bufs = (kb_ref, kr_ref, va_ref, vb_ref, vc_ref, vd_ref)

    for s in range(nseq):
        _emit(_norm_items(x_ref, h_ref, s * seq, seq, g_ref, mod_ref, s))
    _emit(_in_proj_items(h_ref, win_ref, z_ref))

    m = _MixRefs(z_ref, mix_ref, (cq_ref, sq_ref, ck_ref, sk_ref),
                 (decay_ref, qdec_ref, kdec_ref, gl_ref), sink_ref, kv_bufs)
    for s in range(nseq):
        rws = slice(s * seq, (s + 1) * seq)
        k_new = z_ref[rws, OFF_AK:OFF_AK + KV_WIDTH]
        v_new = z_ref[rws, OFF_AV:OFF_AV + KV_WIDTH]
        _fill_kv(kv_bufs, s * lk, WINDOW, cachek_ref[s], cachev_ref[s])
        _fill_kv(kv_bufs, s * lk + WINDOW, seq, k_new, v_new)

        def r_set(hd, val, s=s):
            ro_ref[s, hd] = val

        _emit(_mix_items(m, cfg.layer, s * seq, seq, 0, s * lk, lk,
                         lambda hd, s=s: st_ref[s, hd], r_set, None))
        ko_ref[s, 0:WINDOW - seq, :] = cachek_ref[s, seq:WINDOW, :]
        vo_ref[s, 0:WINDOW - seq, :] = cachev_ref[s, seq:WINDOW, :]
        ko_ref[s, WINDOW - seq:WINDOW, :] = k_new
        vo_ref[s, WINDOW - seq:WINDOW, :] = v_new

    xo_ref[...] = _dot(mix_ref[...], wout_ref[...])
    for s in range(nseq):
        rws = slice(s * seq, (s + 1) * seq)
        gate_row = mod_ref[s][:, 2 * D_MODEL:3 * D_MODEL]
        xo_ref[rws, :] = x_ref[rws, :] + gate_row * xo_ref[rws, :]
    if cfg.last:
        _final_norm_rows(xo_ref, 0, nseq * seq, fg_ref[...])


def _sample_layer(layer, last, x2d, mod, g, w_in, w_out, tabs, dec, sink, final_g,
                  state, cache_k, cache_v, batch, seq, mod_row0):
    nseq = SAMPLE_SEQS
    rows = nseq * seq
    cfg = _SampleCfg(nseq=nseq, seq=seq, last=last, layer=layer)
    cq, sq, ck, sk = tabs
    decay, qdec, kdec, gl = dec
    mod_blk0 = mod_row0 // nseq
    in_specs = [
        pl.BlockSpec((rows, D_MODEL), lambda i: (i, 0)),
        pl.BlockSpec((nseq, 1, 3 * D_MODEL), lambda i: (mod_blk0 + i, 0, 0)),
        _const_spec((1, D_MODEL)),
        _const_spec((D_MODEL, IN_WIDTH)),
        _const_spec((MIX_WIDTH, D_MODEL)),
        _const_spec(cq.shape), _const_spec(sq.shape), _const_spec(ck.shape), _const_spec(sk.shape),
        _const_spec(decay.shape), _const_spec(qdec.shape), _const_spec(kdec.shape),
        _smem_spec(), _smem_spec(),
        _const_spec((1, D_MODEL)),
        pl.BlockSpec((None, nseq, RET_HEADS, RET_DK, RET_DV), lambda i: (layer, i, 0, 0, 0)),
        pl.BlockSpec((None, nseq, WINDOW, KV_WIDTH), lambda i: (layer, i, 0, 0)),
        pl.BlockSpec((None, nseq, WINDOW, KV_WIDTH), lambda i: (layer, i, 0, 0)),
    ]
    out_specs = [
        pl.BlockSpec((rows, D_MODEL), lambda i: (i, 0)),
        pl.BlockSpec((nseq, RET_HEADS, RET_DK, RET_DV), lambda i: (i, 0, 0, 0)),
        pl.BlockSpec((nseq, WINDOW, KV_WIDTH), lambda i: (i, 0, 0)),
        pl.BlockSpec((nseq, WINDOW, KV_WIDTH), lambda i: (i, 0, 0)),
    ]
    out_shape = [
        jax.ShapeDtypeStruct((batch * seq, D_MODEL), F32),
        jax.ShapeDtypeStruct((batch, RET_HEADS, RET_DK, RET_DV), F32),
        jax.ShapeDtypeStruct((batch, WINDOW, KV_WIDTH), F32),
        jax.ShapeDtypeStruct((batch, WINDOW, KV_WIDTH), F32),
    ]
    scratch = [
        pltpu.VMEM((rows, IN_WIDTH), F32),
        pltpu.VMEM((rows, D_MODEL), BF16),
        pltpu.VMEM((rows, MIX_WIDTH), BF16),
        *_kv_scratch(nseq * (WINDOW + seq)),
    ]
    return pl.pallas_call(
        functools.partial(_sample_kernel, cfg=cfg),
        grid=(batch // nseq,),
        in_specs=in_specs, out_specs=out_specs, out_shape=out_shape,
        scratch_shapes=scratch,
        compiler_params=pltpu.CompilerParams(
            dimension_semantics=("arbitrary",),
            vmem_limit_bytes=VMEM_LIMIT_BYTES),
        name=f"sample_layer{layer}",
    )(x2d, mod, g, w_in, w_out, cq, sq, ck, sk, decay, qdec, kdec, gl, sink, final_g,
      state, cache_k, cache_v)


def _rope_tables(pos):
    d = RET_DK
    inv = 1.0 / (ROPE_BASE ** (jnp.arange(0, d, 2, dtype=F32) / d))
    ang = pos.astype(F32)[:, None] * inv[None, :]
    cos, sin = jnp.cos(ang), jnp.sin(ang)
    cos2 = jnp.concatenate([cos, cos], axis=-1)
    sin2 = jnp.concatenate([-sin, sin], axis=-1)
    kscale = RET_DK ** -0.5
    return cos2, sin2, cos2 * kscale, sin2 * kscale


def _decay_tables(n):
    lg = jnp.log(1.0 - 2.0 ** (-5.0 - jnp.arange(RET_HEADS, dtype=F32)))
    idx = jnp.arange(n, dtype=F32)
    diff = idx[:, None] - idx[None, :]
    decay = jnp.where(diff[None] >= 0, jnp.exp(jnp.maximum(diff, 0.0)[None] * lg[:, None, None]), 0.0)
    qdec = jnp.exp((idx + 1.0)[None, :] * lg[:, None])
    kdec = jnp.exp((n - 1.0 - idx)[None, :] * lg[:, None])
    gl = jnp.exp(n * lg)
    bcast = lambda a: jnp.broadcast_to(a[:, :, None], (RET_HEADS, n, LANES))
    return decay, bcast(qdec), bcast(kdec), gl


def kernel(x_prompt, x_sample, c_prompt, c_sample, state_ret, cache_k, cache_v,
           norm_g, w_ada, b_ada, w_in, sink, w_out, final_g):
    batch, seq, _ = x_prompt.shape
    dbatch, dseq, _ = x_sample.shape

    c_all = jnp.concatenate([c_prompt, c_sample], axis=0)
    mod = _adaln(c_all, w_ada, b_ada).reshape(DEPTH, batch + dbatch, 1, 3 * D_MODEL)
    w_in_b = w_in.astype(BF16)
    w_out_b = w_out.astype(BF16)

    tabs_p = _rope_tables(jnp.arange(seq))
    tabs_s = _rope_tables(PAST_LEN + jnp.arange(dseq))
    dec_p = _decay_tables(WINDOW)
    dec_s = _decay_tables(dseq)
    ck4 = cache_k.reshape(DEPTH, dbatch, WINDOW, KV_WIDTH)
    cv4 = cache_v.reshape(DEPTH, dbatch, WINDOW, KV_WIDTH)
    fg = final_g.reshape(1, D_MODEL)

    xp = x_prompt.reshape(batch * seq, D_MODEL)
    xs = x_sample.reshape(dbatch * dseq, D_MODEL)
    rp, kp, vp, rs, ks, vs = [], [], [], [], [], []
    for l in range(DEPTH):
        last = l == DEPTH - 1
        g = norm_g[l].reshape(1, D_MODEL)
        xp, r1, k1, v1 = _prompt_layer(l, last, xp, mod[l], g, w_in_b[l], w_out_b[l], tabs_p, dec_p,
                                       sink, fg, batch, seq)
        xs, r2, k2, v2 = _sample_layer(l, last, xs, mod[l], g, w_in_b[l], w_out_b[l], tabs_s, dec_s,
                                       sink, fg, state_ret, ck4, cv4, dbatch, dseq, mod_row0=batch)
        rp.append(r1); kp.append(k1); vp.append(v1)
        rs.append(r2); ks.append(k2); vs.append(v2)

    kv_shape_p = (DEPTH, batch, WINDOW, ATT_KV_HEADS, ATT_HEAD_DIM)
    kv_shape_s = (DEPTH, dbatch, WINDOW, ATT_KV_HEADS, ATT_HEAD_DIM)
    return (xp.reshape(batch, seq, D_MODEL), xs.reshape(dbatch, dseq, D_MODEL),
            jnp.stack(rp), jnp.stack(kp).reshape(kv_shape_p), jnp.stack(vp).reshape(kv_shape_p),
            jnp.stack(rs), jnp.stack(ks).reshape(kv_shape_s), jnp.stack(vs).reshape(kv_shape_s))
```

```python
import functools
from typing import NamedTuple

import jax
import jax.numpy as jnp
from jax import lax
from jax.experimental import pallas as pl
from jax.experimental.pallas import tpu as pltpu

D_MODEL = 1024
DEPTH = 4
CHUNK = 64
PAST_LEN = 4096
RET_HEADS = 4
RET_DK = 128
RET_DV = 128
RET_WIDTH = RET_HEADS * RET_DV
ATT_HEAD_DIM = 64
ATT_Q_HEADS = 8
ATT_KV_HEADS = 2
ATT_WIDTH = ATT_Q_HEADS * ATT_HEAD_DIM
KV_WIDTH = ATT_KV_HEADS * ATT_HEAD_DIM
WINDOW = 128
MIX_WIDTH = RET_WIDTH + ATT_WIDTH
ROPE_BASE = 10000.0
NORM_EPS = 1e-6
GN_EPS = 1e-5

OFF_RQ = 0
OFF_RK = OFF_RQ + RET_HEADS * RET_DK
OFF_RV = OFF_RK + RET_HEADS * RET_DK
OFF_RG = OFF_RV + RET_WIDTH
OFF_AQ = OFF_RG + RET_WIDTH
OFF_AK = OFF_AQ + ATT_WIDTH
OFF_AV = OFF_AK + KV_WIDTH
OFF_AG = OFF_AV + KV_WIDTH
IN_WIDTH = OFF_AG + ATT_WIDTH

LANES = 128
ATT_SLABS = ATT_WIDTH // LANES
MASKED = -1e30
LOG2E = 1.4426950408889634
VMEM_LIMIT_BYTES = 56 * 1024 * 1024
NORM_ROWS = 32
PROJ_COLS = 512

PROMPT_ROWS = 256
SAMPLE_SEQS = 4

BF16 = jnp.bfloat16
F32 = jnp.float32


def _silu(g):
    return g * (1.0 / (1.0 + jnp.exp(-g)))


def _dot(a, b):
    return jnp.dot(a, b, preferred_element_type=F32)


def _dot_nt(a, b):
    return lax.dot_general(a, b, (((1,), (1,)), ((), ())), preferred_element_type=F32)


def _low_lanes():
    lane = lax.broadcasted_iota(jnp.int32, (1, LANES), 1)
    return lane < ATT_HEAD_DIM


def _adaln_kernel(c_ref, w_ref, b_ref, o_ref):
    s = _silu(c_ref[...]).astype(BF16)
    o_ref[0] = _dot(s, w_ref[0].astype(BF16)) + b_ref[0]


def _adaln(c_all, w_ada, b_ada):
    n = c_all.shape[0]
    tn = D_MODEL
    return pl.pallas_call(
        _adaln_kernel,
        grid=(DEPTH, 3 * D_MODEL // tn),
        in_specs=[
            pl.BlockSpec((n, D_MODEL), lambda l, j: (0, 0)),
            pl.BlockSpec((1, D_MODEL, tn), lambda l, j: (l, 0, j)),
            pl.BlockSpec((1, 1, tn), lambda l, j: (l, 0, j)),
        ],
        out_specs=pl.BlockSpec((1, n, tn), lambda l, j: (l, 0, j)),
        out_shape=jax.ShapeDtypeStruct((DEPTH, n, 3 * D_MODEL), F32),
        compiler_params=pltpu.CompilerParams(
            dimension_semantics=("arbitrary", "arbitrary"),
            vmem_limit_bytes=VMEM_LIMIT_BYTES),
        name="adaln",
    )(c_all, w_ada, b_ada.reshape(DEPTH, 1, 3 * D_MODEL))


def _emit(items):
    for it in items:
        it()


def _interleave(a, b):
    out, nb = [], 0
    for i, it in enumerate(a):
        out.append(it)
        want = (i + 1) * len(b) // len(a)
        out.extend(b[nb:want])
        nb = want
    return out


def _norm_items(x_ref, h_ref, r0, n, g_ref, mod_ref, s):
    def item(a):
        mod_row = mod_ref[s]
        shift = mod_row[:, 0:D_MODEL]
        scale1 = 1.0 + mod_row[:, D_MODEL:2 * D_MODEL]
        xb = x_ref[a:a + NORM_ROWS, :]
        ms = jnp.mean(xb * xb, axis=-1, keepdims=True)
        y = xb * lax.rsqrt(ms + NORM_EPS) * g_ref[...]
        h_ref[a:a + NORM_ROWS, :] = (y * scale1 + shift).astype(BF16)
    return [functools.partial(item, a) for a in range(r0, r0 + n, NORM_ROWS)]


def _in_proj_items(h_ref, win_ref, z_ref):
    def item(c0):
        c1 = min(c0 + PROJ_COLS, IN_WIDTH)
        z_ref[:, c0:c1] = _dot(h_ref[...], win_ref[:, c0:c1])
    return [functools.partial(item, c0) for c0 in range(0, IN_WIDTH, PROJ_COLS)]


def _fill_kv(kv_bufs, dst0, n, k, v):
    kb_ref, kr_ref, va_ref, vb_ref, vc_ref, vd_ref = kv_bufs
    low = _low_lanes()
    kr = pltpu.roll(k, ATT_HEAD_DIM, 1)
    vr = pltpu.roll(v, ATT_HEAD_DIM, 1)
    d = slice(dst0, dst0 + n)
    kb_ref[d, :] = k.astype(BF16)
    kr_ref[d, :] = kr.astype(BF16)
    one_lo = jnp.broadcast_to(jnp.where(low, 1.0, 0.0), (n, LANES))
    one_hi = 1.0 - one_lo
    va_ref[d, :] = jnp.concatenate([jnp.where(low, v, 0.0), one_lo], axis=1).astype(BF16)
    vb_ref[d, :] = jnp.concatenate([jnp.where(low, 0.0, vr), one_hi], axis=1).astype(BF16)
    vc_ref[d, :] = jnp.concatenate([jnp.where(low, vr, 0.0), one_lo], axis=1).astype(BF16)
    vd_ref[d, :] = jnp.concatenate([jnp.where(low, 0.0, v), one_hi], axis=1).astype(BF16)


class _MixRefs(NamedTuple):
    z: object
    mix: object
    tabs: tuple
    dec: tuple
    sink: object
    kv_bufs: tuple


def _mix_items(m: _MixRefs, layer, r0, blk, trow0, win0, lk, r_get, r_set, valid_fn):
    z_ref, mix_ref = m.z, m.mix
    cq_ref, sq_ref, ck_ref, sk_ref = m.tabs
    decay_ref, qdec_ref, kdec_ref, gl_ref = m.dec
    kb_ref, kr_ref, va_ref, vb_ref, vc_ref, vd_ref = m.kv_bufs
    rws = slice(r0, r0 + blk)
    trs = slice(trow0, trow0 + blk)
    win = slice(win0, win0 + lk)

    def ret_head(hd):
        cq, sq, ck, sk = cq_ref[trs, :], sq_ref[trs, :], ck_ref[trs, :], sk_ref[trs, :]
        c = hd * RET_DK
        q = z_ref[rws, OFF_RQ + c:OFF_RQ + c + RET_DK]
        k = z_ref[rws, OFF_RK + c:OFF_RK + c + RET_DK]
        v = z_ref[rws, OFF_RV + c:OFF_RV + c + RET_DV].astype(BF16)
        q = q * cq + pltpu.roll(q, RET_DK // 2, 1) * sq
        k = k * ck + pltpu.roll(k, RET_DK // 2, 1) * sk
        s_in = _dot_nt(q.astype(BF16), k.astype(BF16)) * decay_ref[hd]
        r_old = r_get(hd)
        qd = (q * qdec_ref[hd]).astype(BF16)
        kd_t = (k * kdec_ref[hd]).T.astype(BF16)
        o = _dot(s_in.astype(BF16), v) + _dot(qd, r_old.astype(BF16))
        r_set(hd, gl_ref[hd] * r_old + _dot(kd_t, v))
        mu = jnp.mean(o, axis=-1, keepdims=True)
        oc = o - mu
        var = jnp.mean(oc * oc, axis=-1, keepdims=True)
        on = oc * lax.rsqrt(var + GN_EPS)
        gate = _silu(z_ref[rws, OFF_RG + c:OFF_RG + c + RET_DV])
        mix_ref[rws, c:c + RET_DV] = (on * gate).astype(BF16)

    def att_slab(sl):
        low = _low_lanes()
        valid = None if valid_fn is None else valid_fn()
        c = sl * LANES
        kv = (2 * sl) // (ATT_Q_HEADS // ATT_KV_HEADS)
        qf = z_ref[rws, OFF_AQ + c:OFF_AQ + c + LANES] * (ATT_HEAD_DIM ** -0.5 * LOG2E)
        q_lo = jnp.where(low, qf, 0.0).astype(BF16)
        q_hi = jnp.where(low, 0.0, qf).astype(BF16)
        if kv == 0:
            k_lo, k_hi, v_lo, v_hi = kb_ref, kr_ref, va_ref, vb_ref
        else:
            k_lo, k_hi, v_lo, v_hi = kr_ref, kb_ref, vc_ref, vd_ref
        s0 = _dot_nt(q_lo, k_lo[win, :])
        s1 = _dot_nt(q_hi, k_hi[win, :])
        if valid is not None:
            s0 = jnp.where(valid, s0, MASKED)
            s1 = jnp.where(valid, s1, MASKED)
        sink0 = m.sink[layer, 2 * sl] * LOG2E
        sink1 = m.sink[layer, 2 * sl + 1] * LOG2E
        m0 = jnp.maximum(jnp.max(s0, axis=-1, keepdims=True), sink0)
        m1 = jnp.maximum(jnp.max(s1, axis=-1, keepdims=True), sink1)
        p0 = jnp.exp2(s0 - m0).astype(BF16)
        p1 = jnp.exp2(s1 - m1).astype(BF16)
        acc = _dot(p0, v_lo[win, :]) + _dot(p1, v_hi[win, :])
        den = acc[:, LANES:] + jnp.where(low, jnp.exp2(sink0 - m0), jnp.exp2(sink1 - m1))
        gate = _silu(z_ref[rws, OFF_AG + c:OFF_AG + c + LANES])
        mix_ref[rws, RET_WIDTH + c:RET_WIDTH + c + LANES] = (acc[:, :LANES] / den * gate).astype(BF16)

    return ([functools.partial(ret_head, hd) for hd in range(RET_HEADS)]
            + [functools.partial(att_slab, sl) for sl in range(ATT_SLABS)])


def _final_norm_rows(xo_ref, r0, n, fg_row):
    for a in range(r0, r0 + n, NORM_ROWS):
        xn = xo_ref[a:a + NORM_ROWS, :]
        ms = jnp.mean(xn * xn, axis=-1, keepdims=True)
        xo_ref[a:a + NORM_ROWS, :] = xn * lax.rsqrt(ms + NORM_EPS) * fg_row


class _PromptCfg(NamedTuple):
    rows: int
    nt: int
    last: bool
    layer: int


def _prompt_kernel(xn_ref, xc_ref, modn_ref, modc_ref, g_ref, win_ref, wout_ref,
                   cq_ref, sq_ref, ck_ref, sk_ref, decay_ref, qdec_ref, kdec_ref, gl_ref, sink_ref, fg_ref,
                   xo_ref, ro_ref, ko_ref, vo_ref,
                   za_ref, zb_ref, h_ref, mixa_ref, mixb_ref,
                   kb_ref, kr_ref, va_ref, vb_ref, vc_ref, vd_ref, r_scr,
                   *, cfg: _PromptCfg):
    rows, nt = cfg.rows, cfg.nt
    ntiles = pl.num_programs(0) - 2
    blk, lk = WINDOW, 2 * WINDOW
    g = pl.program_id(0)
    t = jnp.clip(g - 1, 0, ntiles - 1) % nt
    kv_bufs = (kb_ref, kr_ref, va_ref, vb_ref, vc_ref, vd_ref)

    @pl.when(g == 0)
    def _():
        zb_ref[...] = jnp.zeros(zb_ref.shape, F32)
        mixa_ref[...] = jnp.zeros(mixa_ref.shape, BF16)

    @pl.when(t == 0)
    def _():
        for b in kv_bufs:
            b[0:WINDOW, :] = jnp.zeros((WINDOW, b.shape[1]), BF16)
        r_scr[...] = jnp.zeros(r_scr.shape, F32)

    def r_set(hd, val):
        r_scr[hd] = val

    def band(first_block):
        row_i = lax.broadcasted_iota(jnp.int32, (blk, lk), 0)
        col_i = lax.broadcasted_iota(jnp.int32, (blk, lk), 1)
        first_key = jnp.where(row_i < CHUNK, 0, CHUNK)
        ok = (col_i >= first_key) & (col_i < first_key + WINDOW + CHUNK)
        if first_block:
            ok = ok & (col_i >= jnp.where(t == 0, WINDOW, 0))
        return ok

    def step(zw_ref, zr_ref, mixw_ref, mixr_ref):
        norm = _norm_items(xn_ref, h_ref, 0, rows, g_ref, modn_ref, 0)
        proj = _in_proj_items(h_ref, win_ref, zw_ref)

        def out_item(c0):
            cs = slice(c0, c0 + PROJ_COLS)
            gate_row = modc_ref[0][:, 2 * D_MODEL + c0:2 * D_MODEL + c0 + PROJ_COLS]
            xo_ref[:, cs] = xc_ref[:, cs] + gate_row * _dot(mixr_ref[...], wout_ref[:, cs])
        outp = [functools.partial(out_item, c0) for c0 in range(0, D_MODEL, PROJ_COLS)]

        for r0 in range(0, rows, CHUNK):
            _fill_kv(kv_bufs, WINDOW + r0, CHUNK, zr_ref[r0:r0 + CHUNK, OFF_AK:OFF_AK + KV_WIDTH],
                     zr_ref[r0:r0 + CHUNK, OFF_AV:OFF_AV + KV_WIDTH])
        m = _MixRefs(zr_ref, mixw_ref, (cq_ref, sq_ref, ck_ref, sk_ref),
                     (decay_ref, qdec_ref, kdec_ref, gl_ref), sink_ref, kv_bufs)
        mix = []
        for j in range(rows // blk):
            mix += _mix_items(m, cfg.layer, j * blk, blk, j * blk, j * blk, lk,
                              lambda hd: r_scr[hd], r_set, functools.partial(band, j == 0))

        _emit(_interleave(norm, outp))
        _emit(_interleave(mix, proj))
        if cfg.last:
            _final_norm_rows(xo_ref, 0, rows, fg_ref[...])

        @pl.when(t != nt - 1)
        def _():
            for b in kv_bufs:
                b[0:WINDOW, :] = b[rows:rows + WINDOW, :]

        @pl.when((t == nt - 1) & (g <= ntiles))
        def _():
            ro_ref[...] = r_scr[...]
            ko_ref[...] = zr_ref[rows - WINDOW:rows, OFF_AK:OFF_AK + KV_WIDTH]
            vo_ref[...] = zr_ref[rows - WINDOW:rows, OFF_AV:OFF_AV + KV_WIDTH]

    @pl.when(g % 2 == 0)
    def _():
        step(za_ref, zb_ref, mixb_ref, mixa_ref)

    @pl.when(g % 2 == 1)
    def _():
        step(zb_ref, za_ref, mixa_ref, mixb_ref)


def _kv_scratch(krows):
    return [
        pltpu.VMEM((krows, LANES), BF16),
        pltpu.VMEM((krows, LANES), BF16),
        pltpu.VMEM((krows, 2 * LANES), BF16),
        pltpu.VMEM((krows, 2 * LANES), BF16),
        pltpu.VMEM((krows, 2 * LANES), BF16),
        pltpu.VMEM((krows, 2 * LANES), BF16),
    ]


def _const_spec(shape):
    nd = len(shape)
    return pl.BlockSpec(shape, lambda g: (0,) * nd)


def _layer_spec(shape, layer):
    nd = len(shape)
    return pl.BlockSpec((None, *shape), lambda g: (layer,) + (0,) * nd)


def _smem_spec():
    return pl.BlockSpec(memory_space=pltpu.SMEM)


def _prompt_layer(layer, last, x2d, mod, g, w_in, w_out, tabs, dec, sink, final_g, batch, seq):
    rows = PROMPT_ROWS
    nt = seq // rows
    ntiles = batch * nt
    cfg = _PromptCfg(rows=rows, nt=nt, last=last, layer=layer)
    cq, sq, ck, sk = tabs
    decay, qdec, kdec, gl = dec

    nxt = lambda g: jnp.minimum(g, ntiles - 1)
    cur = lambda g: jnp.clip(g - 1, 0, ntiles - 1)
    fin = lambda g: jnp.maximum(g - 2, 0)
    tab_spec = pl.BlockSpec((rows, LANES), lambda g: (cur(g) % nt, 0))
    in_specs = [
        pl.BlockSpec((rows, D_MODEL), lambda g: (nxt(g), 0)),
        pl.BlockSpec((rows, D_MODEL), lambda g: (fin(g), 0)),
        pl.BlockSpec((None, 1, 1, 3 * D_MODEL), lambda g: (layer, nxt(g) // nt, 0, 0)),
        pl.BlockSpec((None, 1, 1, 3 * D_MODEL), lambda g: (layer, fin(g) // nt, 0, 0)),
        _layer_spec((1, D_MODEL), layer),
        _layer_spec((D_MODEL, IN_WIDTH), layer),
        _layer_spec((MIX_WIDTH, D_MODEL), layer),
        tab_spec, tab_spec, tab_spec, tab_spec,
        _const_spec(decay.shape), _const_spec(qdec.shape), _const_spec(kdec.shape),
        _smem_spec(), _smem_spec(),
        _const_spec((1, D_MODEL)),
    ]
    out_specs = [
        pl.BlockSpec((rows, D_MODEL), lambda g: (fin(g), 0)),
        pl.BlockSpec((None, RET_HEADS, RET_DK, RET_DV), lambda g: (cur(g) // nt, 0, 0, 0)),
        pl.BlockSpec((None, WINDOW, KV_WIDTH), lambda g: (cur(g) // nt, 0, 0)),
        pl.BlockSpec((None, WINDOW, KV_WIDTH), lambda g: (cur(g) // nt, 0, 0)),
    ]
    out_shape = [
        jax.ShapeDtypeStruct((batch * seq, D_MODEL), F32),
        jax.ShapeDtypeStruct((batch, RET_HEADS, RET_DK, RET_DV), F32),
        jax.ShapeDtypeStruct((batch, WINDOW, KV_WIDTH), F32),
        jax.ShapeDtypeStruct((batch, WINDOW, KV_WIDTH), F32),
    ]
    scratch = [
        pltpu.VMEM((rows, IN_WIDTH), F32),
        pltpu.VMEM((rows, IN_WIDTH), F32),
        pltpu.VMEM((rows, D_MODEL), BF16),
        pltpu.VMEM((rows, MIX_WIDTH), BF16),
        pltpu.VMEM((rows, MIX_WIDTH), BF16),
        *_kv_scratch(WINDOW + rows),
        pltpu.VMEM((RET_HEADS, RET_DK, RET_DV), F32),
    ]
    return pl.pallas_call(
        functools.partial(_prompt_kernel, cfg=cfg),
        grid=(ntiles + 2,),
        in_specs=in_specs, out_specs=out_specs, out_shape=out_shape,
        scratch_shapes=scratch,
        compiler_params=pltpu.CompilerParams(
            dimension_semantics=("arbitrary",),
            vmem_limit_bytes=VMEM_LIMIT_BYTES),
        name=f"prompt_layer{layer}",
    )(x2d, x2d, mod, mod, g, w_in, w_out, cq, sq, ck, sk, decay, qdec, kdec, gl, sink, final_g)


class _SampleCfg(NamedTuple):
    nseq: int
    seq: int
    last: bool
    layer: int


def _sample_kernel(x_ref, mod_ref, g_ref, win_ref, wout_ref,
                   cq_ref, sq_ref, ck_ref, sk_ref, decay_ref, qdec_ref, kdec_ref, gl_ref, sink_ref, fg_ref,
                   st_ref, cachek_ref, cachev_ref,
                   xo_ref, ro_ref, ko_ref, vo_ref,
                   z_ref, h_ref, mix_ref, kb_ref, kr_ref, va_ref, vb_ref, vc_ref, vd_ref,
                   *, cfg: _SampleCfg):
    nseq, seq = cfg.nseq, cfg.seq
    lk = WINDOW + seq
    kv_bufs = (kb_ref, kr_ref, va_ref, vb_ref, vc_ref, vd_ref)

    for s in range(nseq):
        _emit(_norm_items(x_ref, h_ref, s * seq, seq, g_ref, mod_ref, s))
    _emit(_in_proj_items(h_ref, win_ref, z_ref))

    m = _MixRefs(z_ref, mix_ref, (cq_ref, sq_ref, ck_ref, sk_ref),
                 (decay_ref, qdec_ref, kdec_ref, gl_ref), sink_ref, kv_bufs)
    for s in range(nseq):
        rws = slice(s * seq, (s + 1) * seq)
        k_new = z_ref[rws, OFF_AK:OFF_AK + KV_WIDTH]
        v_new = z_ref[rws, OFF_AV:OFF_AV + KV_WIDTH]
        _fill_kv(kv_bufs, s * lk, WINDOW, cachek_ref[s], cachev_ref[s])
        _fill_kv(kv_bufs, s * lk + WINDOW, seq, k_new, v_new)

        def r_set(hd, val, s=s):
            ro_ref[s, hd] = val

        _emit(_mix_items(m, cfg.layer, s * seq, seq, 0, s * lk, lk,
                         lambda hd, s=s: st_ref[s, hd], r_set, None))
        ko_ref[s, 0:WINDOW - seq, :] = cachek_ref[s, seq:WINDOW, :]
        vo_ref[s, 0:WINDOW - seq, :] = cachev_ref[s, seq:WINDOW, :]
        ko_ref[s, WINDOW - seq:WINDOW, :] = k_new
        vo_ref[s, WINDOW - seq:WINDOW, :] = v_new

    xo_ref[...] = _dot(mix_ref[...], wout_ref[...])
    for s in range(nseq):
        rws = slice(s * seq, (s + 1) * seq)
        gate_row = mod_ref[s][:, 2 * D_MODEL:3 * D_MODEL]
        xo_ref[rws, :] = x_ref[rws, :] + gate_row * xo_ref[rws, :]
    if cfg.last:
        _final_norm_rows(xo_ref, 0, nseq * seq, fg_ref[...])


def _sample_layer(layer, last, x2d, mod, g, w_in, w_out, tabs, dec, sink, final_g,
                  state, cache_k, cache_v, batch, seq, mod_row0):
    nseq = SAMPLE_SEQS
    rows = nseq * seq
    cfg = _SampleCfg(nseq=nseq, seq=seq, last=last, layer=layer)
    cq, sq, ck, sk = tabs
    decay, qdec, kdec, gl = dec
    mod_blk0 = mod_row0 // nseq
    in_specs = [
        pl.BlockSpec((rows, D_MODEL), lambda i: (i, 0)),
        pl.BlockSpec((None, nseq, 1, 3 * D_MODEL), lambda i: (layer, mod_blk0 + i, 0, 0)),
        _layer_spec((1, D_MODEL), layer),
        _layer_spec((D_MODEL, IN_WIDTH), layer),
        _layer_spec((MIX_WIDTH, D_MODEL), layer),
        _const_spec(cq.shape), _const_spec(sq.shape), _const_spec(ck.shape), _const_spec(sk.shape),
        _const_spec(decay.shape), _const_spec(qdec.shape), _const_spec(kdec.shape),
        _smem_spec(), _smem_spec(),
        _const_spec((1, D_MODEL)),
        pl.BlockSpec((None, nseq, RET_HEADS, RET_DK, RET_DV), lambda i: (layer, i, 0, 0, 0)),
        pl.BlockSpec((None, nseq, WINDOW, KV_WIDTH), lambda i: (layer, i, 0, 0)),
        pl.BlockSpec((None, nseq, WINDOW, KV_WIDTH), lambda i: (layer, i, 0, 0)),
    ]
    out_specs = [
        pl.BlockSpec((rows, D_MODEL), lambda i: (i, 0)),
        pl.BlockSpec((nseq, RET_HEADS, RET_DK, RET_DV), lambda i: (i, 0, 0, 0)),
        pl.BlockSpec((nseq, WINDOW, KV_WIDTH), lambda i: (i, 0, 0)),
        pl.BlockSpec((nseq, WINDOW, KV_WIDTH), lambda i: (i, 0, 0)),
    ]
    out_shape = [
        jax.ShapeDtypeStruct((batch * seq, D_MODEL), F32),
        jax.ShapeDtypeStruct((batch, RET_HEADS, RET_DK, RET_DV), F32),
        jax.ShapeDtypeStruct((batch, WINDOW, KV_WIDTH), F32),
        jax.ShapeDtypeStruct((batch, WINDOW, KV_WIDTH), F32),
    ]
    scratch = [
        pltpu.VMEM((rows, IN_WIDTH), F32),
        pltpu.VMEM((rows, D_MODEL), BF16),
        pltpu.VMEM((rows, MIX_WIDTH), BF16),
        *_kv_scratch(nseq * (WINDOW + seq)),
    ]
    return pl.pallas_call(
        functools.partial(_sample_kernel, cfg=cfg),
        grid=(batch // nseq,),
        in_specs=in_specs, out_specs=out_specs, out_shape=out_shape,
        scratch_shapes=scratch,
        compiler_params=pltpu.CompilerParams(
            dimension_semantics=("arbitrary",),
            vmem_limit_bytes=VMEM_LIMIT_BYTES),
        name=f"sample_layer{layer}",
    )(x2d, mod, g, w_in, w_out, cq, sq, ck, sk, decay, qdec, kdec, gl, sink, final_g,
      state, cache_k, cache_v)


def _rope_tables(pos):
    d = RET_DK
    inv = 1.0 / (ROPE_BASE ** (jnp.arange(0, d, 2, dtype=F32) / d))
    ang = pos.astype(F32)[:, None] * inv[None, :]
    cos, sin = jnp.cos(ang), jnp.sin(ang)
    cos2 = jnp.concatenate([cos, cos], axis=-1)
    sin2 = jnp.concatenate([-sin, sin], axis=-1)
    kscale = RET_DK ** -0.5
    return cos2, sin2, cos2 * kscale, sin2 * kscale


def _decay_tables(n):
    lg = jnp.log(1.0 - 2.0 ** (-5.0 - jnp.arange(RET_HEADS, dtype=F32)))
    idx = jnp.arange(n, dtype=F32)
    diff = idx[:, None] - idx[None, :]
    decay = jnp.where(diff[None] >= 0, jnp.exp(jnp.maximum(diff, 0.0)[None] * lg[:, None, None]), 0.0)
    qdec = jnp.exp((idx + 1.0)[None, :] * lg[:, None])
    kdec = jnp.exp((n - 1.0 - idx)[None, :] * lg[:, None])
    gl = jnp.exp(n * lg)
    bcast = lambda a: jnp.broadcast_to(a[:, :, None], (RET_HEADS, n, LANES))
    return decay, bcast(qdec), bcast(kdec), gl


def kernel(x_prompt, x_sample, c_prompt, c_sample, state_ret, cache_k, cache_v,
           norm_g, w_ada, b_ada, w_in, sink, w_out, final_g):
    batch, seq, _ = x_prompt.shape
    dbatch, dseq, _ = x_sample.shape

    c_all = jnp.concatenate([c_prompt, c_sample], axis=0)
    mod = _adaln(c_all, w_ada, b_ada).reshape(DEPTH, batch + dbatch, 1, 3 * D_MODEL)
    w_in_b = w_in.astype(BF16)
    w_out_b = w_out.astype(BF16)

    tabs_p = _rope_tables(jnp.arange(seq))
    tabs_s = _rope_tables(PAST_LEN + jnp.arange(dseq))
    dec_p = _decay_tables(WINDOW)
    dec_s = _decay_tables(dseq)
    ck4 = cache_k.reshape(DEPTH, dbatch, WINDOW, KV_WIDTH)
    cv4 = cache_v.reshape(DEPTH, dbatch, WINDOW, KV_WIDTH)
    fg = final_g.reshape(1, D_MODEL)
    g_all = norm_g.reshape(DEPTH, 1, D_MODEL)

    xp = x_prompt.reshape(batch * seq, D_MODEL)
    xs = x_sample.reshape(dbatch * dseq, D_MODEL)
    rp, kp, vp, rs, ks, vs = [], [], [], [], [], []
    for l in range(DEPTH):
        last = l == DEPTH - 1
        xp, r1, k1, v1 = _prompt_layer(l, last, xp, mod, g_all, w_in_b, w_out_b, tabs_p, dec_p,
                                       sink, fg, batch, seq)
        xs, r2, k2, v2 = _sample_layer(l, last, xs, mod, g_all, w_in_b, w_out_b, tabs_s, dec_s,
                                       sink, fg, state_ret, ck4, cv4, dbatch, dseq, mod_row0=batch)
        rp.append(r1); kp.append(k1); vp.append(v1)
        rs.append(r2); ks.append(k2); vs.append(v2)

    kv_shape_p = (DEPTH, batch, WINDOW, ATT_KV_HEADS, ATT_HEAD_DIM)
    kv_shape_s = (DEPTH, dbatch, WINDOW, ATT_KV_HEADS, ATT_HEAD_DIM)
    return (xp.reshape(batch, seq, D_MODEL), xs.reshape(dbatch, dseq, D_MODEL),
            jnp.stack(rp), jnp.stack(kp).reshape(kv_shape_p), jnp.stack(vp).reshape(kv_shape_p),
            jnp.stack(rs), jnp.stack(ks).reshape(kv_shape_s), jnp.stack(vs).reshape(kv_shape_s))
```

```python
import functools
from typing import NamedTuple

import jax
import jax.numpy as jnp
from jax import lax
from jax.experimental import pallas as pl
from jax.experimental.pallas import tpu as pltpu

D_MODEL = 1024
DEPTH = 4
CHUNK = 64
PAST_LEN = 4096
RET_HEADS = 4
RET_DK = 128
RET_DV = 128
RET_WIDTH = RET_HEADS * RET_DV
ATT_HEAD_DIM = 64
ATT_Q_HEADS = 8
ATT_KV_HEADS = 2
ATT_WIDTH = ATT_Q_HEADS * ATT_HEAD_DIM
KV_WIDTH = ATT_KV_HEADS * ATT_HEAD_DIM
WINDOW = 128
MIX_WIDTH = RET_WIDTH + ATT_WIDTH
ROPE_BASE = 10000.0
NORM_EPS = 1e-6
GN_EPS = 1e-5

OFF_RQ = 0
OFF_RK = OFF_RQ + RET_HEADS * RET_DK
OFF_RV = OFF_RK + RET_HEADS * RET_DK
OFF_RG = OFF_RV + RET_WIDTH
OFF_AQ = OFF_RG + RET_WIDTH
OFF_AK = OFF_AQ + ATT_WIDTH
OFF_AV = OFF_AK + KV_WIDTH
OFF_AG = OFF_AV + KV_WIDTH
IN_WIDTH = OFF_AG + ATT_WIDTH

LANES = 128
ATT_SLABS = ATT_WIDTH // LANES
MASKED = -1e30
LOG2E = 1.4426950408889634
VMEM_LIMIT_BYTES = 56 * 1024 * 1024
NORM_ROWS = 32
PROJ_COLS = 512

PROMPT_ROWS = 512
SAMPLE_SEQS = 4

BF16 = jnp.bfloat16
F32 = jnp.float32


def _silu(g):
    return g * (1.0 / (1.0 + jnp.exp(-g)))


def _dot(a, b):
    return jnp.dot(a, b, preferred_element_type=F32)


def _dot_nt(a, b):
    return lax.dot_general(a, b, (((1,), (1,)), ((), ())), preferred_element_type=F32)


def _low_lanes():
    lane = lax.broadcasted_iota(jnp.int32, (1, LANES), 1)
    return lane < ATT_HEAD_DIM


def _adaln_kernel(c_ref, w_ref, b_ref, o_ref):
    s = _silu(c_ref[...]).astype(BF16)
    o_ref[0] = _dot(s, w_ref[0].astype(BF16)) + b_ref[0]


def _adaln(c_all, w_ada, b_ada):
    n = c_all.shape[0]
    tn = D_MODEL
    return pl.pallas_call(
        _adaln_kernel,
        grid=(DEPTH, 3 * D_MODEL // tn),
        in_specs=[
            pl.BlockSpec((n, D_MODEL), lambda l, j: (0, 0)),
            pl.BlockSpec((1, D_MODEL, tn), lambda l, j: (l, 0, j)),
            pl.BlockSpec((1, 1, tn), lambda l, j: (l, 0, j)),
        ],
        out_specs=pl.BlockSpec((1, n, tn), lambda l, j: (l, 0, j)),
        out_shape=jax.ShapeDtypeStruct((DEPTH, n, 3 * D_MODEL), F32),
        compiler_params=pltpu.CompilerParams(
            dimension_semantics=("arbitrary", "arbitrary"),
            vmem_limit_bytes=VMEM_LIMIT_BYTES),
        name="adaln",
    )(c_all, w_ada, b_ada.reshape(DEPTH, 1, 3 * D_MODEL))


def _emit(items):
    for it in items:
        it()


def _interleave(a, b):
    out, nb = [], 0
    for i, it in enumerate(a):
        out.append(it)
        want = (i + 1) * len(b) // len(a)
        out.extend(b[nb:want])
        nb = want
    return out


def _norm_items(x_ref, h_ref, r0, n, g_ref, mod_ref, s):
    def item(a):
        mod_row = mod_ref[s]
        shift = mod_row[:, 0:D_MODEL]
        scale1 = 1.0 + mod_row[:, D_MODEL:2 * D_MODEL]
        xb = x_ref[a:a + NORM_ROWS, :]
        ms = jnp.mean(xb * xb, axis=-1, keepdims=True)
        y = xb * lax.rsqrt(ms + NORM_EPS) * g_ref[...]
        h_ref[a:a + NORM_ROWS, :] = (y * scale1 + shift).astype(BF16)
    return [functools.partial(item, a) for a in range(r0, r0 + n, NORM_ROWS)]


def _in_proj_items(h_ref, win_ref, z_ref):
    def item(c0):
        c1 = min(c0 + PROJ_COLS, IN_WIDTH)
        z_ref[:, c0:c1] = _dot(h_ref[...], win_ref[:, c0:c1])
    return [functools.partial(item, c0) for c0 in range(0, IN_WIDTH, PROJ_COLS)]


def _fill_kv(kv_bufs, dst0, n, k, v):
    kb_ref, kr_ref, va_ref, vb_ref, vc_ref, vd_ref = kv_bufs
    low = _low_lanes()
    kr = pltpu.roll(k, ATT_HEAD_DIM, 1)
    vr = pltpu.roll(v, ATT_HEAD_DIM, 1)
    d = slice(dst0, dst0 + n)
    kb_ref[d, :] = k.astype(BF16)
    kr_ref[d, :] = kr.astype(BF16)
    one_lo = jnp.broadcast_to(jnp.where(low, 1.0, 0.0), (n, LANES))
    one_hi = 1.0 - one_lo
    va_ref[d, :] = jnp.concatenate([jnp.where(low, v, 0.0), one_lo], axis=1).astype(BF16)
    vb_ref[d, :] = jnp.concatenate([jnp.where(low, 0.0, vr), one_hi], axis=1).astype(BF16)
    vc_ref[d, :] = jnp.concatenate([jnp.where(low, vr, 0.0), one_lo], axis=1).astype(BF16)
    vd_ref[d, :] = jnp.concatenate([jnp.where(low, 0.0, v), one_hi], axis=1).astype(BF16)


class _MixRefs(NamedTuple):
    z: object
    mix: object
    tabs: tuple
    dec: tuple
    sink: object
    kv_bufs: tuple


def _mix_items(m: _MixRefs, layer, r0, blk, trow0, win0, lk, r_get, r_set, valid_fn):
    z_ref, mix_ref = m.z, m.mix
    cq_ref, sq_ref, ck_ref, sk_ref = m.tabs
    decay_ref, qdec_ref, kdec_ref, gl_ref = m.dec
    kb_ref, kr_ref, va_ref, vb_ref, vc_ref, vd_ref = m.kv_bufs
    rws = slice(r0, r0 + blk)
    trs = slice(trow0, trow0 + blk)
    win = slice(win0, win0 + lk)

    def ret_head(hd):
        cq, sq, ck, sk = cq_ref[trs, :], sq_ref[trs, :], ck_ref[trs, :], sk_ref[trs, :]
        c = hd * RET_DK
        q = z_ref[rws, OFF_RQ + c:OFF_RQ + c + RET_DK]
        k = z_ref[rws, OFF_RK + c:OFF_RK + c + RET_DK]
        v = z_ref[rws, OFF_RV + c:OFF_RV + c + RET_DV].astype(BF16)
        q = q * cq + pltpu.roll(q, RET_DK // 2, 1) * sq
        k = k * ck + pltpu.roll(k, RET_DK // 2, 1) * sk
        s_in = _dot_nt(q.astype(BF16), k.astype(BF16)) * decay_ref[hd]
        r_old = r_get(hd)
        qd = (q * qdec_ref[hd]).astype(BF16)
        kd_t = (k * kdec_ref[hd]).T.astype(BF16)
        o = _dot(s_in.astype(BF16), v) + _dot(qd, r_old.astype(BF16))
        r_set(hd, gl_ref[hd] * r_old + _dot(kd_t, v))
        mu = jnp.mean(o, axis=-1, keepdims=True)
        oc = o - mu
        var = jnp.mean(oc * oc, axis=-1, keepdims=True)
        on = oc * lax.rsqrt(var + GN_EPS)
        gate = _silu(z_ref[rws, OFF_RG + c:OFF_RG + c + RET_DV])
        mix_ref[rws, c:c + RET_DV] = (on * gate).astype(BF16)

    def att_slab(sl):
        low = _low_lanes()
        valid = None if valid_fn is None else valid_fn()
        c = sl * LANES
        kv = (2 * sl) // (ATT_Q_HEADS // ATT_KV_HEADS)
        qf = z_ref[rws, OFF_AQ + c:OFF_AQ + c + LANES] * (ATT_HEAD_DIM ** -0.5 * LOG2E)
        q_lo = jnp.where(low, qf, 0.0).astype(BF16)
        q_hi = jnp.where(low, 0.0, qf).astype(BF16)
        if kv == 0:
            k_lo, k_hi, v_lo, v_hi = kb_ref, kr_ref, va_ref, vb_ref
        else:
            k_lo, k_hi, v_lo, v_hi = kr_ref, kb_ref, vc_ref, vd_ref
        s0 = _dot_nt(q_lo, k_lo[win, :])
        s1 = _dot_nt(q_hi, k_hi[win, :])
        if valid is not None:
            s0 = jnp.where(valid, s0, MASKED)
            s1 = jnp.where(valid, s1, MASKED)
        sink0 = m.sink[layer, 2 * sl] * LOG2E
        sink1 = m.sink[layer, 2 * sl + 1] * LOG2E
        m0 = jnp.maximum(jnp.max(s0, axis=-1, keepdims=True), sink0)
        m1 = jnp.maximum(jnp.max(s1, axis=-1, keepdims=True), sink1)
        p0 = jnp.exp2(s0 - m0).astype(BF16)
        p1 = jnp.exp2(s1 - m1).astype(BF16)
        acc = _dot(p0, v_lo[win, :]) + _dot(p1, v_hi[win, :])
        den = acc[:, LANES:] + jnp.where(low, jnp.exp2(sink0 - m0), jnp.exp2(sink1 - m1))
        gate = _silu(z_ref[rws, OFF_AG + c:OFF_AG + c + LANES])
        mix_ref[rws, RET_WIDTH + c:RET_WIDTH + c + LANES] = (acc[:, :LANES] / den * gate).astype(BF16)

    return ([functools.partial(ret_head, hd) for hd in range(RET_HEADS)]
            + [functools.partial(att_slab, sl) for sl in range(ATT_SLABS)])


def _final_norm_rows(xo_ref, r0, n, fg_row):
    for a in range(r0, r0 + n, NORM_ROWS):
        xn = xo_ref[a:a + NORM_ROWS, :]
        ms = jnp.mean(xn * xn, axis=-1, keepdims=True)
        xo_ref[a:a + NORM_ROWS, :] = xn * lax.rsqrt(ms + NORM_EPS) * fg_row


class _PromptCfg(NamedTuple):
    rows: int
    nt: int
    last: bool
    layer: int


def _prompt_kernel(xn_ref, xc_ref, modn_ref, modc_ref, g_ref, win_ref, wout_ref,
                   cq_ref, sq_ref, ck_ref, sk_ref, decay_ref, qdec_ref, kdec_ref, gl_ref, sink_ref, fg_ref,
                   xo_ref, ro_ref, ko_ref, vo_ref,
                   za_ref, zb_ref, h_ref, mixa_ref, mixb_ref,
                   kb_ref, kr_ref, va_ref, vb_ref, vc_ref, vd_ref, r_scr,
                   *, cfg: _PromptCfg):
    rows, nt = cfg.rows, cfg.nt
    ntiles = pl.num_programs(0) - 2
    blk, lk = WINDOW, 2 * WINDOW
    g = pl.program_id(0)
    t = jnp.clip(g - 1, 0, ntiles - 1) % nt
    kv_bufs = (kb_ref, kr_ref, va_ref, vb_ref, vc_ref, vd_ref)

    @pl.when(g == 0)
    def _():
        zb_ref[...] = jnp.zeros(zb_ref.shape, F32)
        mixa_ref[...] = jnp.zeros(mixa_ref.shape, BF16)

    @pl.when(t == 0)
    def _():
        for b in kv_bufs:
            b[0:WINDOW, :] = jnp.zeros((WINDOW, b.shape[1]), BF16)
        r_scr[...] = jnp.zeros(r_scr.shape, F32)

    def r_set(hd, val):
        r_scr[hd] = val

    def band(first_block):
        row_i = lax.broadcasted_iota(jnp.int32, (blk, lk), 0)
        col_i = lax.broadcasted_iota(jnp.int32, (blk, lk), 1)
        first_key = jnp.where(row_i < CHUNK, 0, CHUNK)
        ok = (col_i >= first_key) & (col_i < first_key + WINDOW + CHUNK)
        if first_block:
            ok = ok & (col_i >= jnp.where(t == 0, WINDOW, 0))
        return ok

    def step(zw_ref, zr_ref, mixw_ref, mixr_ref):
        norm = _norm_items(xn_ref, h_ref, 0, rows, g_ref, modn_ref, 0)
        proj = _in_proj_items(h_ref, win_ref, zw_ref)

        def out_item(c0):
            cs = slice(c0, c0 + PROJ_COLS)
            gate_row = modc_ref[0][:, 2 * D_MODEL + c0:2 * D_MODEL + c0 + PROJ_COLS]
            xo_ref[:, cs] = xc_ref[:, cs] + gate_row * _dot(mixr_ref[...], wout_ref[:, cs])
        outp = [functools.partial(out_item, c0) for c0 in range(0, D_MODEL, PROJ_COLS)]

        for r0 in range(0, rows, CHUNK):
            _fill_kv(kv_bufs, WINDOW + r0, CHUNK, zr_ref[r0:r0 + CHUNK, OFF_AK:OFF_AK + KV_WIDTH],
                     zr_ref[r0:r0 + CHUNK, OFF_AV:OFF_AV + KV_WIDTH])
        m = _MixRefs(zr_ref, mixw_ref, (cq_ref, sq_ref, ck_ref, sk_ref),
                     (decay_ref, qdec_ref, kdec_ref, gl_ref), sink_ref, kv_bufs)
        mix = []
        for j in range(rows // blk):
            mix += _mix_items(m, cfg.layer, j * blk, blk, j * blk, j * blk, lk,
                              lambda hd: r_scr[hd], r_set, functools.partial(band, j == 0))

        _emit(_interleave(norm, outp))
        _emit(_interleave(mix, proj))
        if cfg.last:
            _final_norm_rows(xo_ref, 0, rows, fg_ref[...])

        @pl.when(t != nt - 1)
        def _():
            for b in kv_bufs:
                b[0:WINDOW, :] = b[rows:rows + WINDOW, :]

        @pl.when((t == nt - 1) & (g <= ntiles))
        def _():
            ro_ref[...] = r_scr[...]
            ko_ref[...] = zr_ref[rows - WINDOW:rows, OFF_AK:OFF_AK + KV_WIDTH]
            vo_ref[...] = zr_ref[rows - WINDOW:rows, OFF_AV:OFF_AV + KV_WIDTH]

    @pl.when(g % 2 == 0)
    def _():
        step(za_ref, zb_ref, mixb_ref, mixa_ref)

    @pl.when(g % 2 == 1)
    def _():
        step(zb_ref, za_ref, mixa_ref, mixb_ref)


def _kv_scratch(krows):
    return [
        pltpu.VMEM((krows, LANES), BF16),
        pltpu.VMEM((krows, LANES), BF16),
        pltpu.VMEM((krows, 2 * LANES), BF16),
        pltpu.VMEM((krows, 2 * LANES), BF16),
        pltpu.VMEM((krows, 2 * LANES), BF16),
        pltpu.VMEM((krows, 2 * LANES), BF16),
    ]


def _const_spec(shape):
    nd = len(shape)
    return pl.BlockSpec(shape, lambda g: (0,) * nd)


def _layer_spec(shape, layer):
    nd = len(shape)
    return pl.BlockSpec((None, *shape), lambda g: (layer,) + (0,) * nd)


def _smem_spec():
    return pl.BlockSpec(memory_space=pltpu.SMEM)


def _prompt_layer(layer, last, x2d, mod, g, w_in, w_out, tabs, dec, sink, final_g, batch, seq):
    rows = PROMPT_ROWS
    nt = seq // rows
    ntiles = batch * nt
    cfg = _PromptCfg(rows=rows, nt=nt, last=last, layer=layer)
    cq, sq, ck, sk = tabs
    decay, qdec, kdec, gl = dec

    nxt = lambda g: jnp.minimum(g, ntiles - 1)
    cur = lambda g: jnp.clip(g - 1, 0, ntiles - 1)
    fin = lambda g: jnp.maximum(g - 2, 0)
    tab_spec = pl.BlockSpec((rows, LANES), lambda g: (cur(g) % nt, 0))
    in_specs = [
        pl.BlockSpec((rows, D_MODEL), lambda g: (nxt(g), 0)),
        pl.BlockSpec((rows, D_MODEL), lambda g: (fin(g), 0)),
        pl.BlockSpec((None, 1, 1, 3 * D_MODEL), lambda g: (layer, nxt(g) // nt, 0, 0)),
        pl.BlockSpec((None, 1, 1, 3 * D_MODEL), lambda g: (layer, fin(g) // nt, 0, 0)),
        _layer_spec((1, D_MODEL), layer),
        _layer_spec((D_MODEL, IN_WIDTH), layer),
        _layer_spec((MIX_WIDTH, D_MODEL), layer),
        tab_spec, tab_spec, tab_spec, tab_spec,
        _const_spec(decay.shape), _const_spec(qdec.shape), _const_spec(kdec.shape),
        _smem_spec(), _smem_spec(),
        _const_spec((1, D_MODEL)),
    ]
    out_specs = [
        pl.BlockSpec((rows, D_MODEL), lambda g: (fin(g), 0)),
        pl.BlockSpec((None, RET_HEADS, RET_DK, RET_DV), lambda g: (cur(g) // nt, 0, 0, 0)),
        pl.BlockSpec((None, WINDOW, KV_WIDTH), lambda g: (cur(g) // nt, 0, 0)),
        pl.BlockSpec((None, WINDOW, KV_WIDTH), lambda g: (cur(g) // nt, 0, 0)),
    ]
    out_shape = [
        jax.ShapeDtypeStruct((batch * seq, D_MODEL), F32),
        jax.ShapeDtypeStruct((batch, RET_HEADS, RET_DK, RET_DV), F32),
        jax.ShapeDtypeStruct((batch, WINDOW, KV_WIDTH), F32),
        jax.ShapeDtypeStruct((batch, WINDOW, KV_WIDTH), F32),
    ]
    scratch = [
        pltpu.VMEM((rows, IN_WIDTH), F32),
        pltpu.VMEM((rows, IN_WIDTH), F32),
        pltpu.VMEM((rows, D_MODEL), BF16),
        pltpu.VMEM((rows, MIX_WIDTH), BF16),
        pltpu.VMEM((rows, MIX_WIDTH), BF16),
        *_kv_scratch(WINDOW + rows),
        pltpu.VMEM((RET_HEADS, RET_DK, RET_DV), F32),
    ]
    return pl.pallas_call(
        functools.partial(_prompt_kernel, cfg=cfg),
        grid=(ntiles + 2,),
        in_specs=in_specs, out_specs=out_specs, out_shape=out_shape,
        scratch_shapes=scratch,
        compiler_params=pltpu.CompilerParams(
            dimension_semantics=("arbitrary",),
            vmem_limit_bytes=VMEM_LIMIT_BYTES),
        name=f"prompt_layer{layer}",
    )(x2d, x2d, mod, mod, g, w_in, w_out, cq, sq, ck, sk, decay, qdec, kdec, gl, sink, final_g)


class _SampleCfg(NamedTuple):
    nseq: int
    seq: int
    last: bool
    layer: int


def _sample_kernel(x_ref, mod_ref, g_ref, win_ref, wout_ref,
                   cq_ref, sq_ref, ck_ref, sk_ref, decay_ref, qdec_ref, kdec_ref, gl_ref, sink_ref, fg_ref,
                   st_ref, cachek_ref, cachev_ref,
                   xo_ref, ro_ref, ko_ref, vo_ref,
                   z_ref, h_ref, mix_ref, kb_ref, kr_ref, va_ref, vb_ref, vc_ref, vd_ref,
                   *, cfg: _SampleCfg):
    nseq, seq = cfg.nseq, cfg.seq
    lk = WINDOW + seq
    kv_bufs = (kb_ref, kr_ref, va_ref, vb_ref, vc_ref, vd_ref)

    for s in range(nseq):
        _emit(_norm_items(x_ref, h_ref, s * seq, seq, g_ref, mod_ref, s))
    _emit(_in_proj_items(h_ref, win_ref, z_ref))

    m = _MixRefs(z_ref, mix_ref, (cq_ref, sq_ref, ck_ref, sk_ref),
                 (decay_ref, qdec_ref, kdec_ref, gl_ref), sink_ref, kv_bufs)
    for s in range(nseq):
        rws = slice(s * seq, (s + 1) * seq)
        k_new = z_ref[rws, OFF_AK:OFF_AK + KV_WIDTH]
        v_new = z_ref[rws, OFF_AV:OFF_AV + KV_WIDTH]
        _fill_kv(kv_bufs, s * lk, WINDOW, cachek_ref[s], cachev_ref[s])
        _fill_kv(kv_bufs, s * lk + WINDOW, seq, k_new, v_new)

        def r_set(hd, val, s=s):
            ro_ref[s, hd] = val

        _emit(_mix_items(m, cfg.layer, s * seq, seq, 0, s * lk, lk,
                         lambda hd, s=s: st_ref[s, hd], r_set, None))
        ko_ref[s, 0:WINDOW - seq, :] = cachek_ref[s, seq:WINDOW, :]
        vo_ref[s, 0:WINDOW - seq, :] = cachev_ref[s, seq:WINDOW, :]
        ko_ref[s, WINDOW - seq:WINDOW, :] = k_new
        vo_ref[s, WINDOW - seq:WINDOW, :] = v_new

    xo_ref[...] = _dot(mix_ref[...], wout_ref[...])
    for s in range(nseq):
        rws = slice(s * seq, (s + 1) * seq)
        gate_row = mod_ref[s][:, 2 * D_MODEL:3 * D_MODEL]
        xo_ref[rws, :] = x_ref[rws, :] + gate_row * xo_ref[rws, :]
    if cfg.last:
        _final_norm_rows(xo_ref, 0, nseq * seq, fg_ref[...])


def _sample_layer(layer, last, x2d, mod, g, w_in, w_out, tabs, dec, sink, final_g,
                  state, cache_k, cache_v, batch, seq, mod_row0):
    nseq = SAMPLE_SEQS
    rows = nseq * seq
    cfg = _SampleCfg(nseq=nseq, seq=seq, last=last, layer=layer)
    cq, sq, ck, sk = tabs
    decay, qdec, kdec, gl = dec
    mod_blk0 = mod_row0 // nseq
    in_specs = [
        pl.BlockSpec((rows, D_MODEL), lambda i: (i, 0)),
        pl.BlockSpec((None, nseq, 1, 3 * D_MODEL), lambda i: (layer, mod_blk0 + i, 0, 0)),
        _layer_spec((1, D_MODEL), layer),
        _layer_spec((D_MODEL, IN_WIDTH), layer),
        _layer_spec((MIX_WIDTH, D_MODEL), layer),
        _const_spec(cq.shape), _const_spec(sq.shape), _const_spec(ck.shape), _const_spec(sk.shape),
        _const_spec(decay.shape), _const_spec(qdec.shape), _const_spec(kdec.shape),
        _smem_spec(), _smem_spec(),
        _const_spec((1, D_MODEL)),
        pl.BlockSpec((None, nseq, RET_HEADS, RET_DK, RET_DV), lambda i: (layer, i, 0, 0, 0)),
        pl.BlockSpec((None, nseq, WINDOW, KV_WIDTH), lambda i: (layer, i, 0, 0)),
        pl.BlockSpec((None, nseq, WINDOW, KV_WIDTH), lambda i: (layer, i, 0, 0)),
    ]
    out_specs = [
        pl.BlockSpec((rows, D_MODEL), lambda i: (i, 0)),
        pl.BlockSpec((nseq, RET_HEADS, RET_DK, RET_DV), lambda i: (i, 0, 0, 0)),
        pl.BlockSpec((nseq, WINDOW, KV_WIDTH), lambda i: (i, 0, 0)),
        pl.BlockSpec((nseq, WINDOW, KV_WIDTH), lambda i: (i, 0, 0)),
    ]
    out_shape = [
        jax.ShapeDtypeStruct((batch * seq, D_MODEL), F32),
        jax.ShapeDtypeStruct((batch, RET_HEADS, RET_DK, RET_DV), F32),
        jax.ShapeDtypeStruct((batch, WINDOW, KV_WIDTH), F32),
        jax.ShapeDtypeStruct((batch, WINDOW, KV_WIDTH), F32),
    ]
    scratch = [
        pltpu.VMEM((rows, IN_WIDTH), F32),
        pltpu.VMEM((rows, D_MODEL), BF16),
        pltpu.VMEM((rows, MIX_WIDTH), BF16),
        *_kv_scratch(nseq * (WINDOW + seq)),
    ]
    return pl.pallas_call(
        functools.partial(_sample_kernel, cfg=cfg),
        grid=(batch // nseq,),
        in_specs=in_specs, out_specs=out_specs, out_shape=out_shape,
        scratch_shapes=scratch,
        compiler_params=pltpu.CompilerParams(
            dimension_semantics=("arbitrary",),
            vmem_limit_bytes=VMEM_LIMIT_BYTES),
        name=f"sample_layer{layer}",
    )(x2d, mod, g, w_in, w_out, cq, sq, ck, sk, decay, qdec, kdec, gl, sink, final_g,
      state, cache_k, cache_v)


def _rope_tables(pos):
    d = RET_DK
    inv = 1.0 / (ROPE_BASE ** (jnp.arange(0, d, 2, dtype=F32) / d))
    ang = pos.astype(F32)[:, None] * inv[None, :]
    cos, sin = jnp.cos(ang), jnp.sin(ang)
    cos2 = jnp.concatenate([cos, cos], axis=-1)
    sin2 = jnp.concatenate([-sin, sin], axis=-1)
    kscale = RET_DK ** -0.5
    return cos2, sin2, cos2 * kscale, sin2 * kscale


def _decay_tables(n):
    lg = jnp.log(1.0 - 2.0 ** (-5.0 - jnp.arange(RET_HEADS, dtype=F32)))
    idx = jnp.arange(n, dtype=F32)
    diff = idx[:, None] - idx[None, :]
    decay = jnp.where(diff[None] >= 0, jnp.exp(jnp.maximum(diff, 0.0)[None] * lg[:, None, None]), 0.0)
    qdec = jnp.exp((idx + 1.0)[None, :] * lg[:, None])
    kdec = jnp.exp((n - 1.0 - idx)[None, :] * lg[:, None])
    gl = jnp.exp(n * lg)
    bcast = lambda a: jnp.broadcast_to(a[:, :, None], (RET_HEADS, n, LANES))
    return decay, bcast(qdec), bcast(kdec), gl


def kernel(x_prompt, x_sample, c_prompt, c_sample, state_ret, cache_k, cache_v,
           norm_g, w_ada, b_ada, w_in, sink, w_out, final_g):
    batch, seq, _ = x_prompt.shape
    dbatch, dseq, _ = x_sample.shape

    c_all = jnp.concatenate([c_prompt, c_sample], axis=0)
    mod = _adaln(c_all, w_ada, b_ada).reshape(DEPTH, batch + dbatch, 1, 3 * D_MODEL)
    w_in_b = w_in.astype(BF16)
    w_out_b = w_out.astype(BF16)

    tabs_p = _rope_tables(jnp.arange(seq))
    tabs_s = _rope_tables(PAST_LEN + jnp.arange(dseq))
    dec_p = _decay_tables(WINDOW)
    dec_s = _decay_tables(dseq)
    ck4 = cache_k.reshape(DEPTH, dbatch, WINDOW, KV_WIDTH)
    cv4 = cache_v.reshape(DEPTH, dbatch, WINDOW, KV_WIDTH)
    fg = final_g.reshape(1, D_MODEL)
    g_all = norm_g.reshape(DEPTH, 1, D_MODEL)

    xp = x_prompt.reshape(batch * seq, D_MODEL)
    xs = x_sample.reshape(dbatch * dseq, D_MODEL)
    rp, kp, vp, rs, ks, vs = [], [], [], [], [], []
    for l in range(DEPTH):
        last = l == DEPTH - 1
        xp, r1, k1, v1 = _prompt_layer(l, last, xp, mod, g_all, w_in_b, w_out_b, tabs_p, dec_p,
                                       sink, fg, batch, seq)
        xs, r2, k2, v2 = _sample_layer(l, last, xs, mod, g_all, w_in_b, w_out_b, tabs_s, dec_s,
                                       sink, fg, state_ret, ck4, cv4, dbatch, dseq, mod_row0=batch)
        rp.append(r1); kp.append(k1); vp.append(v1)
        rs.append(r2); ks.append(k2); vs.append(v2)

    kv_shape_p = (DEPTH, batch, WINDOW, ATT_KV_HEADS, ATT_HEAD_DIM)
    kv_shape_s = (DEPTH, dbatch, WINDOW, ATT_KV_HEADS, ATT_HEAD_DIM)
    return (xp.reshape(batch, seq, D_MODEL), xs.reshape(dbatch, dseq, D_MODEL),
            jnp.stack(rp), jnp.stack(kp).reshape(kv_shape_p), jnp.stack(vp).reshape(kv_shape_p),
            jnp.stack(rs), jnp.stack(ks).reshape(kv_shape_s), jnp.stack(vs).reshape(kv_shape_s))
```

```python
import functools
from typing import NamedTuple

import jax
import jax.numpy as jnp
from jax import lax
from jax.experimental import pallas as pl
from jax.experimental.pallas import tpu as pltpu

D_MODEL = 1024
DEPTH = 4
CHUNK = 64
PAST_LEN = 4096
RET_HEADS = 4
RET_DK = 128
RET_DV = 128
RET_WIDTH = RET_HEADS * RET_DV
ATT_HEAD_DIM = 64
ATT_Q_HEADS = 8
ATT_KV_HEADS = 2
ATT_WIDTH = ATT_Q_HEADS * ATT_HEAD_DIM
KV_WIDTH = ATT_KV_HEADS * ATT_HEAD_DIM
WINDOW = 128
MIX_WIDTH = RET_WIDTH + ATT_WIDTH
ROPE_BASE = 10000.0
NORM_EPS = 1e-6
GN_EPS = 1e-5

OFF_RQ = 0
OFF_RK = OFF_RQ + RET_HEADS * RET_DK
OFF_RV = OFF_RK + RET_HEADS * RET_DK
OFF_RG = OFF_RV + RET_WIDTH
OFF_AQ = OFF_RG + RET_WIDTH
OFF_AK = OFF_AQ + ATT_WIDTH
OFF_AV = OFF_AK + KV_WIDTH
OFF_AG = OFF_AV + KV_WIDTH
IN_WIDTH = OFF_AG + ATT_WIDTH

LANES = 128
ATT_SLABS = ATT_WIDTH // LANES
MASKED = -1e30
LOG2E = 1.4426950408889634
VMEM_LIMIT_BYTES = 56 * 1024 * 1024
NORM_ROWS = 32
PROJ_COLS = 512

PROMPT_ROWS = 256
PROMPT_BLOCK = 128
SAMPLE_SEQS = 4
SAMPLE_KEYS = 2 * LANES

BF16 = jnp.bfloat16
F32 = jnp.float32


def _silu(g):
    return g * (1.0 / (1.0 + jnp.exp(-g)))


def _dot(a, b):
    return jnp.dot(a, b, preferred_element_type=F32)


def _dot_nt(a, b):
    return lax.dot_general(a, b, (((1,), (1,)), ((), ())), preferred_element_type=F32)


def _low_lanes():
    lane = lax.broadcasted_iota(jnp.int32, (1, LANES), 1)
    return lane < ATT_HEAD_DIM


def _adaln_kernel(c_ref, w_ref, b_ref, o_ref):
    s = _silu(c_ref[...]).astype(BF16)
    o_ref[0] = _dot(s, w_ref[0].astype(BF16)) + b_ref[0]


def _adaln(c_all, w_ada, b_ada):
    n = c_all.shape[0]
    tn = D_MODEL
    return pl.pallas_call(
        _adaln_kernel,
        grid=(DEPTH, 3 * D_MODEL // tn),
        in_specs=[
            pl.BlockSpec((n, D_MODEL), lambda l, j: (0, 0)),
            pl.BlockSpec((1, D_MODEL, tn), lambda l, j: (l, 0, j)),
            pl.BlockSpec((1, 1, tn), lambda l, j: (l, 0, j)),
        ],
        out_specs=pl.BlockSpec((1, n, tn), lambda l, j: (l, 0, j)),
        out_shape=jax.ShapeDtypeStruct((DEPTH, n, 3 * D_MODEL), F32),
        compiler_params=pltpu.CompilerParams(
            dimension_semantics=("arbitrary", "arbitrary"),
            vmem_limit_bytes=VMEM_LIMIT_BYTES),
        name="adaln",
    )(c_all, w_ada, b_ada.reshape(DEPTH, 1, 3 * D_MODEL))


def _emit(items):
    for it in items:
        it()


def _interleave(a, b):
    out, nb = [], 0
    for i, it in enumerate(a):
        out.append(it)
        want = (i + 1) * len(b) // len(a)
        out.extend(b[nb:want])
        nb = want
    return out


def _norm_items(x_ref, h_ref, r0, n, g_ref, mod_ref, s):
    def item(a):
        mod_row = mod_ref[s]
        shift = mod_row[:, 0:D_MODEL]
        scale1 = 1.0 + mod_row[:, D_MODEL:2 * D_MODEL]
        xb = x_ref[a:a + NORM_ROWS, :]
        ms = jnp.mean(xb * xb, axis=-1, keepdims=True)
        y = xb * lax.rsqrt(ms + NORM_EPS) * g_ref[...]
        h_ref[a:a + NORM_ROWS, :] = (y * scale1 + shift).astype(BF16)
    return [functools.partial(item, a) for a in range(r0, r0 + n, NORM_ROWS)]


def _in_proj_items(h_ref, win_ref, z_ref):
    def item(c0):
        c1 = min(c0 + PROJ_COLS, IN_WIDTH)
        z_ref[:, c0:c1] = _dot(h_ref[...], win_ref[:, c0:c1])
    return [functools.partial(item, c0) for c0 in range(0, IN_WIDTH, PROJ_COLS)]


def _fill_kv(kv_bufs, dst0, n, k, v):
    ka_ref, kb_ref, kc_ref, kd_ref, va_ref, vb_ref, vc_ref, vd_ref = kv_bufs
    low = _low_lanes()
    kr = pltpu.roll(k, ATT_HEAD_DIM, 1)
    vr = pltpu.roll(v, ATT_HEAD_DIM, 1)
    d = slice(dst0, dst0 + n)
    ka_ref[d, :] = jnp.where(low, k, 0.0).astype(BF16)
    kb_ref[d, :] = jnp.where(low, 0.0, kr).astype(BF16)
    kc_ref[d, :] = jnp.where(low, kr, 0.0).astype(BF16)
    kd_ref[d, :] = jnp.where(low, 0.0, k).astype(BF16)
    one_lo = jnp.broadcast_to(jnp.where(low, 1.0, 0.0), (n, LANES))
    one_hi = 1.0 - one_lo
    va_ref[d, :] = jnp.concatenate([jnp.where(low, v, 0.0), one_lo], axis=1).astype(BF16)
    vb_ref[d, :] = jnp.concatenate([jnp.where(low, 0.0, vr), one_hi], axis=1).astype(BF16)
    vc_ref[d, :] = jnp.concatenate([jnp.where(low, vr, 0.0), one_lo], axis=1).astype(BF16)
    vd_ref[d, :] = jnp.concatenate([jnp.where(low, 0.0, v), one_hi], axis=1).astype(BF16)


class _MixRefs(NamedTuple):
    z: object
    mix: object
    tabs: tuple
    dec: tuple
    sink: object
    kv_bufs: tuple


def _mix_items(m: _MixRefs, layer, r0, blk, trow0, win0, lk, r_get, r_set, valid_fn):
    z_ref, mix_ref = m.z, m.mix
    cq_ref, sq_ref, ck_ref, sk_ref = m.tabs
    decay_ref, qdec_ref, kdec_ref, gl_ref = m.dec
    ka_ref, kb_ref, kc_ref, kd_ref, va_ref, vb_ref, vc_ref, vd_ref = m.kv_bufs
    rws = slice(r0, r0 + blk)
    trs = slice(trow0, trow0 + blk)
    win = slice(win0, win0 + lk)

    def block_diag(a, b):
        za = jnp.zeros(a.shape, a.dtype)
        return jnp.concatenate([jnp.concatenate([a, za], axis=1), jnp.concatenate([za, b], axis=1)], axis=0)

    def ret_pair(hp):
        cq, sq, ck, sk = cq_ref[trs, :], sq_ref[trs, :], ck_ref[trs, :], sk_ref[trs, :]
        heads = (2 * hp, 2 * hp + 1)
        qs, ks, vs, qds, kds = [], [], [], [], []
        for hd in heads:
            c = hd * RET_DK
            q = z_ref[rws, OFF_RQ + c:OFF_RQ + c + RET_DK]
            k = z_ref[rws, OFF_RK + c:OFF_RK + c + RET_DK]
            q = q * cq + pltpu.roll(q, RET_DK // 2, 1) * sq
            k = k * ck + pltpu.roll(k, RET_DK // 2, 1) * sk
            qs.append(q.astype(BF16))
            ks.append(k.astype(BF16))
            vs.append(z_ref[rws, OFF_RV + c:OFF_RV + c + RET_DV].astype(BF16))
            qds.append((q * qdec_ref[hd]).astype(BF16))
            kds.append(k * kdec_ref[hd])
        s2 = _dot_nt(jnp.concatenate(qs, axis=1), block_diag(*ks)) * decay_ref[hp]
        r_old = [r_get(hd) for hd in heads]
        v_bd = block_diag(*vs)
        r_bd = block_diag(*[r.astype(BF16) for r in r_old])
        o2 = _dot(jnp.concatenate([s2.astype(BF16)] + qds, axis=1), jnp.concatenate([v_bd, r_bd], axis=0))
        kd_t = jnp.concatenate(kds, axis=0).T.astype(BF16)
        r_inc = _dot(kd_t, v_bd)
        for i, hd in enumerate(heads):
            c = hd * RET_DK
            r_set(hd, gl_ref[hd] * r_old[i] + r_inc[:, i * RET_DV:(i + 1) * RET_DV])
            o = o2[:, i * RET_DV:(i + 1) * RET_DV]
            mu = jnp.mean(o, axis=-1, keepdims=True)
            oc = o - mu
            var = jnp.mean(oc * oc, axis=-1, keepdims=True)
            on = oc * lax.rsqrt(var + GN_EPS)
            gate = _silu(z_ref[rws, OFF_RG + c:OFF_RG + c + RET_DV])
            mix_ref[rws, c:c + RET_DV] = (on * gate).astype(BF16)

    def att_kv(kv):
        low = _low_lanes()
        valid = None if valid_fn is None else valid_fn()
        slabs = (2 * kv, 2 * kv + 1)
        if kv == 0:
            k_lo, k_hi, v_lo, v_hi = ka_ref, kb_ref, va_ref, vb_ref
        else:
            k_lo, k_hi, v_lo, v_hi = kc_ref, kd_ref, vc_ref, vd_ref
        qf = jnp.concatenate([z_ref[rws, OFF_AQ + sl * LANES:OFF_AQ + (sl + 1) * LANES] for sl in slabs], axis=0)
        qf = (qf * (ATT_HEAD_DIM ** -0.5 * LOG2E)).astype(BF16)
        keys = jnp.concatenate([k_lo[win, :], k_hi[win, :]], axis=0)
        s = _dot_nt(qf, keys)
        ps, ms = [], []
        for i, sl in enumerate(slabs):
            row_p, row_m = [], []
            for h in range(2):
                sh = s[i * blk:(i + 1) * blk, h * lk:(h + 1) * lk]
                if valid is not None:
                    sh = jnp.where(valid, sh, MASKED)
                sink = m.sink[layer, 2 * sl + h] * LOG2E
                mh = jnp.maximum(jnp.max(sh, axis=-1, keepdims=True), sink)
                row_p.append(jnp.exp2(sh - mh).astype(BF16))
                row_m.append(jnp.exp2(sink - mh))
            ps.append(jnp.concatenate(row_p, axis=1))
            ms.append(row_m)
        vals = jnp.concatenate([v_lo[win, :], v_hi[win, :]], axis=0)
        acc = _dot(jnp.concatenate(ps, axis=0), vals)
        for i, sl in enumerate(slabs):
            c = sl * LANES
            a = acc[i * blk:(i + 1) * blk, :]
            den = a[:, LANES:] + jnp.where(low, ms[i][0], ms[i][1])
            gate = _silu(z_ref[rws, OFF_AG + c:OFF_AG + c + LANES])
            mix_ref[rws, RET_WIDTH + c:RET_WIDTH + c + LANES] = (a[:, :LANES] / den * gate).astype(BF16)

    return ([functools.partial(ret_pair, hp) for hp in range(RET_HEADS // 2)]
            + [functools.partial(att_kv, kv) for kv in range(ATT_KV_HEADS)])


def _final_norm_rows(xo_ref, r0, n, fg_row):
    for a in range(r0, r0 + n, NORM_ROWS):
        xn = xo_ref[a:a + NORM_ROWS, :]
        ms = jnp.mean(xn * xn, axis=-1, keepdims=True)
        xo_ref[a:a + NORM_ROWS, :] = xn * lax.rsqrt(ms + NORM_EPS) * fg_row


class _PromptCfg(NamedTuple):
    rows: int
    nt: int
    last: bool
    layer: int


def _prompt_kernel(xn_ref, xc_ref, modn_ref, modc_ref, g_ref, win_ref, wout_ref,
                   cq_ref, sq_ref, ck_ref, sk_ref, decay_ref, qdec_ref, kdec_ref, gl_ref, sink_ref, fg_ref,
                   xo_ref, ro_ref, ko_ref, vo_ref,
                   za_ref, zb_ref, h_ref, mixa_ref, mixb_ref, r_scr, *kv_bufs,
                   cfg: _PromptCfg):
    rows, nt = cfg.rows, cfg.nt
    ntiles = pl.num_programs(0) - 2
    blk, lk = PROMPT_BLOCK, WINDOW + PROMPT_BLOCK
    g = pl.program_id(0)
    t = jnp.clip(g - 1, 0, ntiles - 1) % nt

    @pl.when(g == 0)
    def _():
        zb_ref[...] = jnp.zeros(zb_ref.shape, F32)
        mixa_ref[...] = jnp.zeros(mixa_ref.shape, BF16)

    @pl.when(t == 0)
    def _():
        for b in kv_bufs:
            b[0:WINDOW, :] = jnp.zeros((WINDOW, b.shape[1]), BF16)
        r_scr[...] = jnp.zeros(r_scr.shape, F32)

    def r_set(hd, val):
        r_scr[hd] = val

    def visible(j):
        col_i = lax.broadcasted_iota(jnp.int32, (blk, lk), 1)
        ok = None
        if blk == 2 * CHUNK:
            row_i = lax.broadcasted_iota(jnp.int32, (blk, lk), 0)
            first_key = jnp.where(row_i < CHUNK, 0, CHUNK)
            ok = (col_i >= first_key) & (col_i < first_key + WINDOW + CHUNK)
        if j * blk < WINDOW:
            started = col_i >= jnp.where(t == 0, WINDOW - j * blk, 0)
            ok = started if ok is None else ok & started
        return ok

    def step(zw_ref, zr_ref, mixw_ref, mixr_ref):
        norm = _norm_items(xn_ref, h_ref, 0, rows, g_ref, modn_ref, 0)
        proj = _in_proj_items(h_ref, win_ref, zw_ref)

        def out_item(c0):
            cs = slice(c0, c0 + PROJ_COLS)
            gate_row = modc_ref[0][:, 2 * D_MODEL + c0:2 * D_MODEL + c0 + PROJ_COLS]
            xo_ref[:, cs] = xc_ref[:, cs] + gate_row * _dot(mixr_ref[...], wout_ref[:, cs])
        outp = [functools.partial(out_item, c0) for c0 in range(0, D_MODEL, PROJ_COLS)]

        for r0 in range(0, rows, CHUNK):
            _fill_kv(kv_bufs, WINDOW + r0, CHUNK, zr_ref[r0:r0 + CHUNK, OFF_AK:OFF_AK + KV_WIDTH],
                     zr_ref[r0:r0 + CHUNK, OFF_AV:OFF_AV + KV_WIDTH])
        m = _MixRefs(zr_ref, mixw_ref, (cq_ref, sq_ref, ck_ref, sk_ref),
                     (decay_ref, qdec_ref, kdec_ref, gl_ref), sink_ref, kv_bufs)
        mix = []
        for j in range(rows // blk):
            mix += _mix_items(m, cfg.layer, j * blk, blk, j * blk, j * blk, lk,
                              lambda hd: r_scr[hd], r_set, functools.partial(visible, j))

        _emit(_interleave(norm, outp))
        _emit(_interleave(mix, proj))
        if cfg.last:
            _final_norm_rows(xo_ref, 0, rows, fg_ref[...])

        @pl.when(t != nt - 1)
        def _():
            for b in kv_bufs:
                b[0:WINDOW, :] = b[rows:rows + WINDOW, :]

        @pl.when((t == nt - 1) & (g <= ntiles))
        def _():
            ro_ref[...] = r_scr[...]
            ko_ref[...] = zr_ref[rows - WINDOW:rows, OFF_AK:OFF_AK + KV_WIDTH]
            vo_ref[...] = zr_ref[rows - WINDOW:rows, OFF_AV:OFF_AV + KV_WIDTH]

    @pl.when(g % 2 == 0)
    def _():
        step(za_ref, zb_ref, mixb_ref, mixa_ref)

    @pl.when(g % 2 == 1)
    def _():
        step(zb_ref, za_ref, mixa_ref, mixb_ref)


def _kv_scratch(krows):
    return ([pltpu.VMEM((krows, LANES), BF16) for _ in range(4)]
            + [pltpu.VMEM((krows, 2 * LANES), BF16) for _ in range(4)])


def _const_spec(shape):
    nd = len(shape)
    return pl.BlockSpec(shape, lambda g: (0,) * nd)


def _layer_spec(shape, layer):
    nd = len(shape)
    return pl.BlockSpec((None, *shape), lambda g: (layer,) + (0,) * nd)


def _smem_spec():
    return pl.BlockSpec(memory_space=pltpu.SMEM)


def _prompt_layer(layer, last, x2d, mod, g, w_in, w_out, tabs, dec, sink, final_g, batch, seq):
    rows = PROMPT_ROWS
    nt = seq // rows
    ntiles = batch * nt
    cfg = _PromptCfg(rows=rows, nt=nt, last=last, layer=layer)
    cq, sq, ck, sk = tabs
    decay, qdec, kdec, gl = dec

    nxt = lambda g: jnp.minimum(g, ntiles - 1)
    cur = lambda g: jnp.clip(g - 1, 0, ntiles - 1)
    fin = lambda g: jnp.maximum(g - 2, 0)
    tab_spec = pl.BlockSpec((rows, LANES), lambda g: (cur(g) % nt, 0))
    in_specs = [
        pl.BlockSpec((rows, D_MODEL), lambda g: (nxt(g), 0)),
        pl.BlockSpec((rows, D_MODEL), lambda g: (fin(g), 0)),
        pl.BlockSpec((None, 1, 1, 3 * D_MODEL), lambda g: (layer, nxt(g) // nt, 0, 0)),
        pl.BlockSpec((None, 1, 1, 3 * D_MODEL), lambda g: (layer, fin(g) // nt, 0, 0)),
        _layer_spec((1, D_MODEL), layer),
        _layer_spec((D_MODEL, IN_WIDTH), layer),
        _layer_spec((MIX_WIDTH, D_MODEL), layer),
        tab_spec, tab_spec, tab_spec, tab_spec,
        _const_spec(decay.shape), _const_spec(qdec.shape), _const_spec(kdec.shape),
        _smem_spec(), _smem_spec(),
        _const_spec((1, D_MODEL)),
    ]
    out_specs = [
        pl.BlockSpec((rows, D_MODEL), lambda g: (fin(g), 0)),
        pl.BlockSpec((None, RET_HEADS, RET_DK, RET_DV), lambda g: (cur(g) // nt, 0, 0, 0)),
        pl.BlockSpec((None, WINDOW, KV_WIDTH), lambda g: (cur(g) // nt, 0, 0)),
        pl.BlockSpec((None, WINDOW, KV_WIDTH), lambda g: (cur(g) // nt, 0, 0)),
    ]
    out_shape = [
        jax.ShapeDtypeStruct((batch * seq, D_MODEL), F32),
        jax.ShapeDtypeStruct((batch, RET_HEADS, RET_DK, RET_DV), F32),
        jax.ShapeDtypeStruct((batch, WINDOW, KV_WIDTH), F32),
        jax.ShapeDtypeStruct((batch, WINDOW, KV_WIDTH), F32),
    ]
    scratch = [
        pltpu.VMEM((rows, IN_WIDTH), F32),
        pltpu.VMEM((rows, IN_WIDTH), F32),
        pltpu.VMEM((rows, D_MODEL), BF16),
        pltpu.VMEM((rows, MIX_WIDTH), BF16),
        pltpu.VMEM((rows, MIX_WIDTH), BF16),
        pltpu.VMEM((RET_HEADS, RET_DK, RET_DV), F32),
        *_kv_scratch(WINDOW + rows),
    ]
    return pl.pallas_call(
        functools.partial(_prompt_kernel, cfg=cfg),
        grid=(ntiles + 2,),
        in_specs=in_specs, out_specs=out_specs, out_shape=out_shape,
        scratch_shapes=scratch,
        compiler_params=pltpu.CompilerParams(
            dimension_semantics=("arbitrary",),
            vmem_limit_bytes=VMEM_LIMIT_BYTES),
        name=f"prompt_layer{layer}",
    )(x2d, x2d, mod, mod, g, w_in, w_out, cq, sq, ck, sk, decay, qdec, kdec, gl, sink, final_g)


class _SampleCfg(NamedTuple):
    nseq: int
    seq: int
    last: bool
    layer: int


def _sample_kernel(x_ref, mod_ref, g_ref, win_ref, wout_ref,
                   cq_ref, sq_ref, ck_ref, sk_ref, decay_ref, qdec_ref, kdec_ref, gl_ref, sink_ref, fg_ref,
                   st_ref, cachek_ref, cachev_ref,
                   xo_ref, ro_ref, ko_ref, vo_ref,
                   z_ref, h_ref, mix_ref, *kv_bufs,
                   cfg: _SampleCfg):
    nseq, seq = cfg.nseq, cfg.seq
    lk = SAMPLE_KEYS
    pad = lk - WINDOW - seq

    for s in range(nseq):
        _emit(_norm_items(x_ref, h_ref, s * seq, seq, g_ref, mod_ref, s))
    _emit(_in_proj_items(h_ref, win_ref, z_ref))

    def visible():
        return lax.broadcasted_iota(jnp.int32, (seq, lk), 1) >= pad

    m = _MixRefs(z_ref, mix_ref, (cq_ref, sq_ref, ck_ref, sk_ref),
                 (decay_ref, qdec_ref, kdec_ref, gl_ref), sink_ref, kv_bufs)
    for s in range(nseq):
        rws = slice(s * seq, (s + 1) * seq)
        k_new = z_ref[rws, OFF_AK:OFF_AK + KV_WIDTH]
        v_new = z_ref[rws, OFF_AV:OFF_AV + KV_WIDTH]
        for b in kv_bufs:
            b[s * lk:s * lk + pad, :] = jnp.zeros((pad, b.shape[1]), BF16)
        _fill_kv(kv_bufs, s * lk + pad, WINDOW, cachek_ref[s], cachev_ref[s])
        _fill_kv(kv_bufs, s * lk + pad + WINDOW, seq, k_new, v_new)

        def r_set(hd, val, s=s):
            ro_ref[s, hd] = val

        _emit(_mix_items(m, cfg.layer, s * seq, seq, 0, s * lk, lk,
                         lambda hd, s=s: st_ref[s, hd], r_set, visible))
        ko_ref[s, 0:WINDOW - seq, :] = cachek_ref[s, seq:WINDOW, :]
        vo_ref[s, 0:WINDOW - seq, :] = cachev_ref[s, seq:WINDOW, :]
        ko_ref[s, WINDOW - seq:WINDOW, :] = k_new
        vo_ref[s, WINDOW - seq:WINDOW, :] = v_new

    xo_ref[...] = _dot(mix_ref[...], wout_ref[...])
    for s in range(nseq):
        rws = slice(s * seq, (s + 1) * seq)
        gate_row = mod_ref[s][:, 2 * D_MODEL:3 * D_MODEL]
        xo_ref[rws, :] = x_ref[rws, :] + gate_row * xo_ref[rws, :]
    if cfg.last:
        _final_norm_rows(xo_ref, 0, nseq * seq, fg_ref[...])


def _sample_layer(layer, last, x2d, mod, g, w_in, w_out, tabs, dec, sink, final_g,
                  state, cache_k, cache_v, batch, seq, mod_row0):
    nseq = SAMPLE_SEQS
    rows = nseq * seq
    cfg = _SampleCfg(nseq=nseq, seq=seq, last=last, layer=layer)
    cq, sq, ck, sk = tabs
    decay, qdec, kdec, gl = dec
    mod_blk0 = mod_row0 // nseq
    in_specs = [
        pl.BlockSpec((rows, D_MODEL), lambda i: (i, 0)),
        pl.BlockSpec((None, nseq, 1, 3 * D_MODEL), lambda i: (layer, mod_blk0 + i, 0, 0)),
        _layer_spec((1, D_MODEL), layer),
        _layer_spec((D_MODEL, IN_WIDTH), layer),
        _layer_spec((MIX_WIDTH, D_MODEL), layer),
        _const_spec(cq.shape), _const_spec(sq.shape), _const_spec(ck.shape), _const_spec(sk.shape),
        _const_spec(decay.shape), _const_spec(qdec.shape), _const_spec(kdec.shape),
        _smem_spec(), _smem_spec(),
        _const_spec((1, D_MODEL)),
        pl.BlockSpec((None, nseq, RET_HEADS, RET_DK, RET_DV), lambda i: (layer, i, 0, 0, 0)),
        pl.BlockSpec((None, nseq, WINDOW, KV_WIDTH), lambda i: (layer, i, 0, 0)),
        pl.BlockSpec((None, nseq, WINDOW, KV_WIDTH), lambda i: (layer, i, 0, 0)),
    ]
    out_specs = [
        pl.BlockSpec((rows, D_MODEL), lambda i: (i, 0)),
        pl.BlockSpec((nseq, RET_HEADS, RET_DK, RET_DV), lambda i: (i, 0, 0, 0)),
        pl.BlockSpec((nseq, WINDOW, KV_WIDTH), lambda i: (i, 0, 0)),
        pl.BlockSpec((nseq, WINDOW, KV_WIDTH), lambda i: (i, 0, 0)),
    ]
    out_shape = [
        jax.ShapeDtypeStruct((batch * seq, D_MODEL), F32),
        jax.ShapeDtypeStruct((batch, RET_HEADS, RET_DK, RET_DV), F32),
        jax.ShapeDtypeStruct((batch, WINDOW, KV_WIDTH), F32),
        jax.ShapeDtypeStruct((batch, WINDOW, KV_WIDTH), F32),
    ]
    scratch = [
        pltpu.VMEM((rows, IN_WIDTH), F32),
        pltpu.VMEM((rows, D_MODEL), BF16),
        pltpu.VMEM((rows, MIX_WIDTH), BF16),
        *_kv_scratch(nseq * SAMPLE_KEYS),
    ]
    return pl.pallas_call(
        functools.partial(_sample_kernel, cfg=cfg),
        grid=(batch // nseq,),
        in_specs=in_specs, out_specs=out_specs, out_shape=out_shape,
        scratch_shapes=scratch,
        compiler_params=pltpu.CompilerParams(
            dimension_semantics=("arbitrary",),
            vmem_limit_bytes=VMEM_LIMIT_BYTES),
        name=f"sample_layer{layer}",
    )(x2d, mod, g, w_in, w_out, cq, sq, ck, sk, decay, qdec, kdec, gl, sink, final_g,
      state, cache_k, cache_v)


def _rope_tables(pos):
    d = RET_DK
    inv = 1.0 / (ROPE_BASE ** (jnp.arange(0, d, 2, dtype=F32) / d))
    ang = pos.astype(F32)[:, None] * inv[None, :]
    cos, sin = jnp.cos(ang), jnp.sin(ang)
    cos2 = jnp.concatenate([cos, cos], axis=-1)
    sin2 = jnp.concatenate([-sin, sin], axis=-1)
    kscale = RET_DK ** -0.5
    return cos2, sin2, cos2 * kscale, sin2 * kscale


def _decay_tables(n):
    lg = jnp.log(1.0 - 2.0 ** (-5.0 - jnp.arange(RET_HEADS, dtype=F32)))
    idx = jnp.arange(n, dtype=F32)
    diff = idx[:, None] - idx[None, :]
    decay = jnp.where(diff[None] >= 0, jnp.exp(jnp.maximum(diff, 0.0)[None] * lg[:, None, None]), 0.0)
    qdec = jnp.exp((idx + 1.0)[None, :] * lg[:, None])
    kdec = jnp.exp((n - 1.0 - idx)[None, :] * lg[:, None])
    gl = jnp.exp(n * lg)
    bcast = lambda a: jnp.broadcast_to(a[:, :, None], (RET_HEADS, n, LANES))
    decay2 = jnp.concatenate([decay[0::2], decay[1::2]], axis=-1)
    return decay2, bcast(qdec), bcast(kdec), gl


def kernel(x_prompt, x_sample, c_prompt, c_sample, state_ret, cache_k, cache_v,
           norm_g, w_ada, b_ada, w_in, sink, w_out, final_g):
    batch, seq, _ = x_prompt.shape
    dbatch, dseq, _ = x_sample.shape

    c_all = jnp.concatenate([c_prompt, c_sample], axis=0)
    mod = _adaln(c_all, w_ada, b_ada).reshape(DEPTH, batch + dbatch, 1, 3 * D_MODEL)
    w_in_b = w_in.astype(BF16)
    w_out_b = w_out.astype(BF16)

    tabs_p = _rope_tables(jnp.arange(seq))
    tabs_s = _rope_tables(PAST_LEN + jnp.arange(dseq))
    dec_p = _decay_tables(PROMPT_BLOCK)
    dec_s = _decay_tables(dseq)
    ck4 = cache_k.reshape(DEPTH, dbatch, WINDOW, KV_WIDTH)
    cv4 = cache_v.reshape(DEPTH, dbatch, WINDOW, KV_WIDTH)
    fg = final_g.reshape(1, D_MODEL)
    g_all = norm_g.reshape(DEPTH, 1, D_MODEL)

    xp = x_prompt.reshape(batch * seq, D_MODEL)
    xs = x_sample.reshape(dbatch * dseq, D_MODEL)
    rp, kp, vp, rs, ks, vs = [], [], [], [], [], []
    for l in range(DEPTH):
        last = l == DEPTH - 1
        xp, r1, k1, v1 = _prompt_layer(l, last, xp, mod, g_all, w_in_b, w_out_b, tabs_p, dec_p,
                                       sink, fg, batch, seq)
        xs, r2, k2, v2 = _sample_layer(l, last, xs, mod, g_all, w_in_b, w_out_b, tabs_s, dec_s,
                                       sink, fg, state_ret, ck4, cv4, dbatch, dseq, mod_row0=batch)
        rp.append(r1); kp.append(k1); vp.append(v1)
        rs.append(r2); ks.append(k2); vs.append(v2)

    kv_shape_p = (DEPTH, batch, WINDOW, ATT_KV_HEADS, ATT_HEAD_DIM)
    kv_shape_s = (DEPTH, dbatch, WINDOW, ATT_KV_HEADS, ATT_HEAD_DIM)
    return (xp.reshape(batch, seq, D_MODEL), xs.reshape(dbatch, dseq, D_MODEL),
            jnp.stack(rp), jnp.stack(kp).reshape(kv_shape_p), jnp.stack(vp).reshape(kv_shape_p),
            jnp.stack(rs), jnp.stack(ks).reshape(kv_shape_s), jnp.stack(vs).reshape(kv_shape_s))
```

```python
import functools
from typing import NamedTuple

import jax
import jax.numpy as jnp
from jax import lax
from jax.experimental import pallas as pl
from jax.experimental.pallas import tpu as pltpu

D_MODEL = 1024
DEPTH = 4
CHUNK = 64
PAST_LEN = 4096
RET_HEADS = 4
RET_DK = 128
RET_DV = 128
RET_WIDTH = RET_HEADS * RET_DV
ATT_HEAD_DIM = 64
ATT_Q_HEADS = 8
ATT_KV_HEADS = 2
ATT_WIDTH = ATT_Q_HEADS * ATT_HEAD_DIM
KV_WIDTH = ATT_KV_HEADS * ATT_HEAD_DIM
WINDOW = 128
MIX_WIDTH = RET_WIDTH + ATT_WIDTH
ROPE_BASE = 10000.0
NORM_EPS = 1e-6
GN_EPS = 1e-5

OFF_RQ = 0
OFF_RK = OFF_RQ + RET_HEADS * RET_DK
OFF_RV = OFF_RK + RET_HEADS * RET_DK
OFF_RG = OFF_RV + RET_WIDTH
OFF_AQ = OFF_RG + RET_WIDTH
OFF_AK = OFF_AQ + ATT_WIDTH
OFF_AV = OFF_AK + KV_WIDTH
OFF_AG = OFF_AV + KV_WIDTH
IN_WIDTH = OFF_AG + ATT_WIDTH

LANES = 128
ATT_SLABS = ATT_WIDTH // LANES
MASKED = -1e30
LOG2E = 1.4426950408889634
VMEM_LIMIT_BYTES = 56 * 1024 * 1024
NORM_ROWS = 32
PROJ_COLS = 512

PROMPT_ROWS = 256
PROMPT_BLOCK = 128
PROMPT_STAGES = 3
SAMPLE_SEQS = 4
SAMPLE_KEYS = 2 * LANES

BF16 = jnp.bfloat16
F32 = jnp.float32


def _silu(g):
    return g * (1.0 / (1.0 + jnp.exp(-g)))


def _dot(a, b):
    return jnp.dot(a, b, preferred_element_type=F32)


def _dot_nt(a, b):
    return lax.dot_general(a, b, (((1,), (1,)), ((), ())), preferred_element_type=F32)


def _low_lanes():
    lane = lax.broadcasted_iota(jnp.int32, (1, LANES), 1)
    return lane < ATT_HEAD_DIM


def _adaln_kernel(c_ref, w_ref, b_ref, o_ref):
    s = _silu(c_ref[...]).astype(BF16)
    o_ref[0] = _dot(s, w_ref[0].astype(BF16)) + b_ref[0]


def _adaln(c_all, w_ada, b_ada):
    n = c_all.shape[0]
    tn = D_MODEL
    return pl.pallas_call(
        _adaln_kernel,
        grid=(DEPTH, 3 * D_MODEL // tn),
        in_specs=[
            pl.BlockSpec((n, D_MODEL), lambda l, j: (0, 0)),
            pl.BlockSpec((1, D_MODEL, tn), lambda l, j: (l, 0, j)),
            pl.BlockSpec((1, 1, tn), lambda l, j: (l, 0, j)),
        ],
        out_specs=pl.BlockSpec((1, n, tn), lambda l, j: (l, 0, j)),
        out_shape=jax.ShapeDtypeStruct((DEPTH, n, 3 * D_MODEL), F32),
        compiler_params=pltpu.CompilerParams(
            dimension_semantics=("arbitrary", "arbitrary"),
            vmem_limit_bytes=VMEM_LIMIT_BYTES),
        name="adaln",
    )(c_all, w_ada, b_ada.reshape(DEPTH, 1, 3 * D_MODEL))


def _emit(items):
    for it in items:
        it()


def _interleave(a, b):
    if not a:
        return list(b)
    out, nb = [], 0
    for i, it in enumerate(a):
        out.append(it)
        want = (i + 1) * len(b) // len(a)
        out.extend(b[nb:want])
        nb = want
    return out


def _norm_items(x_ref, h_ref, r0, n, g_ref, mod_ref, s):
    def item(a):
        mod_row = mod_ref[s]
        shift = mod_row[:, 0:D_MODEL]
        scale1 = 1.0 + mod_row[:, D_MODEL:2 * D_MODEL]
        xb = x_ref[a:a + NORM_ROWS, :]
        ms = jnp.mean(xb * xb, axis=-1, keepdims=True)
        y = xb * lax.rsqrt(ms + NORM_EPS) * g_ref[...]
        h_ref[a:a + NORM_ROWS, :] = (y * scale1 + shift).astype(BF16)
    return [functools.partial(item, a) for a in range(r0, r0 + n, NORM_ROWS)]


def _in_proj_items(h_ref, win_ref, z_ref):
    def item(c0):
        c1 = min(c0 + PROJ_COLS, IN_WIDTH)
        z_ref[:, c0:c1] = _dot(h_ref[...], win_ref[:, c0:c1])
    return [functools.partial(item, c0) for c0 in range(0, IN_WIDTH, PROJ_COLS)]


def _fill_kv(kv_bufs, dst0, n, k, v):
    ka_ref, kb_ref, kc_ref, kd_ref, va_ref, vb_ref, vc_ref, vd_ref = kv_bufs
    low = _low_lanes()
    kr = pltpu.roll(k, ATT_HEAD_DIM, 1)
    vr = pltpu.roll(v, ATT_HEAD_DIM, 1)
    d = slice(dst0, dst0 + n)
    ka_ref[d, :] = jnp.where(low, k, 0.0).astype(BF16)
    kb_ref[d, :] = jnp.where(low, 0.0, kr).astype(BF16)
    kc_ref[d, :] = jnp.where(low, kr, 0.0).astype(BF16)
    kd_ref[d, :] = jnp.where(low, 0.0, k).astype(BF16)
    one_lo = jnp.broadcast_to(jnp.where(low, 1.0, 0.0), (n, LANES))
    one_hi = 1.0 - one_lo
    va_ref[d, :] = jnp.concatenate([jnp.where(low, v, 0.0), one_lo], axis=1).astype(BF16)
    vb_ref[d, :] = jnp.concatenate([jnp.where(low, 0.0, vr), one_hi], axis=1).astype(BF16)
    vc_ref[d, :] = jnp.concatenate([jnp.where(low, vr, 0.0), one_lo], axis=1).astype(BF16)
    vd_ref[d, :] = jnp.concatenate([jnp.where(low, 0.0, v), one_hi], axis=1).astype(BF16)


class _MixRefs(NamedTuple):
    z: object
    mix: object
    tabs: tuple
    dec: tuple
    sink: object
    kv_bufs: tuple


def _mix_items(m: _MixRefs, layer, r0, blk, trow0, win0, lk, r_get, r_set, valid_fn):
    z_ref, mix_ref = m.z, m.mix
    cq_ref, sq_ref, ck_ref, sk_ref = m.tabs
    decay_ref, qdec_ref, kdec_ref, gl_ref = m.dec
    ka_ref, kb_ref, kc_ref, kd_ref, va_ref, vb_ref, vc_ref, vd_ref = m.kv_bufs
    rws = slice(r0, r0 + blk)
    trs = slice(trow0, trow0 + blk)
    win = slice(win0, win0 + lk)

    def block_diag(a, b):
        za = jnp.zeros(a.shape, a.dtype)
        return jnp.concatenate([jnp.concatenate([a, za], axis=1), jnp.concatenate([za, b], axis=1)], axis=0)

    def ret_pair(hp):
        cq, sq, ck, sk = cq_ref[trs, :], sq_ref[trs, :], ck_ref[trs, :], sk_ref[trs, :]
        heads = (2 * hp, 2 * hp + 1)
        qs, ks, vs, qds, kds = [], [], [], [], []
        for hd in heads:
            c = hd * RET_DK
            q = z_ref[rws, OFF_RQ + c:OFF_RQ + c + RET_DK]
            k = z_ref[rws, OFF_RK + c:OFF_RK + c + RET_DK]
            q = q * cq + pltpu.roll(q, RET_DK // 2, 1) * sq
            k = k * ck + pltpu.roll(k, RET_DK // 2, 1) * sk
            qs.append(q.astype(BF16))
            ks.append(k.astype(BF16))
            vs.append(z_ref[rws, OFF_RV + c:OFF_RV + c + RET_DV].astype(BF16))
            qds.append((q * qdec_ref[hd]).astype(BF16))
            kds.append(k * kdec_ref[hd])
        s2 = _dot_nt(jnp.concatenate(qs, axis=1), block_diag(*ks)) * decay_ref[hp]
        r_old = [r_get(hd) for hd in heads]
        v_bd = block_diag(*vs)
        r_bd = block_diag(*[r.astype(BF16) for r in r_old])
        o2 = _dot(jnp.concatenate([s2.astype(BF16)] + qds, axis=1), jnp.concatenate([v_bd, r_bd], axis=0))
        kd_t = jnp.concatenate(kds, axis=0).T.astype(BF16)
        r_inc = _dot(kd_t, v_bd)
        for i, hd in enumerate(heads):
            c = hd * RET_DK
            r_set(hd, gl_ref[hd] * r_old[i] + r_inc[:, i * RET_DV:(i + 1) * RET_DV])
            o = o2[:, i * RET_DV:(i + 1) * RET_DV]
            mu = jnp.mean(o, axis=-1, keepdims=True)
            oc = o - mu
            var = jnp.mean(oc * oc, axis=-1, keepdims=True)
            on = oc * lax.rsqrt(var + GN_EPS)
            gate = _silu(z_ref[rws, OFF_RG + c:OFF_RG + c + RET_DV])
            mix_ref[rws, c:c + RET_DV] = (on * gate).astype(BF16)

    def att_kv(kv):
        low = _low_lanes()
        valid = None if valid_fn is None else valid_fn()
        slabs = (2 * kv, 2 * kv + 1)
        if kv == 0:
            k_lo, k_hi, v_lo, v_hi = ka_ref, kb_ref, va_ref, vb_ref
        else:
            k_lo, k_hi, v_lo, v_hi = kc_ref, kd_ref, vc_ref, vd_ref
        qf = jnp.concatenate([z_ref[rws, OFF_AQ + sl * LANES:OFF_AQ + (sl + 1) * LANES] for sl in slabs], axis=0)
        qf = (qf * (ATT_HEAD_DIM ** -0.5 * LOG2E)).astype(BF16)
        keys = jnp.concatenate([k_lo[win, :], k_hi[win, :]], axis=0)
        s = _dot_nt(qf, keys)
        ps, ms = [], []
        for i, sl in enumerate(slabs):
            row_p, row_m = [], []
            for h in range(2):
                sh = s[i * blk:(i + 1) * blk, h * lk:(h + 1) * lk]
                if valid is not None:
                    sh = jnp.where(valid, sh, MASKED)
                sink = m.sink[layer, 2 * sl + h] * LOG2E
                mh = jnp.maximum(jnp.max(sh, axis=-1, keepdims=True), sink)
                row_p.append(jnp.exp2(sh - mh).astype(BF16))
                row_m.append(jnp.exp2(sink - mh))
            ps.append(jnp.concatenate(row_p, axis=1))
            ms.append(row_m)
        vals = jnp.concatenate([v_lo[win, :], v_hi[win, :]], axis=0)
        acc = _dot(jnp.concatenate(ps, axis=0), vals)
        for i, sl in enumerate(slabs):
            c = sl * LANES
            a = acc[i * blk:(i + 1) * blk, :]
            den = a[:, LANES:] + jnp.where(low, ms[i][0], ms[i][1])
            gate = _silu(z_ref[rws, OFF_AG + c:OFF_AG + c + LANES])
            mix_ref[rws, RET_WIDTH + c:RET_WIDTH + c + LANES] = (a[:, :LANES] / den * gate).astype(BF16)

    return ([functools.partial(ret_pair, hp) for hp in range(RET_HEADS // 2)]
            + [functools.partial(att_kv, kv) for kv in range(ATT_KV_HEADS)])


def _final_norm_rows(xo_ref, r0, n, fg_row):
    for a in range(r0, r0 + n, NORM_ROWS):
        xn = xo_ref[a:a + NORM_ROWS, :]
        ms = jnp.mean(xn * xn, axis=-1, keepdims=True)
        xo_ref[a:a + NORM_ROWS, :] = xn * lax.rsqrt(ms + NORM_EPS) * fg_row


class _PromptCfg(NamedTuple):
    rows: int
    nt: int
    ntiles: int
    last: bool
    layer: int


def _prompt_kernel(xn_ref, xc_ref, modn_ref, modc_ref, g_ref, win_ref, wout_ref,
                   cq_ref, sq_ref, ck_ref, sk_ref, decay_ref, qdec_ref, kdec_ref, gl_ref, sink_ref, fg_ref,
                   xo_ref, ro_ref, ko_ref, vo_ref,
                   za_ref, zb_ref, h_ref, mixa_ref, mixb_ref, r_scr, *kv_bufs,
                   cfg: _PromptCfg):
    rows, nt, ntiles = cfg.rows, cfg.nt, cfg.ntiles
    blk, lk = PROMPT_BLOCK, WINDOW + PROMPT_BLOCK
    g = pl.program_id(0)
    t = jnp.clip(g - 1, 0, ntiles - 1) % nt

    @pl.when(t == 0)
    def _():
        for b in kv_bufs:
            b[0:WINDOW, :] = jnp.zeros((WINDOW, b.shape[1]), BF16)
        r_scr[...] = jnp.zeros(r_scr.shape, F32)

    def r_set(hd, val):
        r_scr[hd] = val

    def visible(j):
        col_i = lax.broadcasted_iota(jnp.int32, (blk, lk), 1)
        ok = None
        if blk == 2 * CHUNK:
            row_i = lax.broadcasted_iota(jnp.int32, (blk, lk), 0)
            first_key = jnp.where(row_i < CHUNK, 0, CHUNK)
            ok = (col_i >= first_key) & (col_i < first_key + WINDOW + CHUNK)
        if j * blk < WINDOW:
            started = col_i >= jnp.where(t == 0, WINDOW - j * blk, 0)
            ok = started if ok is None else ok & started
        return ok

    def step(parity, stages):
        zw_ref, zr_ref, mixw_ref, mixr_ref = ((za_ref, zb_ref, mixb_ref, mixa_ref) if parity == 0 else
                                              (zb_ref, za_ref, mixa_ref, mixb_ref))
        norm, proj, outp, mix = [], [], [], []
        if 1 in stages:
            norm = _norm_items(xn_ref, h_ref, 0, rows, g_ref, modn_ref, 0)
            proj = _in_proj_items(h_ref, win_ref, zw_ref)

        if 3 in stages:
            def out_item(c0):
                cs = slice(c0, c0 + PROJ_COLS)
                gate_row = modc_ref[0][:, 2 * D_MODEL + c0:2 * D_MODEL + c0 + PROJ_COLS]
                xo_ref[:, cs] = xc_ref[:, cs] + gate_row * _dot(mixr_ref[...], wout_ref[:, cs])
            outp = [functools.partial(out_item, c0) for c0 in range(0, D_MODEL, PROJ_COLS)]

        if 2 in stages:
            for r0 in range(0, rows, CHUNK):
                _fill_kv(kv_bufs, WINDOW + r0, CHUNK, zr_ref[r0:r0 + CHUNK, OFF_AK:OFF_AK + KV_WIDTH],
                         zr_ref[r0:r0 + CHUNK, OFF_AV:OFF_AV + KV_WIDTH])
            m = _MixRefs(zr_ref, mixw_ref, (cq_ref, sq_ref, ck_ref, sk_ref),
                         (decay_ref, qdec_ref, kdec_ref, gl_ref), sink_ref, kv_bufs)
            for j in range(rows // blk):
                mix += _mix_items(m, cfg.layer, j * blk, blk, j * blk, j * blk, lk,
                                  lambda hd: r_scr[hd], r_set, functools.partial(visible, j))

        _emit(_interleave(norm, outp))
        _emit(_interleave(mix, proj))
        if 3 in stages and cfg.last:
            _final_norm_rows(xo_ref, 0, rows, fg_ref[...])

        if 2 in stages:
            @pl.when(t != nt - 1)
            def _():
                for b in kv_bufs:
                    b[0:WINDOW, :] = b[rows:rows + WINDOW, :]

            @pl.when(t == nt - 1)
            def _():
                ro_ref[...] = r_scr[...]
                ko_ref[...] = zr_ref[rows - WINDOW:rows, OFF_AK:OFF_AK + KV_WIDTH]
                vo_ref[...] = zr_ref[rows - WINDOW:rows, OFF_AV:OFF_AV + KV_WIDTH]

    @pl.when(g == 0)
    def _():
        mixb_ref[...] = jnp.zeros(mixb_ref.shape, BF16)
        step(0, (1,))

    for parity in (0, 1):
        @pl.when((g % 2 == parity) & (g >= 1) & (g <= ntiles - 1))
        def _():
            step(parity, (1, 2, 3))

    @pl.when(g == ntiles)
    def _():
        step(ntiles % 2, (2, 3))

    @pl.when(g == ntiles + 1)
    def _():
        step((ntiles + 1) % 2, (3,))


def _kv_scratch(krows):
    return ([pltpu.VMEM((krows, LANES), BF16) for _ in range(4)]
            + [pltpu.VMEM((krows, 2 * LANES), BF16) for _ in range(4)])


def _const_spec(shape):
    nd = len(shape)
    return pl.BlockSpec(shape, lambda g: (0,) * nd)


def _layer_spec(shape, layer):
    nd = len(shape)
    return pl.BlockSpec((None, *shape), lambda g: (layer,) + (0,) * nd)


def _smem_spec():
    return pl.BlockSpec(memory_space=pltpu.SMEM)


def _prompt_layer(layer, last, x2d, mod, g, w_in, w_out, tabs, dec, sink, final_g, batch, seq):
    rows = PROMPT_ROWS
    nt = seq // rows
    ntiles = batch * nt
    cfg = _PromptCfg(rows=rows, nt=nt, ntiles=ntiles, last=last, layer=layer)
    cq, sq, ck, sk = tabs
    decay, qdec, kdec, gl = dec

    nxt = lambda g: jnp.minimum(g, ntiles - 1)
    cur = lambda g: jnp.clip(g - 1, 0, ntiles - 1)
    fin = lambda g: jnp.maximum(g - 2, 0)
    tab_spec = pl.BlockSpec((rows, LANES), lambda g: (cur(g) % nt, 0))
    in_specs = [
        pl.BlockSpec((rows, D_MODEL), lambda g: (nxt(g), 0)),
        pl.BlockSpec((rows, D_MODEL), lambda g: (fin(g), 0)),
        pl.BlockSpec((None, 1, 1, 3 * D_MODEL), lambda g: (layer, nxt(g) // nt, 0, 0)),
        pl.BlockSpec((None, 1, 1, 3 * D_MODEL), lambda g: (layer, fin(g) // nt, 0, 0)),
        _layer_spec((1, D_MODEL), layer),
        _layer_spec((D_MODEL, IN_WIDTH), layer),
        _layer_spec((MIX_WIDTH, D_MODEL), layer),
        tab_spec, tab_spec, tab_spec, tab_spec,
        _const_spec(decay.shape), _const_spec(qdec.shape), _const_spec(kdec.shape),
        _smem_spec(), _smem_spec(),
        _const_spec((1, D_MODEL)),
    ]
    out_specs = [
        pl.BlockSpec((rows, D_MODEL), lambda g: (fin(g), 0)),
        pl.BlockSpec((None, RET_HEADS, RET_DK, RET_DV), lambda g: (cur(g) // nt, 0, 0, 0)),
        pl.BlockSpec((None, WINDOW, KV_WIDTH), lambda g: (cur(g) // nt, 0, 0)),
        pl.BlockSpec((None, WINDOW, KV_WIDTH), lambda g: (cur(g) // nt, 0, 0)),
    ]
    out_shape = [
        jax.ShapeDtypeStruct((batch * seq, D_MODEL), F32),
        jax.ShapeDtypeStruct((batch, RET_HEADS, RET_DK, RET_DV), F32),
        jax.ShapeDtypeStruct((batch, WINDOW, KV_WIDTH), F32),
        jax.ShapeDtypeStruct((batch, WINDOW, KV_WIDTH), F32),
    ]
    scratch = [
        pltpu.VMEM((rows, IN_WIDTH), F32),
        pltpu.VMEM((rows, IN_WIDTH), F32),
        pltpu.VMEM((rows, D_MODEL), BF16),
        pltpu.VMEM((rows, MIX_WIDTH), BF16),
        pltpu.VMEM((rows, MIX_WIDTH), BF16),
        pltpu.VMEM((RET_HEADS, RET_DK, RET_DV), F32),
        *_kv_scratch(WINDOW + rows),
    ]
    return pl.pallas_call(
        functools.partial(_prompt_kernel, cfg=cfg),
        grid=(ntiles + PROMPT_STAGES - 1,),
        in_specs=in_specs, out_specs=out_specs, out_shape=out_shape,
        scratch_shapes=scratch,
        compiler_params=pltpu.CompilerParams(
            dimension_semantics=("arbitrary",),
            vmem_limit_bytes=VMEM_LIMIT_BYTES),
        name=f"prompt_layer{layer}",
    )(x2d, x2d, mod, mod, g, w_in, w_out, cq, sq, ck, sk, decay, qdec, kdec, gl, sink, final_g)


class _SampleCfg(NamedTuple):
    nseq: int
    seq: int
    last: bool
    layer: int


def _sample_kernel(x_ref, mod_ref, g_ref, win_ref, wout_ref,
                   cq_ref, sq_ref, ck_ref, sk_ref, decay_ref, qdec_ref, kdec_ref, gl_ref, sink_ref, fg_ref,
                   st_ref, cachek_ref, cachev_ref,
                   xo_ref, ro_ref, ko_ref, vo_ref,
                   z_ref, h_ref, mix_ref, *kv_bufs,
                   cfg: _SampleCfg):
    nseq, seq = cfg.nseq, cfg.seq
    lk = SAMPLE_KEYS
    pad = lk - WINDOW - seq

    for s in range(nseq):
        _emit(_norm_items(x_ref, h_ref, s * seq, seq, g_ref, mod_ref, s))
    _emit(_in_proj_items(h_ref, win_ref, z_ref))

    def visible():
        return lax.broadcasted_iota(jnp.int32, (seq, lk), 1) >= pad

    m = _MixRefs(z_ref, mix_ref, (cq_ref, sq_ref, ck_ref, sk_ref),
                 (decay_ref, qdec_ref, kdec_ref, gl_ref), sink_ref, kv_bufs)
    for s in range(nseq):
        rws = slice(s * seq, (s + 1) * seq)
        k_new = z_ref[rws, OFF_AK:OFF_AK + KV_WIDTH]
        v_new = z_ref[rws, OFF_AV:OFF_AV + KV_WIDTH]
        for b in kv_bufs:
            b[s * lk:s * lk + pad, :] = jnp.zeros((pad, b.shape[1]), BF16)
        _fill_kv(kv_bufs, s * lk + pad, WINDOW, cachek_ref[s], cachev_ref[s])
        _fill_kv(kv_bufs, s * lk + pad + WINDOW, seq, k_new, v_new)

        def r_set(hd, val, s=s):
            ro_ref[s, hd] = val

        _emit(_mix_items(m, cfg.layer, s * seq, seq, 0, s * lk, lk,
                         lambda hd, s=s: st_ref[s, hd], r_set, visible))
        ko_ref[s, 0:WINDOW - seq, :] = cachek_ref[s, seq:WINDOW, :]
        vo_ref[s, 0:WINDOW - seq, :] = cachev_ref[s, seq:WINDOW, :]
        ko_ref[s, WINDOW - seq:WINDOW, :] = k_new
        vo_ref[s, WINDOW - seq:WINDOW, :] = v_new

    xo_ref[...] = _dot(mix_ref[...], wout_ref[...])
    for s in range(nseq):
        rws = slice(s * seq, (s + 1) * seq)
        gate_row = mod_ref[s][:, 2 * D_MODEL:3 * D_MODEL]
        xo_ref[rws, :] = x_ref[rws, :] + gate_row * xo_ref[rws, :]
    if cfg.last:
        _final_norm_rows(xo_ref, 0, nseq * seq, fg_ref[...])


def _sample_layer(layer, last, x2d, mod, g, w_in, w_out, tabs, dec, sink, final_g,
                  state, cache_k, cache_v, batch, seq, mod_row0):
    nseq = SAMPLE_SEQS
    rows = nseq * seq
    cfg = _SampleCfg(nseq=nseq, seq=seq, last=last, layer=layer)
    cq, sq, ck, sk = tabs
    decay, qdec, kdec, gl = dec
    mod_blk0 = mod_row0 // nseq
    in_specs = [
        pl.BlockSpec((rows, D_MODEL), lambda i: (i, 0)),
        pl.BlockSpec((None, nseq, 1, 3 * D_MODEL), lambda i: (layer, mod_blk0 + i, 0, 0)),
        _layer_spec((1, D_MODEL), layer),
        _layer_spec((D_MODEL, IN_WIDTH), layer),
        _layer_spec((MIX_WIDTH, D_MODEL), layer),
        _const_spec(cq.shape), _const_spec(sq.shape), _const_spec(ck.shape), _const_spec(sk.shape),
        _const_spec(decay.shape), _const_spec(qdec.shape), _const_spec(kdec.shape),
        _smem_spec(), _smem_spec(),
        _const_spec((1, D_MODEL)),
        pl.BlockSpec((None, nseq, RET_HEADS, RET_DK, RET_DV), lambda i: (layer, i, 0, 0, 0)),
        pl.BlockSpec((None, nseq, WINDOW, KV_WIDTH), lambda i: (layer, i, 0, 0)),
        pl.BlockSpec((None, nseq, WINDOW, KV_WIDTH), lambda i: (layer, i, 0, 0)),
    ]
    out_specs = [
        pl.BlockSpec((rows, D_MODEL), lambda i: (i, 0)),
        pl.BlockSpec((nseq, RET_HEADS, RET_DK, RET_DV), lambda i: (i, 0, 0, 0)),
        pl.BlockSpec((nseq, WINDOW, KV_WIDTH), lambda i: (i, 0, 0)),
        pl.BlockSpec((nseq, WINDOW, KV_WIDTH), lambda i: (i, 0, 0)),
    ]
    out_shape = [
        jax.ShapeDtypeStruct((batch * seq, D_MODEL), F32),
        jax.ShapeDtypeStruct((batch, RET_HEADS, RET_DK, RET_DV), F32),
        jax.ShapeDtypeStruct((batch, WINDOW, KV_WIDTH), F32),
        jax.ShapeDtypeStruct((batch, WINDOW, KV_WIDTH), F32),
    ]
    scratch = [
        pltpu.VMEM((rows, IN_WIDTH), F32),
        pltpu.VMEM((rows, D_MODEL), BF16),
        pltpu.VMEM((rows, MIX_WIDTH), BF16),
        *_kv_scratch(nseq * SAMPLE_KEYS),
    ]
    return pl.pallas_call(
        functools.partial(_sample_kernel, cfg=cfg),
        grid=(batch // nseq,),
        in_specs=in_specs, out_specs=out_specs, out_shape=out_shape,
        scratch_shapes=scratch,
        compiler_params=pltpu.CompilerParams(
            dimension_semantics=("arbitrary",),
            vmem_limit_bytes=VMEM_LIMIT_BYTES),
        name=f"sample_layer{layer}",
    )(x2d, mod, g, w_in, w_out, cq, sq, ck, sk, decay, qdec, kdec, gl, sink, final_g,
      state, cache_k, cache_v)


def _rope_tables(start, n):
    d = RET_DK
    inv = 1.0 / (ROPE_BASE ** (jnp.arange(0, d, 2, dtype=F32) / d))
    ang_a = jnp.arange(start, start + n, CHUNK).astype(F32)[:, None] * inv[None, :]
    ang_b = jnp.arange(CHUNK).astype(F32)[:, None] * inv[None, :]
    ca, sa, cb, sb = lax.optimization_barrier((jnp.cos(ang_a), jnp.sin(ang_a), jnp.cos(ang_b), jnp.sin(ang_b)))
    cos = (ca[:, None, :] * cb[None] - sa[:, None, :] * sb[None]).reshape(n, d // 2)
    sin = (sa[:, None, :] * cb[None] + ca[:, None, :] * sb[None]).reshape(n, d // 2)
    cos2 = jnp.concatenate([cos, cos], axis=-1)
    sin2 = jnp.concatenate([-sin, sin], axis=-1)
    kscale = RET_DK ** -0.5
    return cos2, sin2, cos2 * kscale, sin2 * kscale


def _decay_tables(n):
    lg = jnp.log(1.0 - 2.0 ** (-5.0 - jnp.arange(RET_HEADS, dtype=F32)))
    idx = jnp.arange(n, dtype=F32)
    diff = idx[:, None] - idx[None, :]
    decay = jnp.where(diff[None] >= 0, jnp.exp(jnp.maximum(diff, 0.0)[None] * lg[:, None, None]), 0.0)
    qdec = jnp.exp((idx + 1.0)[None, :] * lg[:, None])
    kdec = jnp.exp((n - 1.0 - idx)[None, :] * lg[:, None])
    gl = jnp.exp(n * lg)
    bcast = lambda a: jnp.broadcast_to(a[:, :, None], (RET_HEADS, n, LANES))
    decay2 = jnp.concatenate([decay[0::2], decay[1::2]], axis=-1)
    return decay2, bcast(qdec), bcast(kdec), gl


def kernel(x_prompt, x_sample, c_prompt, c_sample, state_ret, cache_k, cache_v,
           norm_g, w_ada, b_ada, w_in, sink, w_out, final_g):
    batch, seq, _ = x_prompt.shape
    dbatch, dseq, _ = x_sample.shape

    c_all = jnp.concatenate([c_prompt, c_sample], axis=0)
    mod = _adaln(c_all, w_ada, b_ada).reshape(DEPTH, batch + dbatch, 1, 3 * D_MODEL)
    w_in_b = w_in.astype(BF16)
    w_out_b = w_out.astype(BF16)

    tabs_p = _rope_tables(0, seq)
    tabs_s = _rope_tables(PAST_LEN, dseq)
    dec_p = _decay_tables(PROMPT_BLOCK)
    dec_s = _decay_tables(dseq)
    ck4 = cache_k.reshape(DEPTH, dbatch, WINDOW, KV_WIDTH)
    cv4 = cache_v.reshape(DEPTH, dbatch, WINDOW, KV_WIDTH)
    fg = final_g.reshape(1, D_MODEL)
    g_all = norm_g.reshape(DEPTH, 1, D_MODEL)

    xp = x_prompt.reshape(batch * seq, D_MODEL)
    xs = x_sample.reshape(dbatch * dseq, D_MODEL)
    rp, kp, vp, rs, ks, vs = [], [], [], [], [], []
    for l in range(DEPTH):
        last = l == DEPTH - 1
        xp, r1, k1, v1 = _prompt_layer(l, last, xp, mod, g_all, w_in_b, w_out_b, tabs_p, dec_p,
                                       sink, fg, batch, seq)
        xs, r2, k2, v2 = _sample_layer(l, last, xs, mod, g_all, w_in_b, w_out_b, tabs_s, dec_s,
                                       sink, fg, state_ret, ck4, cv4, dbatch, dseq, mod_row0=batch)
        rp.append(r1); kp.append(k1); vp.append(v1)
        rs.append(r2); ks.append(k2); vs.append(v2)

    kv_shape_p = (DEPTH, batch, WINDOW, ATT_KV_HEADS, ATT_HEAD_DIM)
    kv_shape_s = (DEPTH, dbatch, WINDOW, ATT_KV_HEADS, ATT_HEAD_DIM)
    return (xp.reshape(batch, seq, D_MODEL), xs.reshape(dbatch, dseq, D_MODEL),
            jnp.stack(rp), jnp.stack(kp).reshape(kv_shape_p), jnp.stack(vp).reshape(kv_shape_p),
            jnp.stack(rs), jnp.stack(ks).reshape(kv_shape_s), jnp.stack(vs).reshape(kv_shape_s))
```

```python
import functools
from typing import NamedTuple

import jax
import jax.numpy as jnp
from jax import lax
from jax.experimental import pallas as pl
from jax.experimental.pallas import tpu as pltpu

D_MODEL = 1024
DEPTH = 4
CHUNK = 64
PAST_LEN = 4096
RET_HEADS = 4
RET_DK = 128
RET_DV = 128
RET_WIDTH = RET_HEADS * RET_DV
ATT_HEAD_DIM = 64
ATT_Q_HEADS = 8
ATT_KV_HEADS = 2
ATT_WIDTH = ATT_Q_HEADS * ATT_HEAD_DIM
KV_WIDTH = ATT_KV_HEADS * ATT_HEAD_DIM
WINDOW = 128
MIX_WIDTH = RET_WIDTH + ATT_WIDTH
ROPE_BASE = 10000.0
NORM_EPS = 1e-6
GN_EPS = 1e-5

OFF_RQ = 0
OFF_RK = OFF_RQ + RET_HEADS * RET_DK
OFF_RV = OFF_RK + RET_HEADS * RET_DK
OFF_RG = OFF_RV + RET_WIDTH
OFF_AQ = OFF_RG + RET_WIDTH
OFF_AK = OFF_AQ + ATT_WIDTH
OFF_AV = OFF_AK + KV_WIDTH
OFF_AG = OFF_AV + KV_WIDTH
IN_WIDTH = OFF_AG + ATT_WIDTH

LANES = 128
ATT_SLABS = ATT_WIDTH // LANES
MASKED = -1e30
LOG2E = 1.4426950408889634
VMEM_LIMIT_BYTES = 56 * 1024 * 1024
NORM_ROWS = 32
PROJ_COLS = 512

PROMPT_ROWS = 256
PROMPT_BLOCK = 128
PROMPT_STAGES = 4
SAMPLE_SEQS = 4
SAMPLE_KEYS = 2 * LANES

BF16 = jnp.bfloat16
F32 = jnp.float32


def _silu(g):
    return g * (1.0 / (1.0 + jnp.exp(-g)))


def _dot(a, b):
    return jnp.dot(a, b, preferred_element_type=F32)


def _dot_nt(a, b):
    return lax.dot_general(a, b, (((1,), (1,)), ((), ())), preferred_element_type=F32)


def _low_lanes():
    lane = lax.broadcasted_iota(jnp.int32, (1, LANES), 1)
    return lane < ATT_HEAD_DIM


def _adaln_kernel(c_ref, w_ref, b_ref, o_ref):
    s = _silu(c_ref[...]).astype(BF16)
    o_ref[0] = _dot(s, w_ref[0].astype(BF16)) + b_ref[0]


def _adaln(c_all, w_ada, b_ada):
    n = c_all.shape[0]
    tn = D_MODEL
    return pl.pallas_call(
        _adaln_kernel,
        grid=(DEPTH, 3 * D_MODEL // tn),
        in_specs=[
            pl.BlockSpec((n, D_MODEL), lambda l, j: (0, 0)),
            pl.BlockSpec((1, D_MODEL, tn), lambda l, j: (l, 0, j)),
            pl.BlockSpec((1, 1, tn), lambda l, j: (l, 0, j)),
        ],
        out_specs=pl.BlockSpec((1, n, tn), lambda l, j: (l, 0, j)),
        out_shape=jax.ShapeDtypeStruct((DEPTH, n, 3 * D_MODEL), F32),
        compiler_params=pltpu.CompilerParams(
            dimension_semantics=("arbitrary", "arbitrary"),
            vmem_limit_bytes=VMEM_LIMIT_BYTES),
        name="adaln",
    )(c_all, w_ada, b_ada.reshape(DEPTH, 1, 3 * D_MODEL))


def _emit(items):
    for it in items:
        it()


def _interleave(a, b):
    if not a:
        return list(b)
    out, nb = [], 0
    for i, it in enumerate(a):
        out.append(it)
        want = (i + 1) * len(b) // len(a)
        out.extend(b[nb:want])
        nb = want
    return out


def _norm_items(x_ref, h_ref, r0, n, g_ref, mod_ref, s):
    def item(a):
        mod_row = mod_ref[s]
        shift = mod_row[:, 0:D_MODEL]
        scale1 = 1.0 + mod_row[:, D_MODEL:2 * D_MODEL]
        xb = x_ref[a:a + NORM_ROWS, :]
        ms = jnp.mean(xb * xb, axis=-1, keepdims=True)
        y = xb * lax.rsqrt(ms + NORM_EPS) * g_ref[...]
        h_ref[a:a + NORM_ROWS, :] = (y * scale1 + shift).astype(BF16)
    return [functools.partial(item, a) for a in range(r0, r0 + n, NORM_ROWS)]


def _in_proj_items(h_ref, win_ref, z_ref):
    def item(c0):
        c1 = min(c0 + PROJ_COLS, IN_WIDTH)
        z_ref[:, c0:c1] = _dot(h_ref[...], win_ref[:, c0:c1])
    return [functools.partial(item, c0) for c0 in range(0, IN_WIDTH, PROJ_COLS)]


def _fill_kv(kv_bufs, dst0, n, k, v):
    ka_ref, kb_ref, kc_ref, kd_ref, va_ref, vb_ref, vc_ref, vd_ref = kv_bufs
    low = _low_lanes()
    kr = pltpu.roll(k, ATT_HEAD_DIM, 1)
    vr = pltpu.roll(v, ATT_HEAD_DIM, 1)
    d = slice(dst0, dst0 + n)
    ka_ref[d, :] = jnp.where(low, k, 0.0).astype(BF16)
    kb_ref[d, :] = jnp.where(low, 0.0, kr).astype(BF16)
    kc_ref[d, :] = jnp.where(low, kr, 0.0).astype(BF16)
    kd_ref[d, :] = jnp.where(low, 0.0, k).astype(BF16)
    one_lo = jnp.broadcast_to(jnp.where(low, 1.0, 0.0), (n, LANES))
    one_hi = 1.0 - one_lo
    va_ref[d, :] = jnp.concatenate([jnp.where(low, v, 0.0), one_lo], axis=1).astype(BF16)
    vb_ref[d, :] = jnp.concatenate([jnp.where(low, 0.0, vr), one_hi], axis=1).astype(BF16)
    vc_ref[d, :] = jnp.concatenate([jnp.where(low, vr, 0.0), one_lo], axis=1).astype(BF16)
    vd_ref[d, :] = jnp.concatenate([jnp.where(low, 0.0, v), one_hi], axis=1).astype(BF16)


class _MixRefs(NamedTuple):
    z: object
    mix: object
    tabs: tuple
    dec: tuple
    sink: object
    kv_bufs: tuple


class _ValueStash:
    def __init__(self):
        self.d = {}

    def put(self, key, *vals):
        self.d[key] = vals

    def get(self, key):
        return self.d[key]


class _RefStash:
    def __init__(self, refs, j, blk):
        self.refs, self.j, self.blk = refs, j, blk
        self.rws = slice(j * blk, (j + 1) * blk)

    def _slots(self, key):
        lhs_ref, kdt_ref, v_ref, gate_ref, p_ref, e_ref = self.refs
        kind, i = key
        rws, j, blk = self.rws, self.j, self.blk
        if kind == "ret":
            w = lhs_ref.shape[1] // 2
            r0 = (2 * j + i) * RET_DK
            return [(lhs_ref, rws, slice(i * w, (i + 1) * w)), (kdt_ref, slice(r0, r0 + RET_DK), slice(None)),
                    (v_ref, rws, slice(2 * i * RET_DV, 2 * (i + 1) * RET_DV))]
        if kind == "gate":
            return [(gate_ref, rws, slice(None))]
        r0 = (2 * j + i) * 2 * blk
        return [(p_ref, slice(r0, r0 + 2 * blk), slice(None)),
                (e_ref, rws, slice(2 * i * LANES, (2 * i + 1) * LANES)),
                (e_ref, rws, slice((2 * i + 1) * LANES, (2 * i + 2) * LANES))]

    def put(self, key, *vals):
        for (ref, r, c), v in zip(self._slots(key), vals):
            ref[r, c] = v

    def get(self, key):
        return tuple(ref[r, c] for ref, r, c in self._slots(key))


def _block_diag(a, b):
    za = jnp.zeros(a.shape, a.dtype)
    return jnp.concatenate([jnp.concatenate([a, za], axis=1), jnp.concatenate([za, b], axis=1)], axis=0)


def _front_items(m: _MixRefs, layer, r0, blk, trow0, win0, lk, valid_fn, stash):
    z_ref = m.z
    cq_ref, sq_ref, ck_ref, sk_ref = m.tabs
    decay_ref, qdec_ref, kdec_ref, _ = m.dec
    ka_ref, kb_ref, kc_ref, kd_ref = m.kv_bufs[:4]
    rws = slice(r0, r0 + blk)
    trs = slice(trow0, trow0 + blk)
    win = slice(win0, win0 + lk)

    def ret_front(hp):
        cq, sq, ck, sk = cq_ref[trs, :], sq_ref[trs, :], ck_ref[trs, :], sk_ref[trs, :]
        qs, ks, vs, qds, kds = [], [], [], [], []
        for hd in (2 * hp, 2 * hp + 1):
            c = hd * RET_DK
            q = z_ref[rws, OFF_RQ + c:OFF_RQ + c + RET_DK]
            k = z_ref[rws, OFF_RK + c:OFF_RK + c + RET_DK]
            q = q * cq + pltpu.roll(q, RET_DK // 2, 1) * sq
            k = k * ck + pltpu.roll(k, RET_DK // 2, 1) * sk
            qs.append(q.astype(BF16))
            ks.append(k.astype(BF16))
            vs.append(z_ref[rws, OFF_RV + c:OFF_RV + c + RET_DV].astype(BF16))
            qds.append((q * qdec_ref[hd]).astype(BF16))
            kds.append(k * kdec_ref[hd])
        s2 = _dot_nt(jnp.concatenate(qs, axis=1), _block_diag(*ks)) * decay_ref[hp]
        lhs = jnp.concatenate([s2.astype(BF16)] + qds, axis=1)
        kd_t = jnp.concatenate(kds, axis=0).T.astype(BF16)
        stash.put(("ret", hp), lhs, kd_t, jnp.concatenate(vs, axis=1))

    def gates():
        stash.put(("gate", 0), jnp.concatenate([_silu(z_ref[rws, OFF_RG:OFF_RG + RET_WIDTH]),
                                                _silu(z_ref[rws, OFF_AG:OFF_AG + ATT_WIDTH])], axis=1))

    def att_front(kv):
        low = _low_lanes()
        valid = None if valid_fn is None else valid_fn()
        slabs = (2 * kv, 2 * kv + 1)
        k_lo, k_hi = (ka_ref, kb_ref) if kv == 0 else (kc_ref, kd_ref)
        qf = jnp.concatenate([z_ref[rws, OFF_AQ + sl * LANES:OFF_AQ + (sl + 1) * LANES] for sl in slabs], axis=0)
        qf = (qf * (ATT_HEAD_DIM ** -0.5 * LOG2E)).astype(BF16)
        keys = jnp.concatenate([k_lo[win, :], k_hi[win, :]], axis=0)
        s = _dot_nt(qf, keys)
        ps, es = [], []
        for i, sl in enumerate(slabs):
            row_p, row_e = [], []
            for h in range(2):
                sh = s[i * blk:(i + 1) * blk, h * lk:(h + 1) * lk]
                if valid is not None:
                    sh = jnp.where(valid, sh, MASKED)
                sink = m.sink[layer, 2 * sl + h] * LOG2E
                mh = jnp.maximum(jnp.max(sh, axis=-1, keepdims=True), sink)
                row_p.append(jnp.exp2(sh - mh).astype(BF16))
                row_e.append(jnp.exp2(sink - mh))
            ps.append(jnp.concatenate(row_p, axis=1))
            es.append(jnp.broadcast_to(jnp.where(low, row_e[0], row_e[1]), (blk, LANES)))
        stash.put(("att", kv), jnp.concatenate(ps, axis=0), *es)

    return ([functools.partial(ret_front, hp) for hp in range(RET_HEADS // 2)] + [gates]
            + [functools.partial(att_front, kv) for kv in range(ATT_KV_HEADS)])


def _back_items(m: _MixRefs, r0, blk, win0, lk, r_get, r_set, stash):
    mix_ref = m.mix
    gl_ref = m.dec[3]
    va_ref, vb_ref, vc_ref, vd_ref = m.kv_bufs[4:]
    rws = slice(r0, r0 + blk)
    win = slice(win0, win0 + lk)

    def ret_back(hp):
        heads = (2 * hp, 2 * hp + 1)
        lhs, kd_t, v2 = stash.get(("ret", hp))
        gate = stash.get(("gate", 0))[0]
        r_old = [r_get(hd) for hd in heads]
        v_bd = _block_diag(v2[:, :RET_DV], v2[:, RET_DV:])
        r_bd = _block_diag(*[r.astype(BF16) for r in r_old])
        o2 = _dot(lhs, jnp.concatenate([v_bd, r_bd], axis=0))
        r_inc = _dot(kd_t, v_bd)
        for i, hd in enumerate(heads):
            c = hd * RET_DK
            r_set(hd, gl_ref[hd] * r_old[i] + r_inc[:, i * RET_DV:(i + 1) * RET_DV])
            o = o2[:, i * RET_DV:(i + 1) * RET_DV]
            mu = jnp.mean(o, axis=-1, keepdims=True)
            oc = o - mu
            var = jnp.mean(oc * oc, axis=-1, keepdims=True)
            on = oc * lax.rsqrt(var + GN_EPS)
            mix_ref[rws, c:c + RET_DV] = (on * gate[:, c:c + RET_DV]).astype(BF16)

    def att_back(kv):
        v_lo, v_hi = (va_ref, vb_ref) if kv == 0 else (vc_ref, vd_ref)
        p, *es = stash.get(("att", kv))
        gate = stash.get(("gate", 0))[0]
        vals = jnp.concatenate([v_lo[win, :], v_hi[win, :]], axis=0)
        acc = _dot(p, vals)
        for i, sl in enumerate((2 * kv, 2 * kv + 1)):
            c = RET_WIDTH + sl * LANES
            a = acc[i * blk:(i + 1) * blk, :]
            den = a[:, LANES:] + es[i]
            mix_ref[rws, c:c + LANES] = (a[:, :LANES] / den * gate[:, c:c + LANES]).astype(BF16)

    return ([functools.partial(ret_back, hp) for hp in range(RET_HEADS // 2)]
            + [functools.partial(att_back, kv) for kv in range(ATT_KV_HEADS)])


def _final_norm_rows(xo_ref, r0, n, fg_row):
    for a in range(r0, r0 + n, NORM_ROWS):
        xn = xo_ref[a:a + NORM_ROWS, :]
        ms = jnp.mean(xn * xn, axis=-1, keepdims=True)
        xo_ref[a:a + NORM_ROWS, :] = xn * lax.rsqrt(ms + NORM_EPS) * fg_row


class _PromptCfg(NamedTuple):
    rows: int
    nt: int
    ntiles: int
    last: bool
    layer: int


N_STASH = 6
N_KV = 8


def _prompt_kernel(xn_ref, xc_ref, modn_ref, modc_ref, g_ref, win_ref, wout_ref,
                   cq_ref, sq_ref, ck_ref, sk_ref, decay_ref, qdec_ref, kdec_ref, gl_ref, sink_ref, fg_ref,
                   xo_ref, ro_ref, ko_ref, vo_ref,
                   za_ref, zb_ref, h_ref, mixa_ref, mixb_ref, r_scr, *bufs,
                   cfg: _PromptCfg):
    rows, nt, ntiles = cfg.rows, cfg.nt, cfg.ntiles
    blk, lk = PROMPT_BLOCK, WINDOW + PROMPT_BLOCK
    stash_refs = (bufs[0:N_STASH], bufs[N_STASH:2 * N_STASH])
    kv_refs = (bufs[2 * N_STASH:2 * N_STASH + N_KV], bufs[2 * N_STASH + N_KV:2 * N_STASH + 2 * N_KV])
    z_refs = (za_ref, zb_ref)
    mix_refs = (mixa_ref, mixb_ref)
    g = pl.program_id(0)
    t = jnp.clip(g - 1, 0, ntiles - 1) % nt
    t3 = jnp.clip(g - 2, 0, ntiles - 1) % nt

    @pl.when(g == 0)
    def _():
        zb_ref[...] = jnp.zeros(zb_ref.shape, F32)
        mixb_ref[...] = jnp.zeros(mixb_ref.shape, BF16)
        for b in stash_refs[0] + kv_refs[0]:
            b[...] = jnp.zeros(b.shape, b.dtype)
        stash_refs[0][-1][...] = jnp.ones(stash_refs[0][-1].shape, F32)

    @pl.when(t3 == 0)
    def _():
        r_scr[...] = jnp.zeros(r_scr.shape, F32)

    def r_set(hd, val):
        r_scr[hd] = val

    def visible(j):
        col_i = lax.broadcasted_iota(jnp.int32, (blk, lk), 1)
        ok = None
        if blk == 2 * CHUNK:
            row_i = lax.broadcasted_iota(jnp.int32, (blk, lk), 0)
            first_key = jnp.where(row_i < CHUNK, 0, CHUNK)
            ok = (col_i >= first_key) & (col_i < first_key + WINDOW + CHUNK)
        if j * blk < WINDOW:
            started = col_i >= jnp.where(t == 0, WINDOW - j * blk, 0)
            ok = started if ok is None else ok & started
        return ok

    def step(p):
        q = 1 - p
        tabs = (cq_ref, sq_ref, ck_ref, sk_ref)
        dec = (decay_ref, qdec_ref, kdec_ref, gl_ref)

        norm = _norm_items(xn_ref, h_ref, 0, rows, g_ref, modn_ref, 0)
        proj = _in_proj_items(h_ref, win_ref, z_refs[p])

        def out_item(c0):
            cs = slice(c0, c0 + PROJ_COLS)
            gate_row = modc_ref[0][:, 2 * D_MODEL + c0:2 * D_MODEL + c0 + PROJ_COLS]
            xo_ref[:, cs] = xc_ref[:, cs] + gate_row * _dot(mix_refs[q][...], wout_ref[:, cs])
        outp = [functools.partial(out_item, c0) for c0 in range(0, D_MODEL, PROJ_COLS)]

        zr_ref = z_refs[q]
        for dst, src in zip(kv_refs[q], kv_refs[p]):
            dst[0:WINDOW, :] = src[rows:rows + WINDOW, :]
        for r0 in range(0, rows, CHUNK):
            _fill_kv(kv_refs[q], WINDOW + r0, CHUNK, zr_ref[r0:r0 + CHUNK, OFF_AK:OFF_AK + KV_WIDTH],
                     zr_ref[r0:r0 + CHUNK, OFF_AV:OFF_AV + KV_WIDTH])
        front = []
        for j in range(rows // blk):
            front += _front_items(_MixRefs(zr_ref, None, tabs, dec, sink_ref, kv_refs[q]), cfg.layer,
                                  j * blk, blk, j * blk, j * blk, lk, functools.partial(visible, j),
                                  _RefStash(stash_refs[q], j, blk))

        back = []
        for j in range(rows // blk):
            back += _back_items(_MixRefs(None, mix_refs[p], tabs, dec, sink_ref, kv_refs[p]),
                                j * blk, blk, j * blk, lk, lambda hd: r_scr[hd], r_set,
                                _RefStash(stash_refs[p], j, blk))

        _emit(_interleave(norm, outp))
        _emit(_interleave(_interleave(front, back), proj))
        if cfg.last:
            _final_norm_rows(xo_ref, 0, rows, fg_ref[...])

        @pl.when(t == nt - 1)
        def _():
            ko_ref[...] = zr_ref[rows - WINDOW:rows, OFF_AK:OFF_AK + KV_WIDTH]
            vo_ref[...] = zr_ref[rows - WINDOW:rows, OFF_AV:OFF_AV + KV_WIDTH]

    for parity in (0, 1):
        @pl.when(g % 2 == parity)
        def _():
            step(parity)

    @pl.when((t3 == nt - 1) & (g <= ntiles + 1))
    def _():
        ro_ref[...] = r_scr[...]


def _stash_scratch(rows, blk, lk):
    nblk = rows // blk
    return [
        pltpu.VMEM((rows, 2 * (2 * blk + 2 * RET_DK)), BF16),
        pltpu.VMEM((nblk * 2 * RET_DK, 2 * blk), BF16),
        pltpu.VMEM((rows, RET_WIDTH), BF16),
        pltpu.VMEM((rows, MIX_WIDTH), F32),
        pltpu.VMEM((nblk * 2 * 2 * blk, 2 * lk), BF16),
        pltpu.VMEM((rows, ATT_WIDTH), F32),
    ]


def _kv_scratch(krows):
    return ([pltpu.VMEM((krows, LANES), BF16) for _ in range(4)]
            + [pltpu.VMEM((krows, 2 * LANES), BF16) for _ in range(4)])


def _const_spec(shape):
    nd = len(shape)
    return pl.BlockSpec(shape, lambda g: (0,) * nd)


def _layer_spec(shape, layer):
    nd = len(shape)
    return pl.BlockSpec((None, *shape), lambda g: (layer,) + (0,) * nd)


def _smem_spec():
    return pl.BlockSpec(memory_space=pltpu.SMEM)


def _prompt_layer(layer, last, x2d, mod, g, w_in, w_out, tabs, dec, sink, final_g, batch, seq):
    rows = PROMPT_ROWS
    blk, lk = PROMPT_BLOCK, WINDOW + PROMPT_BLOCK
    nt = seq // rows
    ntiles = batch * nt
    cfg = _PromptCfg(rows=rows, nt=nt, ntiles=ntiles, last=last, layer=layer)
    cq, sq, ck, sk = tabs
    decay, qdec, kdec, gl = dec

    nxt = lambda g: jnp.minimum(g, ntiles - 1)
    cur = lambda g: jnp.clip(g - 1, 0, ntiles - 1)
    bak = lambda g: jnp.clip(g - 2, 0, ntiles - 1)
    fin = lambda g: jnp.maximum(g - 3, 0)
    tab_spec = pl.BlockSpec((rows, LANES), lambda g: (cur(g) % nt, 0))
    in_specs = [
        pl.BlockSpec((rows, D_MODEL), lambda g: (nxt(g), 0)),
        pl.BlockSpec((rows, D_MODEL), lambda g: (fin(g), 0)),
        pl.BlockSpec((None, 1, 1, 3 * D_MODEL), lambda g: (layer, nxt(g) // nt, 0, 0)),
        pl.BlockSpec((None, 1, 1, 3 * D_MODEL), lambda g: (layer, fin(g) // nt, 0, 0)),
        _layer_spec((1, D_MODEL), layer),
        _layer_spec((D_MODEL, IN_WIDTH), layer),
        _layer_spec((MIX_WIDTH, D_MODEL), layer),
        tab_spec, tab_spec, tab_spec, tab_spec,
        _const_spec(decay.shape), _const_spec(qdec.shape), _const_spec(kdec.shape),
        _smem_spec(), _smem_spec(),
        _const_spec((1, D_MODEL)),
    ]
    out_specs = [
        pl.BlockSpec((rows, D_MODEL), lambda g: (fin(g), 0)),
        pl.BlockSpec((None, RET_HEADS, RET_DK, RET_DV), lambda g: (bak(g) // nt, 0, 0, 0)),
        pl.BlockSpec((None, WINDOW, KV_WIDTH), lambda g: (cur(g) // nt, 0, 0)),
        pl.BlockSpec((None, WINDOW, KV_WIDTH), lambda g: (cur(g) // nt, 0, 0)),
    ]
    out_shape = [
        jax.ShapeDtypeStruct((batch * seq, D_MODEL), F32),
        jax.ShapeDtypeStruct((batch, RET_HEADS, RET_DK, RET_DV), F32),
        jax.ShapeDtypeStruct((batch, WINDOW, KV_WIDTH), F32),
        jax.ShapeDtypeStruct((batch, WINDOW, KV_WIDTH), F32),
    ]
    scratch = [
        pltpu.VMEM((rows, IN_WIDTH), F32),
        pltpu.VMEM((rows, IN_WIDTH), F32),
        pltpu.VMEM((rows, D_MODEL), BF16),
        pltpu.VMEM((rows, MIX_WIDTH), BF16),
        pltpu.VMEM((rows, MIX_WIDTH), BF16),
        pltpu.VMEM((RET_HEADS, RET_DK, RET_DV), F32),
        *_stash_scratch(rows, blk, lk), *_stash_scratch(rows, blk, lk),
        *_kv_scratch(WINDOW + rows), *_kv_scratch(WINDOW + rows),
    ]
    return pl.pallas_call(
        functools.partial(_prompt_kernel, cfg=cfg),
        grid=(ntiles + PROMPT_STAGES - 1,),
        in_specs=in_specs, out_specs=out_specs, out_shape=out_shape,
        scratch_shapes=scratch,
        compiler_params=pltpu.CompilerParams(
            dimension_semantics=("arbitrary",),
            vmem_limit_bytes=VMEM_LIMIT_BYTES),
        name=f"prompt_layer{layer}",
    )(x2d, x2d, mod, mod, g, w_in, w_out, cq, sq, ck, sk, decay, qdec, kdec, gl, sink, final_g)


class _SampleCfg(NamedTuple):
    nseq: int
    seq: int
    last: bool
    layer: int


def _sample_kernel(x_ref, mod_ref, g_ref, win_ref, wout_ref,
                   cq_ref, sq_ref, ck_ref, sk_ref, decay_ref, qdec_ref, kdec_ref, gl_ref, sink_ref, fg_ref,
                   st_ref, cachek_ref, cachev_ref,
                   xo_ref, ro_ref, ko_ref, vo_ref,
                   z_ref, h_ref, mix_ref, *kv_bufs,
                   cfg: _SampleCfg):
    nseq, seq = cfg.nseq, cfg.seq
    lk = SAMPLE_KEYS
    pad = lk - WINDOW - seq

    for s in range(nseq):
        _emit(_norm_items(x_ref, h_ref, s * seq, seq, g_ref, mod_ref, s))
    _emit(_in_proj_items(h_ref, win_ref, z_ref))

    def visible():
        return lax.broadcasted_iota(jnp.int32, (seq, lk), 1) >= pad

    m = _MixRefs(z_ref, mix_ref, (cq_ref, sq_ref, ck_ref, sk_ref),
                 (decay_ref, qdec_ref, kdec_ref, gl_ref), sink_ref, kv_bufs)
    for s in range(nseq):
        rws = slice(s * seq, (s + 1) * seq)
        k_new = z_ref[rws, OFF_AK:OFF_AK + KV_WIDTH]
        v_new = z_ref[rws, OFF_AV:OFF_AV + KV_WIDTH]
        for b in kv_bufs:
            b[s * lk:s * lk + pad, :] = jnp.zeros((pad, b.shape[1]), BF16)
        _fill_kv(kv_bufs, s * lk + pad, WINDOW, cachek_ref[s], cachev_ref[s])
        _fill_kv(kv_bufs, s * lk + pad + WINDOW, seq, k_new, v_new)

        def r_set(hd, val, s=s):
            ro_ref[s, hd] = val

        stash = _ValueStash()
        _emit(_front_items(m, cfg.layer, s * seq, seq, 0, s * lk, lk, visible, stash))
        _emit(_back_items(m, s * seq, seq, s * lk, lk, lambda hd, s=s: st_ref[s, hd], r_set, stash))
        ko_ref[s, 0:WINDOW - seq, :] = cachek_ref[s, seq:WINDOW, :]
        vo_ref[s, 0:WINDOW - seq, :] = cachev_ref[s, seq:WINDOW, :]
        ko_ref[s, WINDOW - seq:WINDOW, :] = k_new
        vo_ref[s, WINDOW - seq:WINDOW, :] = v_new

    xo_ref[...] = _dot(mix_ref[...], wout_ref[...])
    for s in range(nseq):
        rws = slice(s * seq, (s + 1) * seq)
        gate_row = mod_ref[s][:, 2 * D_MODEL:3 * D_MODEL]
        xo_ref[rws, :] = x_ref[rws, :] + gate_row * xo_ref[rws, :]
    if cfg.last:
        _final_norm_rows(xo_ref, 0, nseq * seq, fg_ref[...])


def _sample_layer(layer, last, x2d, mod, g, w_in, w_out, tabs, dec, sink, final_g,
                  state, cache_k, cache_v, batch, seq, mod_row0):
    nseq = SAMPLE_SEQS
    rows = nseq * seq
    cfg = _SampleCfg(nseq=nseq, seq=seq, last=last, layer=layer)
    cq, sq, ck, sk = tabs
    decay, qdec, kdec, gl = dec
    mod_blk0 = mod_row0 // nseq
    in_specs = [
        pl.BlockSpec((rows, D_MODEL), lambda i: (i, 0)),
        pl.BlockSpec((None, nseq, 1, 3 * D_MODEL), lambda i: (layer, mod_blk0 + i, 0, 0)),
        _layer_spec((1, D_MODEL), layer),
        _layer_spec((D_MODEL, IN_WIDTH), layer),
        _layer_spec((MIX_WIDTH, D_MODEL), layer),
        _const_spec(cq.shape), _const_spec(sq.shape), _const_spec(ck.shape), _const_spec(sk.shape),
        _const_spec(decay.shape), _const_spec(qdec.shape), _const_spec(kdec.shape),
        _smem_spec(), _smem_spec(),
        _const_spec((1, D_MODEL)),
        pl.BlockSpec((None, nseq, RET_HEADS, RET_DK, RET_DV), lambda i: (layer, i, 0, 0, 0)),
        pl.BlockSpec((None, nseq, WINDOW, KV_WIDTH), lambda i: (layer, i, 0, 0)),
        pl.BlockSpec((None, nseq, WINDOW, KV_WIDTH), lambda i: (layer, i, 0, 0)),
    ]
    out_specs = [
        pl.BlockSpec((rows, D_MODEL), lambda i: (i, 0)),
        pl.BlockSpec((nseq, RET_HEADS, RET_DK, RET_DV), lambda i: (i, 0, 0, 0)),
        pl.BlockSpec((nseq, WINDOW, KV_WIDTH), lambda i: (i, 0, 0)),
        pl.BlockSpec((nseq, WINDOW, KV_WIDTH), lambda i: (i, 0, 0)),
    ]
    out_shape = [
        jax.ShapeDtypeStruct((batch * seq, D_MODEL), F32),
        jax.ShapeDtypeStruct((batch, RET_HEADS, RET_DK, RET_DV), F32),
        jax.ShapeDtypeStruct((batch, WINDOW, KV_WIDTH), F32),
        jax.ShapeDtypeStruct((batch, WINDOW, KV_WIDTH), F32),
    ]
    scratch = [
        pltpu.VMEM((rows, IN_WIDTH), F32),
        pltpu.VMEM((rows, D_MODEL), BF16),
        pltpu.VMEM((rows, MIX_WIDTH), BF16),
        *_kv_scratch(nseq * SAMPLE_KEYS),
    ]
    return pl.pallas_call(
        functools.partial(_sample_kernel, cfg=cfg),
        grid=(batch // nseq,),
        in_specs=in_specs, out_specs=out_specs, out_shape=out_shape,
        scratch_shapes=scratch,
        compiler_params=pltpu.CompilerParams(
            dimension_semantics=("arbitrary",),
            vmem_limit_bytes=VMEM_LIMIT_BYTES),
        name=f"sample_layer{layer}",
    )(x2d, mod, g, w_in, w_out, cq, sq, ck, sk, decay, qdec, kdec, gl, sink, final_g,
      state, cache_k, cache_v)


def _rope_tables(start, n):
    d = RET_DK
    inv = 1.0 / (ROPE_BASE ** (jnp.arange(0, d, 2, dtype=F32) / d))
    ang_a = jnp.arange(start, start + n, CHUNK).astype(F32)[:, None] * inv[None, :]
    ang_b = jnp.arange(CHUNK).astype(F32)[:, None] * inv[None, :]
    ca, sa, cb, sb = lax.optimization_barrier((jnp.cos(ang_a), jnp.sin(ang_a), jnp.cos(ang_b), jnp.sin(ang_b)))
    cos = (ca[:, None, :] * cb[None] - sa[:, None, :] * sb[None]).reshape(n, d // 2)
    sin = (sa[:, None, :] * cb[None] + ca[:, None, :] * sb[None]).reshape(n, d // 2)
    cos2 = jnp.concatenate([cos, cos], axis=-1)
    sin2 = jnp.concatenate([-sin, sin], axis=-1)
    kscale = RET_DK ** -0.5
    return cos2, sin2, cos2 * kscale, sin2 * kscale


def _decay_tables(n):
    lg = jnp.log(1.0 - 2.0 ** (-5.0 - jnp.arange(RET_HEADS, dtype=F32)))
    idx = jnp.arange(n, dtype=F32)
    diff = idx[:, None] - idx[None, :]
    decay = jnp.where(diff[None] >= 0, jnp.exp(jnp.maximum(diff, 0.0)[None] * lg[:, None, None]), 0.0)
    qdec = jnp.exp((idx + 1.0)[None, :] * lg[:, None])
    kdec = jnp.exp((n - 1.0 - idx)[None, :] * lg[:, None])
    gl = jnp.exp(n * lg)
    bcast = lambda a: jnp.broadcast_to(a[:, :, None], (RET_HEADS, n, LANES))
    decay2 = jnp.concatenate([decay[0::2], decay[1::2]], axis=-1)
    return decay2, bcast(qdec), bcast(kdec), gl


def kernel(x_prompt, x_sample, c_prompt, c_sample, state_ret, cache_k, cache_v,
           norm_g, w_ada, b_ada, w_in, sink, w_out, final_g):
    batch, seq, _ = x_prompt.shape
    dbatch, dseq, _ = x_sample.shape

    c_all = jnp.concatenate([c_prompt, c_sample], axis=0)
    mod = _adaln(c_all, w_ada, b_ada).reshape(DEPTH, batch + dbatch, 1, 3 * D_MODEL)
    w_in_b = w_in.astype(BF16)
    w_out_b = w_out.astype(BF16)

    tabs_p = _rope_tables(0, seq)
    tabs_s = _rope_tables(PAST_LEN, dseq)
    dec_p = _decay_tables(PROMPT_BLOCK)
    dec_s = _decay_tables(dseq)
    ck4 = cache_k.reshape(DEPTH, dbatch, WINDOW, KV_WIDTH)
    cv4 = cache_v.reshape(DEPTH, dbatch, WINDOW, KV_WIDTH)
    fg = final_g.reshape(1, D_MODEL)
    g_all = norm_g.reshape(DEPTH, 1, D_MODEL)

    xp = x_prompt.reshape(batch * seq, D_MODEL)
    xs = x_sample.reshape(dbatch * dseq, D_MODEL)
    rp, kp, vp, rs, ks, vs = [], [], [], [], [], []
    for l in range(DEPTH):
        last = l == DEPTH - 1
        xp, r1, k1, v1 = _prompt_layer(l, last, xp, mod, g_all, w_in_b, w_out_b, tabs_p, dec_p,
                                       sink, fg, batch, seq)
        xs, r2, k2, v2 = _sample_layer(l, last, xs, mod, g_all, w_in_b, w_out_b, tabs_s, dec_s,
                                       sink, fg, state_ret, ck4, cv4, dbatch, dseq, mod_row0=batch)
        rp.append(r1); kp.append(k1); vp.append(v1)
        rs.append(r2); ks.append(k2); vs.append(v2)

    kv_shape_p = (DEPTH, batch, WINDOW, ATT_KV_HEADS, ATT_HEAD_DIM)
    kv_shape_s = (DEPTH, dbatch, WINDOW, ATT_KV_HEADS, ATT_HEAD_DIM)
    return (xp.reshape(batch, seq, D_MODEL), xs.reshape(dbatch, dseq, D_MODEL),
            jnp.stack(rp), jnp.stack(kp).reshape(kv_shape_p), jnp.stack(vp).reshape(kv_shape_p),
            jnp.stack(rs), jnp.stack(ks).reshape(kv_shape_s), jnp.stack(vs).reshape(kv_shape_s))
```

```python
import functools
from typing import NamedTuple

import jax
import jax.numpy as jnp
from jax import lax
from jax.experimental import pallas as pl
from jax.experimental.pallas import tpu as pltpu

D_MODEL = 1024
DEPTH = 4
CHUNK = 64
PAST_LEN = 4096
RET_HEADS = 4
RET_DK = 128
RET_DV = 128
RET_WIDTH = RET_HEADS * RET_DV
ATT_HEAD_DIM = 64
ATT_Q_HEADS = 8
ATT_KV_HEADS = 2
ATT_WIDTH = ATT_Q_HEADS * ATT_HEAD_DIM
KV_WIDTH = ATT_KV_HEADS * ATT_HEAD_DIM
WINDOW = 128
MIX_WIDTH = RET_WIDTH + ATT_WIDTH
ROPE_BASE = 10000.0
NORM_EPS = 1e-6
GN_EPS = 1e-5

OFF_RQ = 0
OFF_RK = OFF_RQ + RET_HEADS * RET_DK
OFF_RV = OFF_RK + RET_HEADS * RET_DK
OFF_RG = OFF_RV + RET_WIDTH
OFF_AQ = OFF_RG + RET_WIDTH
OFF_AK = OFF_AQ + ATT_WIDTH
OFF_AV = OFF_AK + KV_WIDTH
OFF_AG = OFF_AV + KV_WIDTH
IN_WIDTH = OFF_AG + ATT_WIDTH

LANES = 128
ATT_SLABS = ATT_WIDTH // LANES
MASKED = -1e30
LOG2E = 1.4426950408889634
VMEM_LIMIT_BYTES = 56 * 1024 * 1024
NORM_ROWS = 32
PROJ_COLS = 512

PROMPT_ROWS = 256
PROMPT_BLOCK = 128
PROMPT_STAGES = 4
SAMPLE_SEQS = 4
SAMPLE_KEYS = 2 * LANES

BF16 = jnp.bfloat16
F32 = jnp.float32


def _silu(g):
    return g * (1.0 / (1.0 + jnp.exp(-g)))


def _dot(a, b):
    return jnp.dot(a, b, preferred_element_type=F32)


def _dot_nt(a, b):
    return lax.dot_general(a, b, (((1,), (1,)), ((), ())), preferred_element_type=F32)


def _low_lanes():
    lane = lax.broadcasted_iota(jnp.int32, (1, LANES), 1)
    return lane < ATT_HEAD_DIM


def _adaln_kernel(c_ref, w_ref, b_ref, o_ref):
    s = _silu(c_ref[...]).astype(BF16)
    o_ref[0] = _dot(s, w_ref[0].astype(BF16)) + b_ref[0]


def _adaln(c_all, w_ada, b_ada):
    n = c_all.shape[0]
    tn = D_MODEL
    return pl.pallas_call(
        _adaln_kernel,
        grid=(DEPTH, 3 * D_MODEL // tn),
        in_specs=[
            pl.BlockSpec((n, D_MODEL), lambda l, j: (0, 0)),
            pl.BlockSpec((1, D_MODEL, tn), lambda l, j: (l, 0, j)),
            pl.BlockSpec((1, 1, tn), lambda l, j: (l, 0, j)),
        ],
        out_specs=pl.BlockSpec((1, n, tn), lambda l, j: (l, 0, j)),
        out_shape=jax.ShapeDtypeStruct((DEPTH, n, 3 * D_MODEL), F32),
        compiler_params=pltpu.CompilerParams(
            dimension_semantics=("arbitrary", "arbitrary"),
            vmem_limit_bytes=VMEM_LIMIT_BYTES),
        name="adaln",
    )(c_all, w_ada, b_ada.reshape(DEPTH, 1, 3 * D_MODEL))


def _emit(items):
    for it in items:
        it()


def _interleave(a, b):
    if not a:
        return list(b)
    out, nb = [], 0
    for i, it in enumerate(a):
        out.append(it)
        want = (i + 1) * len(b) // len(a)
        out.extend(b[nb:want])
        nb = want
    return out


def _norm_items(x_ref, h_ref, r0, n, g_ref, mod_ref, s):
    def item(a):
        mod_row = mod_ref[s]
        shift = mod_row[:, 0:D_MODEL]
        scale1 = 1.0 + mod_row[:, D_MODEL:2 * D_MODEL]
        xb = x_ref[a:a + NORM_ROWS, :]
        ms = jnp.mean(xb * xb, axis=-1, keepdims=True)
        y = xb * lax.rsqrt(ms + NORM_EPS) * g_ref[...]
        h_ref[a:a + NORM_ROWS, :] = (y * scale1 + shift).astype(BF16)
    return [functools.partial(item, a) for a in range(r0, r0 + n, NORM_ROWS)]


def _in_proj_items(h_ref, win_ref, z_ref):
    def item(c0):
        c1 = min(c0 + PROJ_COLS, IN_WIDTH)
        z_ref[:, c0:c1] = _dot(h_ref[...], win_ref[:, c0:c1])
    return [functools.partial(item, c0) for c0 in range(0, IN_WIDTH, PROJ_COLS)]


def _fill_kv(kv_bufs, dst0, n, k, v):
    ka_ref, kb_ref, kc_ref, kd_ref, va_ref, vb_ref, vc_ref, vd_ref = kv_bufs
    low = _low_lanes()
    kr = pltpu.roll(k, ATT_HEAD_DIM, 1)
    vr = pltpu.roll(v, ATT_HEAD_DIM, 1)
    d = slice(dst0, dst0 + n)
    ka_ref[d, :] = jnp.where(low, k, 0.0).astype(BF16)
    kb_ref[d, :] = jnp.where(low, 0.0, kr).astype(BF16)
    kc_ref[d, :] = jnp.where(low, kr, 0.0).astype(BF16)
    kd_ref[d, :] = jnp.where(low, 0.0, k).astype(BF16)
    one_lo = jnp.broadcast_to(jnp.where(low, 1.0, 0.0), (n, LANES))
    one_hi = 1.0 - one_lo
    va_ref[d, :] = jnp.concatenate([jnp.where(low, v, 0.0), one_lo], axis=1).astype(BF16)
    vb_ref[d, :] = jnp.concatenate([jnp.where(low, 0.0, vr), one_hi], axis=1).astype(BF16)
    vc_ref[d, :] = jnp.concatenate([jnp.where(low, vr, 0.0), one_lo], axis=1).astype(BF16)
    vd_ref[d, :] = jnp.concatenate([jnp.where(low, 0.0, v), one_hi], axis=1).astype(BF16)


class _MixRefs(NamedTuple):
    z: object
    mix: object
    tabs: tuple
    dec: tuple
    sink: object
    kv_bufs: tuple


class _ValueStash:
    def __init__(self):
        self.d = {}

    def put(self, key, *vals):
        self.d[key] = vals

    def get(self, key):
        return self.d[key]


class _RefStash:
    def __init__(self, refs, j, blk):
        self.refs, self.j, self.blk = refs, j, blk
        self.rws = slice(j * blk, (j + 1) * blk)

    def _slots(self, key):
        lhs_ref, kdt_ref, v_ref, gate_ref, p_ref, e_ref = self.refs
        kind, i = key
        rws, j, blk = self.rws, self.j, self.blk
        if kind == "ret":
            w = lhs_ref.shape[1] // 2
            r0 = (2 * j + i) * RET_DK
            return [(lhs_ref, rws, slice(i * w, (i + 1) * w)), (kdt_ref, slice(r0, r0 + RET_DK), slice(None)),
                    (v_ref, rws, slice(2 * i * RET_DV, 2 * (i + 1) * RET_DV))]
        if kind == "gate":
            return [(gate_ref, rws, slice(None))]
        r0 = (2 * j + i) * 2 * blk
        return [(p_ref, slice(r0, r0 + 2 * blk), slice(None)),
                (e_ref, rws, slice(2 * i * LANES, (2 * i + 1) * LANES)),
                (e_ref, rws, slice((2 * i + 1) * LANES, (2 * i + 2) * LANES))]

    def put(self, key, *vals):
        for (ref, r, c), v in zip(self._slots(key), vals):
            ref[r, c] = v

    def get(self, key):
        return tuple(ref[r, c] for ref, r, c in self._slots(key))


def _block_diag(a, b):
    za = jnp.zeros(a.shape, a.dtype)
    return jnp.concatenate([jnp.concatenate([a, za], axis=1), jnp.concatenate([za, b], axis=1)], axis=0)


def _front_items(m: _MixRefs, layer, r0, blk, trow0, win0, lk, valid_fn, stash):
    z_ref = m.z
    cq_ref, sq_ref, ck_ref, sk_ref = m.tabs
    decay_ref, qdec_ref, kdec_ref, _ = m.dec
    ka_ref, kb_ref, kc_ref, kd_ref = m.kv_bufs[:4]
    rws = slice(r0, r0 + blk)
    trs = slice(trow0, trow0 + blk)
    win = slice(win0, win0 + lk)

    def ret_front(hp):
        cq, sq, ck, sk = cq_ref[trs, :], sq_ref[trs, :], ck_ref[trs, :], sk_ref[trs, :]
        qs, ks, vs, qds, kds = [], [], [], [], []
        for hd in (2 * hp, 2 * hp + 1):
            c = hd * RET_DK
            q = z_ref[rws, OFF_RQ + c:OFF_RQ + c + RET_DK]
            k = z_ref[rws, OFF_RK + c:OFF_RK + c + RET_DK]
            q = q * cq + pltpu.roll(q, RET_DK // 2, 1) * sq
            k = k * ck + pltpu.roll(k, RET_DK // 2, 1) * sk
            qs.append(q.astype(BF16))
            ks.append(k.astype(BF16))
            vs.append(z_ref[rws, OFF_RV + c:OFF_RV + c + RET_DV].astype(BF16))
            qds.append((q * qdec_ref[hd]).astype(BF16))
            kds.append(k * kdec_ref[hd])
        s2 = _dot_nt(jnp.concatenate(qs, axis=1), _block_diag(*ks)) * decay_ref[hp]
        lhs = jnp.concatenate([s2.astype(BF16)] + qds, axis=1)
        kd_t = jnp.concatenate(kds, axis=0).T.astype(BF16)
        stash.put(("ret", hp), lhs, kd_t, jnp.concatenate(vs, axis=1))

    def gates():
        stash.put(("gate", 0), jnp.concatenate([_silu(z_ref[rws, OFF_RG:OFF_RG + RET_WIDTH]),
                                                _silu(z_ref[rws, OFF_AG:OFF_AG + ATT_WIDTH])], axis=1))

    def att_front(kv):
        low = _low_lanes()
        valid = None if valid_fn is None else valid_fn()
        slabs = (2 * kv, 2 * kv + 1)
        k_lo, k_hi = (ka_ref, kb_ref) if kv == 0 else (kc_ref, kd_ref)
        qf = jnp.concatenate([z_ref[rws, OFF_AQ + sl * LANES:OFF_AQ + (sl + 1) * LANES] for sl in slabs], axis=0)
        qf = (qf * (ATT_HEAD_DIM ** -0.5 * LOG2E)).astype(BF16)
        keys = jnp.concatenate([k_lo[win, :], k_hi[win, :]], axis=0)
        s = _dot_nt(qf, keys)
        ps, es = [], []
        for i, sl in enumerate(slabs):
            row_p, row_e = [], []
            for h in range(2):
                sh = s[i * blk:(i + 1) * blk, h * lk:(h + 1) * lk]
                if valid is not None:
                    sh = jnp.where(valid, sh, MASKED)
                sink = m.sink[layer, 2 * sl + h] * LOG2E
                mh = jnp.maximum(jnp.max(sh, axis=-1, keepdims=True), sink)
                row_p.append(jnp.exp2(sh - mh).astype(BF16))
                row_e.append(jnp.exp2(sink - mh))
            ps.append(jnp.concatenate(row_p, axis=1))
            es.append(jnp.broadcast_to(jnp.where(low, row_e[0], row_e[1]), (blk, LANES)))
        stash.put(("att", kv), jnp.concatenate(ps, axis=0), *es)

    return ([functools.partial(ret_front, hp) for hp in range(RET_HEADS // 2)] + [gates]
            + [functools.partial(att_front, kv) for kv in range(ATT_KV_HEADS)])


def _back_items(m: _MixRefs, r0, blk, win0, lk, r_get, r_set, stash):
    mix_ref = m.mix
    gl_ref = m.dec[3]
    va_ref, vb_ref, vc_ref, vd_ref = m.kv_bufs[4:]
    rws = slice(r0, r0 + blk)
    win = slice(win0, win0 + lk)

    def ret_back(hp):
        heads = (2 * hp, 2 * hp + 1)
        lhs, kd_t, v2 = stash.get(("ret", hp))
        gate = stash.get(("gate", 0))[0]
        r_old = [r_get(hd) for hd in heads]
        v_bd = _block_diag(v2[:, :RET_DV], v2[:, RET_DV:])
        r_bd = _block_diag(*[r.astype(BF16) for r in r_old])
        o2 = _dot(lhs, jnp.concatenate([v_bd, r_bd], axis=0))
        r_inc = _dot(kd_t, v_bd)
        for i, hd in enumerate(heads):
            c = hd * RET_DK
            r_set(hd, gl_ref[hd] * r_old[i] + r_inc[:, i * RET_DV:(i + 1) * RET_DV])
            o = o2[:, i * RET_DV:(i + 1) * RET_DV]
            mu = jnp.mean(o, axis=-1, keepdims=True)
            oc = o - mu
            var = jnp.mean(oc * oc, axis=-1, keepdims=True)
            on = oc * lax.rsqrt(var + GN_EPS)
            mix_ref[rws, c:c + RET_DV] = (on * gate[:, c:c + RET_DV]).astype(BF16)

    def att_back(kv):
        v_lo, v_hi = (va_ref, vb_ref) if kv == 0 else (vc_ref, vd_ref)
        p, *es = stash.get(("att", kv))
        gate = stash.get(("gate", 0))[0]
        vals = jnp.concatenate([v_lo[win, :], v_hi[win, :]], axis=0)
        acc = _dot(p, vals)
        for i, sl in enumerate((2 * kv, 2 * kv + 1)):
            c = RET_WIDTH + sl * LANES
            a = acc[i * blk:(i + 1) * blk, :]
            den = a[:, LANES:] + es[i]
            mix_ref[rws, c:c + LANES] = (a[:, :LANES] / den * gate[:, c:c + LANES]).astype(BF16)

    return ([functools.partial(ret_back, hp) for hp in range(RET_HEADS // 2)]
            + [functools.partial(att_back, kv) for kv in range(ATT_KV_HEADS)])


def _final_norm_rows(xo_ref, r0, n, fg_row):
    for a in range(r0, r0 + n, NORM_ROWS):
        xn = xo_ref[a:a + NORM_ROWS, :]
        ms = jnp.mean(xn * xn, axis=-1, keepdims=True)
        xo_ref[a:a + NORM_ROWS, :] = xn * lax.rsqrt(ms + NORM_EPS) * fg_row


class _PromptCfg(NamedTuple):
    rows: int
    nt: int
    ntiles: int
    last: bool
    layer: int


N_STASH = 6
N_KV = 8


def _prompt_kernel(xn_ref, xc_ref, modn_ref, modc_ref, g_ref, win_ref, wout_ref,
                   cq_ref, sq_ref, ck_ref, sk_ref, decay_ref, qdec_ref, kdec_ref, gl_ref, sink_ref, fg_ref,
                   _ro_prev, _ko_prev, _vo_prev,
                   xo_ref, ro_ref, ko_ref, vo_ref,
                   za_ref, zb_ref, h_ref, mixa_ref, mixb_ref, r_scr, *bufs,
                   cfg: _PromptCfg):
    rows, nt, ntiles = cfg.rows, cfg.nt, cfg.ntiles
    blk, lk = PROMPT_BLOCK, WINDOW + PROMPT_BLOCK
    stash_refs = (bufs[0:N_STASH], bufs[N_STASH:2 * N_STASH])
    kv_refs = (bufs[2 * N_STASH:2 * N_STASH + N_KV], bufs[2 * N_STASH + N_KV:2 * N_STASH + 2 * N_KV])
    z_refs = (za_ref, zb_ref)
    mix_refs = (mixa_ref, mixb_ref)
    g = pl.program_id(0)
    t = jnp.clip(g - 1, 0, ntiles - 1) % nt
    t3 = jnp.clip(g - 2, 0, ntiles - 1) % nt

    @pl.when(g == 0)
    def _():
        zb_ref[...] = jnp.zeros(zb_ref.shape, F32)
        mixb_ref[...] = jnp.zeros(mixb_ref.shape, BF16)
        for b in stash_refs[0] + kv_refs[0]:
            b[...] = jnp.zeros(b.shape, b.dtype)
        stash_refs[0][-1][...] = jnp.ones(stash_refs[0][-1].shape, F32)

    @pl.when(t3 == 0)
    def _():
        r_scr[...] = jnp.zeros(r_scr.shape, F32)

    def r_set(hd, val):
        r_scr[hd] = val

    def visible(j):
        col_i = lax.broadcasted_iota(jnp.int32, (blk, lk), 1)
        ok = None
        if blk == 2 * CHUNK:
            row_i = lax.broadcasted_iota(jnp.int32, (blk, lk), 0)
            first_key = jnp.where(row_i < CHUNK, 0, CHUNK)
            ok = (col_i >= first_key) & (col_i < first_key + WINDOW + CHUNK)
        if j * blk < WINDOW:
            started = col_i >= jnp.where(t == 0, WINDOW - j * blk, 0)
            ok = started if ok is None else ok & started
        return ok

    def step(p):
        q = 1 - p
        tabs = (cq_ref, sq_ref, ck_ref, sk_ref)
        dec = (decay_ref, qdec_ref, kdec_ref, gl_ref)

        norm = _norm_items(xn_ref, h_ref, 0, rows, g_ref, modn_ref, 0)
        proj = _in_proj_items(h_ref, win_ref, z_refs[p])

        def out_item(c0):
            cs = slice(c0, c0 + PROJ_COLS)
            gate_row = modc_ref[0][:, 2 * D_MODEL + c0:2 * D_MODEL + c0 + PROJ_COLS]
            xo_ref[:, cs] = xc_ref[:, cs] + gate_row * _dot(mix_refs[q][...], wout_ref[:, cs])
        outp = [functools.partial(out_item, c0) for c0 in range(0, D_MODEL, PROJ_COLS)]

        zr_ref = z_refs[q]
        for dst, src in zip(kv_refs[q], kv_refs[p]):
            dst[0:WINDOW, :] = src[rows:rows + WINDOW, :]
        for r0 in range(0, rows, CHUNK):
            _fill_kv(kv_refs[q], WINDOW + r0, CHUNK, zr_ref[r0:r0 + CHUNK, OFF_AK:OFF_AK + KV_WIDTH],
                     zr_ref[r0:r0 + CHUNK, OFF_AV:OFF_AV + KV_WIDTH])
        front = []
        for j in range(rows // blk):
            front += _front_items(_MixRefs(zr_ref, None, tabs, dec, sink_ref, kv_refs[q]), cfg.layer,
                                  j * blk, blk, j * blk, j * blk, lk, functools.partial(visible, j),
                                  _RefStash(stash_refs[q], j, blk))

        back = []
        for j in range(rows // blk):
            back += _back_items(_MixRefs(None, mix_refs[p], tabs, dec, sink_ref, kv_refs[p]),
                                j * blk, blk, j * blk, lk, lambda hd: r_scr[hd], r_set,
                                _RefStash(stash_refs[p], j, blk))

        _emit(_interleave(norm, outp))
        _emit(_interleave(_interleave(front, back), proj))
        if cfg.last:
            _final_norm_rows(xo_ref, 0, rows, fg_ref[...])

        @pl.when(t == nt - 1)
        def _():
            ko_ref[...] = zr_ref[rows - WINDOW:rows, OFF_AK:OFF_AK + KV_WIDTH]
            vo_ref[...] = zr_ref[rows - WINDOW:rows, OFF_AV:OFF_AV + KV_WIDTH]

    for parity in (0, 1):
        @pl.when(g % 2 == parity)
        def _():
            step(parity)

    @pl.when((t3 == nt - 1) & (g <= ntiles + 1))
    def _():
        ro_ref[...] = r_scr[...]


def _stash_scratch(rows, blk, lk):
    nblk = rows // blk
    return [
        pltpu.VMEM((rows, 2 * (2 * blk + 2 * RET_DK)), BF16),
        pltpu.VMEM((nblk * 2 * RET_DK, 2 * blk), BF16),
        pltpu.VMEM((rows, RET_WIDTH), BF16),
        pltpu.VMEM((rows, MIX_WIDTH), F32),
        pltpu.VMEM((nblk * 2 * 2 * blk, 2 * lk), BF16),
        pltpu.VMEM((rows, ATT_WIDTH), F32),
    ]


def _kv_scratch(krows):
    return ([pltpu.VMEM((krows, LANES), BF16) for _ in range(4)]
            + [pltpu.VMEM((krows, 2 * LANES), BF16) for _ in range(4)])


def _const_spec(shape):
    nd = len(shape)
    return pl.BlockSpec(shape, lambda g: (0,) * nd)


def _layer_spec(shape, layer):
    nd = len(shape)
    return pl.BlockSpec((None, *shape), lambda g: (layer,) + (0,) * nd)


def _smem_spec():
    return pl.BlockSpec(memory_space=pltpu.SMEM)


def _state_alias(n_in, acc):
    in_specs = [pl.BlockSpec(memory_space=pl.ANY)] * len(acc)
    aliases = {n_in + i: 1 + i for i in range(len(acc))}
    out_shape = [jax.ShapeDtypeStruct(a.shape, a.dtype) for a in acc]
    return in_specs, aliases, out_shape


def _prompt_layer(layer, last, x2d, mod, g, w_in, w_out, tabs, dec, sink, final_g, acc, batch, seq):
    rows = PROMPT_ROWS
    blk, lk = PROMPT_BLOCK, WINDOW + PROMPT_BLOCK
    nt = seq // rows
    ntiles = batch * nt
    cfg = _PromptCfg(rows=rows, nt=nt, ntiles=ntiles, last=last, layer=layer)
    cq, sq, ck, sk = tabs
    decay, qdec, kdec, gl = dec

    nxt = lambda g: jnp.minimum(g, ntiles - 1)
    cur = lambda g: jnp.clip(g - 1, 0, ntiles - 1)
    bak = lambda g: jnp.clip(g - 2, 0, ntiles - 1)
    fin = lambda g: jnp.maximum(g - 3, 0)
    tab_spec = pl.BlockSpec((rows, LANES), lambda g: (cur(g) % nt, 0))
    in_specs = [
        pl.BlockSpec((rows, D_MODEL), lambda g: (nxt(g), 0)),
        pl.BlockSpec((rows, D_MODEL), lambda g: (fin(g), 0)),
        pl.BlockSpec((None, 1, 1, 3 * D_MODEL), lambda g: (layer, nxt(g) // nt, 0, 0)),
        pl.BlockSpec((None, 1, 1, 3 * D_MODEL), lambda g: (layer, fin(g) // nt, 0, 0)),
        _layer_spec((1, D_MODEL), layer),
        _layer_spec((D_MODEL, IN_WIDTH), layer),
        _layer_spec((MIX_WIDTH, D_MODEL), layer),
        tab_spec, tab_spec, tab_spec, tab_spec,
        _const_spec(decay.shape), _const_spec(qdec.shape), _const_spec(kdec.shape),
        _smem_spec(), _smem_spec(),
        _const_spec((1, D_MODEL)),
    ]
    acc_specs, aliases, acc_shape = _state_alias(len(in_specs), acc)
    out_specs = [
        pl.BlockSpec((rows, D_MODEL), lambda g: (fin(g), 0)),
        pl.BlockSpec((None, None, RET_HEADS, RET_DK, RET_DV), lambda g: (layer, bak(g) // nt, 0, 0, 0)),
        pl.BlockSpec((None, None, WINDOW, KV_WIDTH), lambda g: (layer, cur(g) // nt, 0, 0)),
        pl.BlockSpec((None, None, WINDOW, KV_WIDTH), lambda g: (layer, cur(g) // nt, 0, 0)),
    ]
    out_shape = [jax.ShapeDtypeStruct((batch * seq, D_MODEL), F32)] + acc_shape
    scratch = [
        pltpu.VMEM((rows, IN_WIDTH), F32),
        pltpu.VMEM((rows, IN_WIDTH), F32),
        pltpu.VMEM((rows, D_MODEL), BF16),
        pltpu.VMEM((rows, MIX_WIDTH), BF16),
        pltpu.VMEM((rows, MIX_WIDTH), BF16),
        pltpu.VMEM((RET_HEADS, RET_DK, RET_DV), F32),
        *_stash_scratch(rows, blk, lk), *_stash_scratch(rows, blk, lk),
        *_kv_scratch(WINDOW + rows), *_kv_scratch(WINDOW + rows),
    ]
    return pl.pallas_call(
        functools.partial(_prompt_kernel, cfg=cfg),
        grid=(ntiles + PROMPT_STAGES - 1,),
        in_specs=in_specs + acc_specs, out_specs=out_specs, out_shape=out_shape,
        input_output_aliases=aliases,
        scratch_shapes=scratch,
        compiler_params=pltpu.CompilerParams(
            dimension_semantics=("arbitrary",),
            vmem_limit_bytes=VMEM_LIMIT_BYTES),
        name=f"prompt_layer{layer}",
    )(x2d, x2d, mod, mod, g, w_in, w_out, cq, sq, ck, sk, decay, qdec, kdec, gl, sink, final_g, *acc)


class _SampleCfg(NamedTuple):
    nseq: int
    seq: int
    last: bool
    layer: int


def _sample_kernel(x_ref, mod_ref, g_ref, win_ref, wout_ref,
                   cq_ref, sq_ref, ck_ref, sk_ref, decay_ref, qdec_ref, kdec_ref, gl_ref, sink_ref, fg_ref,
                   st_ref, cachek_ref, cachev_ref,
                   _ro_prev, _ko_prev, _vo_prev,
                   xo_ref, ro_ref, ko_ref, vo_ref,
                   z_ref, h_ref, mix_ref, *kv_bufs,
                   cfg: _SampleCfg):
    nseq, seq = cfg.nseq, cfg.seq
    lk = SAMPLE_KEYS
    pad = lk - WINDOW - seq

    for s in range(nseq):
        _emit(_norm_items(x_ref, h_ref, s * seq, seq, g_ref, mod_ref, s))
    _emit(_in_proj_items(h_ref, win_ref, z_ref))

    def visible():
        return lax.broadcasted_iota(jnp.int32, (seq, lk), 1) >= pad

    m = _MixRefs(z_ref, mix_ref, (cq_ref, sq_ref, ck_ref, sk_ref),
                 (decay_ref, qdec_ref, kdec_ref, gl_ref), sink_ref, kv_bufs)
    for s in range(nseq):
        rws = slice(s * seq, (s + 1) * seq)
        k_new = z_ref[rws, OFF_AK:OFF_AK + KV_WIDTH]
        v_new = z_ref[rws, OFF_AV:OFF_AV + KV_WIDTH]
        for b in kv_bufs:
            b[s * lk:s * lk + pad, :] = jnp.zeros((pad, b.shape[1]), BF16)
        _fill_kv(kv_bufs, s * lk + pad, WINDOW, cachek_ref[s], cachev_ref[s])
        _fill_kv(kv_bufs, s * lk + pad + WINDOW, seq, k_new, v_new)

        def r_set(hd, val, s=s):
            ro_ref[s, hd] = val

        stash = _ValueStash()
        _emit(_front_items(m, cfg.layer, s * seq, seq, 0, s * lk, lk, visible, stash))
        _emit(_back_items(m, s * seq, seq, s * lk, lk, lambda hd, s=s: st_ref[s, hd], r_set, stash))
        ko_ref[s, 0:WINDOW - seq, :] = cachek_ref[s, seq:WINDOW, :]
        vo_ref[s, 0:WINDOW - seq, :] = cachev_ref[s, seq:WINDOW, :]
        ko_ref[s, WINDOW - seq:WINDOW, :] = k_new
        vo_ref[s, WINDOW - seq:WINDOW, :] = v_new

    xo_ref[...] = _dot(mix_ref[...], wout_ref[...])
    for s in range(nseq):
        rws = slice(s * seq, (s + 1) * seq)
        gate_row = mod_ref[s][:, 2 * D_MODEL:3 * D_MODEL]
        xo_ref[rws, :] = x_ref[rws, :] + gate_row * xo_ref[rws, :]
    if cfg.last:
        _final_norm_rows(xo_ref, 0, nseq * seq, fg_ref[...])


def _sample_layer(layer, last, x2d, mod, g, w_in, w_out, tabs, dec, sink, final_g,
                  state, cache_k, cache_v, acc, batch, seq, mod_row0):
    nseq = SAMPLE_SEQS
    rows = nseq * seq
    cfg = _SampleCfg(nseq=nseq, seq=seq, last=last, layer=layer)
    cq, sq, ck, sk = tabs
    decay, qdec, kdec, gl = dec
    mod_blk0 = mod_row0 // nseq
    in_specs = [
        pl.BlockSpec((rows, D_MODEL), lambda i: (i, 0)),
        pl.BlockSpec((None, nseq, 1, 3 * D_MODEL), lambda i: (layer, mod_blk0 + i, 0, 0)),
        _layer_spec((1, D_MODEL), layer),
        _layer_spec((D_MODEL, IN_WIDTH), layer),
        _layer_spec((MIX_WIDTH, D_MODEL), layer),
        _const_spec(cq.shape), _const_spec(sq.shape), _const_spec(ck.shape), _const_spec(sk.shape),
        _const_spec(decay.shape), _const_spec(qdec.shape), _const_spec(kdec.shape),
        _smem_spec(), _smem_spec(),
        _const_spec((1, D_MODEL)),
        pl.BlockSpec((None, nseq, RET_HEADS, RET_DK, RET_DV), lambda i: (layer, i, 0, 0, 0)),
        pl.BlockSpec((None, nseq, WINDOW, KV_WIDTH), lambda i: (layer, i, 0, 0)),
        pl.BlockSpec((None, nseq, WINDOW, KV_WIDTH), lambda i: (layer, i, 0, 0)),
    ]
    acc_specs, aliases, acc_shape = _state_alias(len(in_specs), acc)
    out_specs = [
        pl.BlockSpec((rows, D_MODEL), lambda i: (i, 0)),
        pl.BlockSpec((None, nseq, RET_HEADS, RET_DK, RET_DV), lambda i: (layer, i, 0, 0, 0)),
        pl.BlockSpec((None, nseq, WINDOW, KV_WIDTH), lambda i: (layer, i, 0, 0)),
        pl.BlockSpec((None, nseq, WINDOW, KV_WIDTH), lambda i: (layer, i, 0, 0)),
    ]
    out_shape = [jax.ShapeDtypeStruct((batch * seq, D_MODEL), F32)] + acc_shape
    scratch = [
        pltpu.VMEM((rows, IN_WIDTH), F32),
        pltpu.VMEM((rows, D_MODEL), BF16),
        pltpu.VMEM((rows, MIX_WIDTH), BF16),
        *_kv_scratch(nseq * SAMPLE_KEYS),
    ]
    return pl.pallas_call(
        functools.partial(_sample_kernel, cfg=cfg),
        grid=(batch // nseq,),
        in_specs=in_specs + acc_specs, out_specs=out_specs, out_shape=out_shape,
        input_output_aliases=aliases,
        scratch_shapes=scratch,
        compiler_params=pltpu.CompilerParams(
            dimension_semantics=("arbitrary",),
            vmem_limit_bytes=VMEM_LIMIT_BYTES),
        name=f"sample_layer{layer}",
    )(x2d, mod, g, w_in, w_out, cq, sq, ck, sk, decay, qdec, kdec, gl, sink, final_g,
      state, cache_k, cache_v, *acc)


def _rope_tables(start, n):
    d = RET_DK
    inv = 1.0 / (ROPE_BASE ** (jnp.arange(0, d, 2, dtype=F32) / d))
    ang_a = jnp.arange(start, start + n, CHUNK).astype(F32)[:, None] * inv[None, :]
    ang_b = jnp.arange(CHUNK).astype(F32)[:, None] * inv[None, :]
    ca, sa, cb, sb = lax.optimization_barrier((jnp.cos(ang_a), jnp.sin(ang_a), jnp.cos(ang_b), jnp.sin(ang_b)))
    cos = (ca[:, None, :] * cb[None] - sa[:, None, :] * sb[None]).reshape(n, d // 2)
    sin = (sa[:, None, :] * cb[None] + ca[:, None, :] * sb[None]).reshape(n, d // 2)
    cos2 = jnp.concatenate([cos, cos], axis=-1)
    sin2 = jnp.concatenate([-sin, sin], axis=-1)
    kscale = RET_DK ** -0.5
    return cos2, sin2, cos2 * kscale, sin2 * kscale


def _decay_tables(n):
    lg = jnp.log(1.0 - 2.0 ** (-5.0 - jnp.arange(RET_HEADS, dtype=F32)))
    idx = jnp.arange(n, dtype=F32)
    diff = idx[:, None] - idx[None, :]
    decay = jnp.where(diff[None] >= 0, jnp.exp(jnp.maximum(diff, 0.0)[None] * lg[:, None, None]), 0.0)
    qdec = jnp.exp((idx + 1.0)[None, :] * lg[:, None])
    kdec = jnp.exp((n - 1.0 - idx)[None, :] * lg[:, None])
    gl = jnp.exp(n * lg)
    bcast = lambda a: jnp.broadcast_to(a[:, :, None], (RET_HEADS, n, LANES))
    decay2 = jnp.concatenate([decay[0::2], decay[1::2]], axis=-1)
    return decay2, bcast(qdec), bcast(kdec), gl


def kernel(x_prompt, x_sample, c_prompt, c_sample, state_ret, cache_k, cache_v,
           norm_g, w_ada, b_ada, w_in, sink, w_out, final_g):
    batch, seq, _ = x_prompt.shape
    dbatch, dseq, _ = x_sample.shape

    c_all = jnp.concatenate([c_prompt, c_sample], axis=0)
    mod = _adaln(c_all, w_ada, b_ada).reshape(DEPTH, batch + dbatch, 1, 3 * D_MODEL)
    w_in_b = w_in.astype(BF16)
    w_out_b = w_out.astype(BF16)

    tabs_p = _rope_tables(0, seq)
    tabs_s = _rope_tables(PAST_LEN, dseq)
    dec_p = _decay_tables(PROMPT_BLOCK)
    dec_s = _decay_tables(dseq)
    ck4 = cache_k.reshape(DEPTH, dbatch, WINDOW, KV_WIDTH)
    cv4 = cache_v.reshape(DEPTH, dbatch, WINDOW, KV_WIDTH)
    fg = final_g.reshape(1, D_MODEL)
    g_all = norm_g.reshape(DEPTH, 1, D_MODEL)

    xp = x_prompt.reshape(batch * seq, D_MODEL)
    xs = x_sample.reshape(dbatch * dseq, D_MODEL)
    def state_acc(nb):
        return (jnp.zeros((DEPTH, nb, RET_HEADS, RET_DK, RET_DV), F32),
                jnp.zeros((DEPTH, nb, WINDOW, KV_WIDTH), F32), jnp.zeros((DEPTH, nb, WINDOW, KV_WIDTH), F32))

    acc_p, acc_s = state_acc(batch), state_acc(dbatch)
    for l in range(DEPTH):
        last = l == DEPTH - 1
        xp, *acc_p = _prompt_layer(l, last, xp, mod, g_all, w_in_b, w_out_b, tabs_p, dec_p,
                                   sink, fg, acc_p, batch, seq)
        xs, *acc_s = _sample_layer(l, last, xs, mod, g_all, w_in_b, w_out_b, tabs_s, dec_s,
                                   sink, fg, state_ret, ck4, cv4, acc_s, dbatch, dseq, mod_row0=batch)

    kv_shape_p = (DEPTH, batch, WINDOW, ATT_KV_HEADS, ATT_HEAD_DIM)
    kv_shape_s = (DEPTH, dbatch, WINDOW, ATT_KV_HEADS, ATT_HEAD_DIM)
    return (xp.reshape(batch, seq, D_MODEL), xs.reshape(dbatch, dseq, D_MODEL),
            acc_p[0], acc_p[1].reshape(kv_shape_p), acc_p[2].reshape(kv_shape_p),
            acc_s[0], acc_s[1].reshape(kv_shape_s), acc_s[2].reshape(kv_shape_s))
```

```python
import functools
from typing import NamedTuple

import jax
import jax.numpy as jnp
from jax import lax
from jax.experimental import pallas as pl
from jax.experimental.pallas import tpu as pltpu

D_MODEL = 1024
DEPTH = 4
CHUNK = 64
PAST_LEN = 4096
RET_HEADS = 4
RET_DK = 128
RET_DV = 128
RET_WIDTH = RET_HEADS * RET_DV
ATT_HEAD_DIM = 64
ATT_Q_HEADS = 8
ATT_KV_HEADS = 2
ATT_WIDTH = ATT_Q_HEADS * ATT_HEAD_DIM
KV_WIDTH = ATT_KV_HEADS * ATT_HEAD_DIM
WINDOW = 128
MIX_WIDTH = RET_WIDTH + ATT_WIDTH
ROPE_BASE = 10000.0
NORM_EPS = 1e-6
GN_EPS = 1e-5

OFF_RQ = 0
OFF_RK = OFF_RQ + RET_HEADS * RET_DK
OFF_RV = OFF_RK + RET_HEADS * RET_DK
OFF_RG = OFF_RV + RET_WIDTH
OFF_AQ = OFF_RG + RET_WIDTH
OFF_AK = OFF_AQ + ATT_WIDTH
OFF_AV = OFF_AK + KV_WIDTH
OFF_AG = OFF_AV + KV_WIDTH
IN_WIDTH = OFF_AG + ATT_WIDTH

LANES = 128
ATT_SLABS = ATT_WIDTH // LANES
MASKED = -1e30
LOG2E = 1.4426950408889634
VMEM_LIMIT_BYTES = 56 * 1024 * 1024
NORM_ROWS = 32
PROJ_COLS = 512

PROMPT_ROWS = 256
PROMPT_BLOCK = 128
PROMPT_STAGES = 4
SAMPLE_SEQS = 4
SAMPLE_KEYS = 2 * LANES

BF16 = jnp.bfloat16
F32 = jnp.float32


def _silu(g):
    return g * (1.0 / (1.0 + jnp.exp(-g)))


def _dot(a, b):
    return jnp.dot(a, b, preferred_element_type=F32)


def _dot_nt(a, b):
    return lax.dot_general(a, b, (((1,), (1,)), ((), ())), preferred_element_type=F32)


def _low_lanes():
    lane = lax.broadcasted_iota(jnp.int32, (1, LANES), 1)
    return lane < ATT_HEAD_DIM


def _adaln_kernel(c_ref, w_ref, b_ref, o_ref):
    s = _silu(c_ref[...]).astype(BF16)
    o_ref[0] = _dot(s, w_ref[0].astype(BF16)) + b_ref[0]


def _adaln(c_all, w_ada, b_ada):
    n = c_all.shape[0]
    tn = D_MODEL
    return pl.pallas_call(
        _adaln_kernel,
        grid=(DEPTH, 3 * D_MODEL // tn),
        in_specs=[
            pl.BlockSpec((n, D_MODEL), lambda l, j: (0, 0)),
            pl.BlockSpec((1, D_MODEL, tn), lambda l, j: (l, 0, j)),
            pl.BlockSpec((1, 1, tn), lambda l, j: (l, 0, j)),
        ],
        out_specs=pl.BlockSpec((1, n, tn), lambda l, j: (l, 0, j)),
        out_shape=jax.ShapeDtypeStruct((DEPTH, n, 3 * D_MODEL), F32),
        compiler_params=pltpu.CompilerParams(
            dimension_semantics=("arbitrary", "arbitrary"),
            vmem_limit_bytes=VMEM_LIMIT_BYTES),
        name="adaln",
    )(c_all, w_ada, b_ada.reshape(DEPTH, 1, 3 * D_MODEL))


def _emit(items):
    for it in items:
        it()


def _interleave(a, b):
    if not a:
        return list(b)
    out, nb = [], 0
    for i, it in enumerate(a):
        out.append(it)
        want = (i + 1) * len(b) // len(a)
        out.extend(b[nb:want])
        nb = want
    return out


def _norm_items(x_ref, h_ref, r0, n, g_ref, mod_ref, s):
    def item(a):
        mod_row = mod_ref[s]
        shift = mod_row[:, 0:D_MODEL]
        scale1 = 1.0 + mod_row[:, D_MODEL:2 * D_MODEL]
        xb = x_ref[a:a + NORM_ROWS, :]
        ms = jnp.mean(xb * xb, axis=-1, keepdims=True)
        y = xb * lax.rsqrt(ms + NORM_EPS) * g_ref[...]
        h_ref[a:a + NORM_ROWS, :] = (y * scale1 + shift).astype(BF16)
    return [functools.partial(item, a) for a in range(r0, r0 + n, NORM_ROWS)]


def _in_proj_items(h_ref, win_ref, z_ref):
    def item(c0):
        c1 = min(c0 + PROJ_COLS, IN_WIDTH)
        z_ref[:, c0:c1] = _dot(h_ref[...], win_ref[:, c0:c1])
    return [functools.partial(item, c0) for c0 in range(0, IN_WIDTH, PROJ_COLS)]


def _fill_kv(kv_bufs, dst0, n, k, v):
    ka_ref, kb_ref, kc_ref, kd_ref, va_ref, vb_ref, vc_ref, vd_ref = kv_bufs
    low = _low_lanes()
    kr = pltpu.roll(k, ATT_HEAD_DIM, 1)
    vr = pltpu.roll(v, ATT_HEAD_DIM, 1)
    d = slice(dst0, dst0 + n)
    ka_ref[d, :] = jnp.where(low, k, 0.0).astype(BF16)
    kb_ref[d, :] = jnp.where(low, 0.0, kr).astype(BF16)
    kc_ref[d, :] = jnp.where(low, kr, 0.0).astype(BF16)
    kd_ref[d, :] = jnp.where(low, 0.0, k).astype(BF16)
    one_lo = jnp.broadcast_to(jnp.where(low, 1.0, 0.0), (n, LANES))
    one_hi = 1.0 - one_lo
    va_ref[d, :] = jnp.concatenate([jnp.where(low, v, 0.0), one_lo], axis=1).astype(BF16)
    vb_ref[d, :] = jnp.concatenate([jnp.where(low, 0.0, vr), one_hi], axis=1).astype(BF16)
    vc_ref[d, :] = jnp.concatenate([jnp.where(low, vr, 0.0), one_lo], axis=1).astype(BF16)
    vd_ref[d, :] = jnp.concatenate([jnp.where(low, 0.0, v), one_hi], axis=1).astype(BF16)


class _MixRefs(NamedTuple):
    z: object
    mix: object
    tabs: tuple
    dec: tuple
    sink: object
    kv_bufs: tuple


class _ValueStash:
    def __init__(self):
        self.d = {}

    def put(self, key, *vals):
        self.d[key] = vals

    def get(self, key):
        return self.d[key]


class _RefStash:
    def __init__(self, refs, j, blk):
        self.refs, self.j, self.blk = refs, j, blk
        self.rws = slice(j * blk, (j + 1) * blk)

    def _slots(self, key):
        lhs_ref, kdt_ref, v_ref, gate_ref, p_ref, e_ref = self.refs
        kind, i = key
        rws, j, blk = self.rws, self.j, self.blk
        if kind == "ret":
            w = lhs_ref.shape[1] // 2
            r0 = (2 * j + i) * RET_DK
            return [(lhs_ref, rws, slice(i * w, (i + 1) * w)), (kdt_ref, slice(r0, r0 + RET_DK), slice(None)),
                    (v_ref, rws, slice(2 * i * RET_DV, 2 * (i + 1) * RET_DV))]
        if kind == "gate":
            return [(gate_ref, rws, slice(None))]
        r0 = (2 * j + i) * 2 * blk
        return [(p_ref, slice(r0, r0 + 2 * blk), slice(None)),
                (e_ref, rws, slice(2 * i * LANES, (2 * i + 1) * LANES)),
                (e_ref, rws, slice((2 * i + 1) * LANES, (2 * i + 2) * LANES))]

    def put(self, key, *vals):
        for (ref, r, c), v in zip(self._slots(key), vals):
            ref[r, c] = v

    def get(self, key):
        return tuple(ref[r, c] for ref, r, c in self._slots(key))


def _block_diag(a, b):
    za = jnp.zeros(a.shape, a.dtype)
    return jnp.concatenate([jnp.concatenate([a, za], axis=1), jnp.concatenate([za, b], axis=1)], axis=0)


def _front_items(m: _MixRefs, layer, r0, blk, trow0, win0, lk, valid_fn, stash):
    z_ref = m.z
    cq_ref, sq_ref, ck_ref, sk_ref = m.tabs
    decay_ref, qdec_ref, kdec_ref, _ = m.dec
    ka_ref, kb_ref, kc_ref, kd_ref = m.kv_bufs[:4]
    rws = slice(r0, r0 + blk)
    trs = slice(trow0, trow0 + blk)
    win = slice(win0, win0 + lk)

    def ret_front(hp):
        cq, sq, ck, sk = cq_ref[trs, :], sq_ref[trs, :], ck_ref[trs, :], sk_ref[trs, :]
        qs, ks, vs, qds, kds = [], [], [], [], []
        for hd in (2 * hp, 2 * hp + 1):
            c = hd * RET_DK
            q = z_ref[rws, OFF_RQ + c:OFF_RQ + c + RET_DK]
            k = z_ref[rws, OFF_RK + c:OFF_RK + c + RET_DK]
            q = q * cq + pltpu.roll(q, RET_DK // 2, 1) * sq
            k = k * ck + pltpu.roll(k, RET_DK // 2, 1) * sk
            qs.append(q.astype(BF16))
            ks.append(k.astype(BF16))
            vs.append(z_ref[rws, OFF_RV + c:OFF_RV + c + RET_DV].astype(BF16))
            qds.append((q * qdec_ref[hd]).astype(BF16))
            kds.append(k * kdec_ref[hd])
        s2 = _dot_nt(jnp.concatenate(qs, axis=1), _block_diag(*ks)) * decay_ref[hp]
        lhs = jnp.concatenate([s2.astype(BF16)] + qds, axis=1)
        kd_t = jnp.concatenate(kds, axis=0).T.astype(BF16)
        stash.put(("ret", hp), lhs, kd_t, jnp.concatenate(vs, axis=1))

    def gates():
        stash.put(("gate", 0), jnp.concatenate([_silu(z_ref[rws, OFF_RG:OFF_RG + RET_WIDTH]),
                                                _silu(z_ref[rws, OFF_AG:OFF_AG + ATT_WIDTH])], axis=1))

    def att_front(kv):
        low = _low_lanes()
        valid = None if valid_fn is None else valid_fn()
        slabs = (2 * kv, 2 * kv + 1)
        k_lo, k_hi = (ka_ref, kb_ref) if kv == 0 else (kc_ref, kd_ref)
        qf = jnp.concatenate([z_ref[rws, OFF_AQ + sl * LANES:OFF_AQ + (sl + 1) * LANES] for sl in slabs], axis=0)
        qf = (qf * (ATT_HEAD_DIM ** -0.5 * LOG2E)).astype(BF16)
        keys = jnp.concatenate([k_lo[win, :], k_hi[win, :]], axis=0)
        s = _dot_nt(qf, keys)
        ps, es = [], []
        for i, sl in enumerate(slabs):
            row_p, row_e = [], []
            for h in range(2):
                sh = s[i * blk:(i + 1) * blk, h * lk:(h + 1) * lk]
                if valid is not None:
                    sh = jnp.where(valid, sh, MASKED)
                sink = m.sink[layer, 2 * sl + h] * LOG2E
                mh = jnp.maximum(jnp.max(sh, axis=-1, keepdims=True), sink)
                row_p.append(jnp.exp2(sh - mh).astype(BF16))
                row_e.append(jnp.exp2(sink - mh))
            ps.append(jnp.concatenate(row_p, axis=1))
            es.append(jnp.broadcast_to(jnp.where(low, row_e[0], row_e[1]), (blk, LANES)))
        stash.put(("att", kv), jnp.concatenate(ps, axis=0), *es)

    return ([functools.partial(ret_front, hp) for hp in range(RET_HEADS // 2)] + [gates]
            + [functools.partial(att_front, kv) for kv in range(ATT_KV_HEADS)])


def _back_items(m: _MixRefs, r0, blk, win0, lk, r_get, r_set, stash):
    mix_ref = m.mix
    gl_ref = m.dec[3]
    va_ref, vb_ref, vc_ref, vd_ref = m.kv_bufs[4:]
    rws = slice(r0, r0 + blk)
    win = slice(win0, win0 + lk)

    def ret_back(hp):
        heads = (2 * hp, 2 * hp + 1)
        lhs, kd_t, v2 = stash.get(("ret", hp))
        gate = stash.get(("gate", 0))[0]
        r_old = [r_get(hd) for hd in heads]
        v_bd = _block_diag(v2[:, :RET_DV], v2[:, RET_DV:])
        r_bd = _block_diag(*[r.astype(BF16) for r in r_old])
        o2 = _dot(lhs, jnp.concatenate([v_bd, r_bd], axis=0))
        r_inc = _dot(kd_t, v_bd)
        for i, hd in enumerate(heads):
            c = hd * RET_DK
            r_set(hd, gl_ref[hd] * r_old[i] + r_inc[:, i * RET_DV:(i + 1) * RET_DV])
            o = o2[:, i * RET_DV:(i + 1) * RET_DV]
            mu = jnp.mean(o, axis=-1, keepdims=True)
            oc = o - mu
            var = jnp.mean(oc * oc, axis=-1, keepdims=True)
            on = oc * lax.rsqrt(var + GN_EPS)
            mix_ref[rws, c:c + RET_DV] = (on * gate[:, c:c + RET_DV]).astype(BF16)

    def att_back(kv):
        v_lo, v_hi = (va_ref, vb_ref) if kv == 0 else (vc_ref, vd_ref)
        p, *es = stash.get(("att", kv))
        gate = stash.get(("gate", 0))[0]
        vals = jnp.concatenate([v_lo[win, :], v_hi[win, :]], axis=0)
        acc = _dot(p, vals)
        for i, sl in enumerate((2 * kv, 2 * kv + 1)):
            c = RET_WIDTH + sl * LANES
            a = acc[i * blk:(i + 1) * blk, :]
            den = a[:, LANES:] + es[i]
            mix_ref[rws, c:c + LANES] = (a[:, :LANES] / den * gate[:, c:c + LANES]).astype(BF16)

    return ([functools.partial(ret_back, hp) for hp in range(RET_HEADS // 2)]
            + [functools.partial(att_back, kv) for kv in range(ATT_KV_HEADS)])


def _final_norm_rows(xo_ref, r0, n, fg_row):
    for a in range(r0, r0 + n, NORM_ROWS):
        xn = xo_ref[a:a + NORM_ROWS, :]
        ms = jnp.mean(xn * xn, axis=-1, keepdims=True)
        xo_ref[a:a + NORM_ROWS, :] = xn * lax.rsqrt(ms + NORM_EPS) * fg_row


class _PromptCfg(NamedTuple):
    rows: int
    nt: int
    ntiles: int
    last: bool
    layer: int


N_STASH = 6
N_KV = 8


def _prompt_kernel(xn_ref, xc_ref, modn_ref, modc_ref, g_ref, win_ref, wout_ref,
                   cq_ref, sq_ref, ck_ref, sk_ref, decay_ref, qdec_ref, kdec_ref, gl_ref, sink_ref, fg_ref,
                   _ro_prev, _ko_prev, _vo_prev,
                   xo_ref, ro_ref, ko_ref, vo_ref,
                   za_ref, zb_ref, h_ref, mixa_ref, mixb_ref, r_scr, *bufs,
                   cfg: _PromptCfg):
    rows, nt, ntiles = cfg.rows, cfg.nt, cfg.ntiles
    blk, lk = PROMPT_BLOCK, WINDOW + PROMPT_BLOCK
    stash_refs = (bufs[0:N_STASH], bufs[N_STASH:2 * N_STASH])
    kv_refs = (bufs[2 * N_STASH:2 * N_STASH + N_KV], bufs[2 * N_STASH + N_KV:2 * N_STASH + 2 * N_KV])
    z_refs = (za_ref, zb_ref)
    mix_refs = (mixa_ref, mixb_ref)
    g = pl.program_id(0)
    t = jnp.clip(g - 1, 0, ntiles - 1) % nt
    t3 = jnp.clip(g - 2, 0, ntiles - 1) % nt

    @pl.when(g == 0)
    def _():
        zb_ref[...] = jnp.zeros(zb_ref.shape, F32)
        mixb_ref[...] = jnp.zeros(mixb_ref.shape, BF16)
        for b in stash_refs[0] + kv_refs[0]:
            b[...] = jnp.zeros(b.shape, b.dtype)
        stash_refs[0][-1][...] = jnp.ones(stash_refs[0][-1].shape, F32)

    @pl.when(t3 == 0)
    def _():
        r_scr[...] = jnp.zeros(r_scr.shape, F32)

    def r_set(hd, val):
        r_scr[hd] = val

    def visible(j):
        col_i = lax.broadcasted_iota(jnp.int32, (blk, lk), 1)
        ok = None
        if blk == 2 * CHUNK:
            row_i = lax.broadcasted_iota(jnp.int32, (blk, lk), 0)
            first_key = jnp.where(row_i < CHUNK, 0, CHUNK)
            ok = (col_i >= first_key) & (col_i < first_key + WINDOW + CHUNK)
        if j * blk < WINDOW:
            started = col_i >= jnp.where(t == 0, WINDOW - j * blk, 0)
            ok = started if ok is None else ok & started
        return ok

    def step(p):
        a, b = p, 1 - p
        tabs = (cq_ref, sq_ref, ck_ref, sk_ref)
        dec = (decay_ref, qdec_ref, kdec_ref, gl_ref)

        norm = _norm_items(xn_ref, h_ref, 0, rows, g_ref, modn_ref, 0)
        proj = _in_proj_items(h_ref, win_ref, z_refs[a])

        def out_item(c0):
            cs = slice(c0, c0 + PROJ_COLS)
            gate_row = modc_ref[0][:, 2 * D_MODEL + c0:2 * D_MODEL + c0 + PROJ_COLS]
            xo_ref[:, cs] = xc_ref[:, cs] + gate_row * _dot(mix_refs[b][...], wout_ref[:, cs])
        outp = [functools.partial(out_item, c0) for c0 in range(0, D_MODEL, PROJ_COLS)]

        zr_ref = z_refs[b]
        for dst, src in zip(kv_refs[b], kv_refs[a]):
            dst[0:WINDOW, :] = src[rows:rows + WINDOW, :]
        for r0 in range(0, rows, CHUNK):
            _fill_kv(kv_refs[b], WINDOW + r0, CHUNK, zr_ref[r0:r0 + CHUNK, OFF_AK:OFF_AK + KV_WIDTH],
                     zr_ref[r0:r0 + CHUNK, OFF_AV:OFF_AV + KV_WIDTH])
        front = []
        for j in range(rows // blk):
            front += _front_items(_MixRefs(zr_ref, None, tabs, dec, sink_ref, kv_refs[b]), cfg.layer,
                                  j * blk, blk, j * blk, j * blk, lk, functools.partial(visible, j),
                                  _RefStash(stash_refs[b], j, blk))

        back = []
        for j in range(rows // blk):
            back += _back_items(_MixRefs(None, mix_refs[a], tabs, dec, sink_ref, kv_refs[a]),
                                j * blk, blk, j * blk, lk, lambda hd: r_scr[hd], r_set,
                                _RefStash(stash_refs[a], j, blk))

        _emit(_interleave(norm, outp))
        _emit(_interleave(_interleave(front, back), proj))
        if cfg.last:
            _final_norm_rows(xo_ref, 0, rows, fg_ref[...])

        @pl.when(t == nt - 1)
        def _():
            ko_ref[...] = zr_ref[rows - WINDOW:rows, OFF_AK:OFF_AK + KV_WIDTH]
            vo_ref[...] = zr_ref[rows - WINDOW:rows, OFF_AV:OFF_AV + KV_WIDTH]

    for parity in (0, 1):
        @pl.when(g % 2 == parity)
        def _():
            step(parity)

    @pl.when((t3 == nt - 1) & (g <= ntiles + 1))
    def _():
        ro_ref[...] = r_scr[...]


def _stash_scratch(rows, blk, lk):
    nblk = rows // blk
    return [
        pltpu.VMEM((rows, 2 * (2 * blk + 2 * RET_DK)), BF16),
        pltpu.VMEM((nblk * 2 * RET_DK, 2 * blk), BF16),
        pltpu.VMEM((rows, RET_WIDTH), BF16),
        pltpu.VMEM((rows, MIX_WIDTH), F32),
        pltpu.VMEM((nblk * 2 * 2 * blk, 2 * lk), BF16),
        pltpu.VMEM((rows, ATT_WIDTH), F32),
    ]


def _kv_scratch(krows):
    return ([pltpu.VMEM((krows, LANES), BF16) for _ in range(4)]
            + [pltpu.VMEM((krows, 2 * LANES), BF16) for _ in range(4)])


def _const_spec(shape):
    nd = len(shape)
    return pl.BlockSpec(shape, lambda g: (0,) * nd)


def _layer_spec(shape, layer):
    nd = len(shape)
    return pl.BlockSpec((None, *shape), lambda g: (layer,) + (0,) * nd)


def _smem_spec():
    return pl.BlockSpec(memory_space=pltpu.SMEM)


N_STATE = 3


def _state_alias(kernel, n_in, acc):
    out_shape = [jax.ShapeDtypeStruct(a.shape, a.dtype) for a in acc]
    if all(isinstance(a, jax.ShapeDtypeStruct) for a in acc):
        def first(*refs, **kw):
            return kernel(*refs[:n_in], *([None] * N_STATE), *refs[n_in:], **kw)
        return first, [], [], {}, out_shape
    in_specs = [pl.BlockSpec(memory_space=pl.ANY)] * N_STATE
    aliases = {n_in + i: 1 + i for i in range(N_STATE)}
    return kernel, list(acc), in_specs, aliases, out_shape


def _prompt_layer(layer, last, x2d, mod, g, w_in, w_out, tabs, dec, sink, final_g, acc, batch, seq):
    rows = PROMPT_ROWS
    blk, lk = PROMPT_BLOCK, WINDOW + PROMPT_BLOCK
    nt = seq // rows
    ntiles = batch * nt
    cfg = _PromptCfg(rows=rows, nt=nt, ntiles=ntiles, last=last, layer=layer)
    cq, sq, ck, sk = tabs
    decay, qdec, kdec, gl = dec

    nxt = lambda g: jnp.minimum(g, ntiles - 1)
    cur = lambda g: jnp.clip(g - 1, 0, ntiles - 1)
    bak = lambda g: jnp.clip(g - 2, 0, ntiles - 1)
    fin = lambda g: jnp.maximum(g - 3, 0)
    tab_spec = pl.BlockSpec((rows, LANES), lambda g: (cur(g) % nt, 0))
    in_specs = [
        pl.BlockSpec((rows, D_MODEL), lambda g: (nxt(g), 0)),
        pl.BlockSpec((rows, D_MODEL), lambda g: (fin(g), 0)),
        pl.BlockSpec((None, 1, 1, 3 * D_MODEL), lambda g: (layer, nxt(g) // nt, 0, 0)),
        pl.BlockSpec((None, 1, 1, 3 * D_MODEL), lambda g: (layer, fin(g) // nt, 0, 0)),
        _layer_spec((1, D_MODEL), layer),
        _layer_spec((D_MODEL, IN_WIDTH), layer),
        _layer_spec((MIX_WIDTH, D_MODEL), layer),
        tab_spec, tab_spec, tab_spec, tab_spec,
        _const_spec(decay.shape), _const_spec(qdec.shape), _const_spec(kdec.shape),
        _smem_spec(), _smem_spec(),
        _const_spec((1, D_MODEL)),
    ]
    body, acc_args, acc_specs, aliases, acc_shape = _state_alias(_prompt_kernel, len(in_specs), acc)
    out_specs = [
        pl.BlockSpec((rows, D_MODEL), lambda g: (fin(g), 0)),
        pl.BlockSpec((None, None, RET_HEADS, RET_DK, RET_DV), lambda g: (layer, bak(g) // nt, 0, 0, 0)),
        pl.BlockSpec((None, None, WINDOW, KV_WIDTH), lambda g: (layer, cur(g) // nt, 0, 0)),
        pl.BlockSpec((None, None, WINDOW, KV_WIDTH), lambda g: (layer, cur(g) // nt, 0, 0)),
    ]
    out_shape = [jax.ShapeDtypeStruct((batch * seq, D_MODEL), F32)] + acc_shape
    scratch = [
        pltpu.VMEM((rows, IN_WIDTH), F32),
        pltpu.VMEM((rows, IN_WIDTH), F32),
        pltpu.VMEM((rows, D_MODEL), BF16),
        pltpu.VMEM((rows, MIX_WIDTH), BF16),
        pltpu.VMEM((rows, MIX_WIDTH), BF16),
        pltpu.VMEM((RET_HEADS, RET_DK, RET_DV), F32),
        *_stash_scratch(rows, blk, lk), *_stash_scratch(rows, blk, lk),
        *_kv_scratch(WINDOW + rows), *_kv_scratch(WINDOW + rows),
    ]
    return pl.pallas_call(
        functools.partial(body, cfg=cfg),
        grid=(ntiles + PROMPT_STAGES - 1,),
        in_specs=in_specs + acc_specs, out_specs=out_specs, out_shape=out_shape,
        input_output_aliases=aliases,
        scratch_shapes=scratch,
        compiler_params=pltpu.CompilerParams(
            dimension_semantics=("arbitrary",),
            vmem_limit_bytes=VMEM_LIMIT_BYTES),
        name=f"prompt_layer{layer}",
    )(x2d, x2d, mod, mod, g, w_in, w_out, cq, sq, ck, sk, decay, qdec, kdec, gl, sink, final_g, *acc_args)


class _SampleCfg(NamedTuple):
    nseq: int
    seq: int
    last: bool
    layer: int


def _sample_kernel(x_ref, mod_ref, g_ref, win_ref, wout_ref,
                   cq_ref, sq_ref, ck_ref, sk_ref, decay_ref, qdec_ref, kdec_ref, gl_ref, sink_ref, fg_ref,
                   st_ref, cachek_ref, cachev_ref,
                   _ro_prev, _ko_prev, _vo_prev,
                   xo_ref, ro_ref, ko_ref, vo_ref,
                   z_ref, h_ref, mix_ref, *kv_bufs,
                   cfg: _SampleCfg):
    nseq, seq = cfg.nseq, cfg.seq
    lk = SAMPLE_KEYS
    pad = lk - WINDOW - seq

    for s in range(nseq):
        _emit(_norm_items(x_ref, h_ref, s * seq, seq, g_ref, mod_ref, s))
    _emit(_in_proj_items(h_ref, win_ref, z_ref))

    def visible():
        return lax.broadcasted_iota(jnp.int32, (seq, lk), 1) >= pad

    m = _MixRefs(z_ref, mix_ref, (cq_ref, sq_ref, ck_ref, sk_ref),
                 (decay_ref, qdec_ref, kdec_ref, gl_ref), sink_ref, kv_bufs)
    for s in range(nseq):
        rws = slice(s * seq, (s + 1) * seq)
        k_new = z_ref[rws, OFF_AK:OFF_AK + KV_WIDTH]
        v_new = z_ref[rws, OFF_AV:OFF_AV + KV_WIDTH]
        for b in kv_bufs:
            b[s * lk:s * lk + pad, :] = jnp.zeros((pad, b.shape[1]), BF16)
        _fill_kv(kv_bufs, s * lk + pad, WINDOW, cachek_ref[s], cachev_ref[s])
        _fill_kv(kv_bufs, s * lk + pad + WINDOW, seq, k_new, v_new)

        def r_set(hd, val, s=s):
            ro_ref[s, hd] = val

        stash = _ValueStash()
        _emit(_front_items(m, cfg.layer, s * seq, seq, 0, s * lk, lk, visible, stash))
        _emit(_back_items(m, s * seq, seq, s * lk, lk, lambda hd, s=s: st_ref[s, hd], r_set, stash))
        ko_ref[s, 0:WINDOW - seq, :] = cachek_ref[s, seq:WINDOW, :]
        vo_ref[s, 0:WINDOW - seq, :] = cachev_ref[s, seq:WINDOW, :]
        ko_ref[s, WINDOW - seq:WINDOW, :] = k_new
        vo_ref[s, WINDOW - seq:WINDOW, :] = v_new

    xo_ref[...] = _dot(mix_ref[...], wout_ref[...])
    for s in range(nseq):
        rws = slice(s * seq, (s + 1) * seq)
        gate_row = mod_ref[s][:, 2 * D_MODEL:3 * D_MODEL]
        xo_ref[rws, :] = x_ref[rws, :] + gate_row * xo_ref[rws, :]
    if cfg.last:
        _final_norm_rows(xo_ref, 0, nseq * seq, fg_ref[...])


def _sample_layer(layer, last, x2d, mod, g, w_in, w_out, tabs, dec, sink, final_g,
                  state, cache_k, cache_v, acc, batch, seq, mod_row0):
    nseq = SAMPLE_SEQS
    rows = nseq * seq
    cfg = _SampleCfg(nseq=nseq, seq=seq, last=last, layer=layer)
    cq, sq, ck, sk = tabs
    decay, qdec, kdec, gl = dec
    mod_blk0 = mod_row0 // nseq
    in_specs = [
        pl.BlockSpec((rows, D_MODEL), lambda i: (i, 0)),
        pl.BlockSpec((None, nseq, 1, 3 * D_MODEL), lambda i: (layer, mod_blk0 + i, 0, 0)),
        _layer_spec((1, D_MODEL), layer),
        _layer_spec((D_MODEL, IN_WIDTH), layer),
        _layer_spec((MIX_WIDTH, D_MODEL), layer),
        _const_spec(cq.shape), _const_spec(sq.shape), _const_spec(ck.shape), _const_spec(sk.shape),
        _const_spec(decay.shape), _const_spec(qdec.shape), _const_spec(kdec.shape),
        _smem_spec(), _smem_spec(),
        _const_spec((1, D_MODEL)),
        pl.BlockSpec((None, nseq, RET_HEADS, RET_DK, RET_DV), lambda i: (layer, i, 0, 0, 0)),
        pl.BlockSpec((None, nseq, WINDOW, KV_WIDTH), lambda i: (layer, i, 0, 0)),
        pl.BlockSpec((None, nseq, WINDOW, KV_WIDTH), lambda i: (layer, i, 0, 0)),
    ]
    body, acc_args, acc_specs, aliases, acc_shape = _state_alias(_sample_kernel, len(in_specs), acc)
    out_specs = [
        pl.BlockSpec((rows, D_MODEL), lambda i: (i, 0)),
        pl.BlockSpec((None, nseq, RET_HEADS, RET_DK, RET_DV), lambda i: (layer, i, 0, 0, 0)),
        pl.BlockSpec((None, nseq, WINDOW, KV_WIDTH), lambda i: (layer, i, 0, 0)),
        pl.BlockSpec((None, nseq, WINDOW, KV_WIDTH), lambda i: (layer, i, 0, 0)),
    ]
    out_shape = [jax.ShapeDtypeStruct((batch * seq, D_MODEL), F32)] + acc_shape
    scratch = [
        pltpu.VMEM((rows, IN_WIDTH), F32),
        pltpu.VMEM((rows, D_MODEL), BF16),
        pltpu.VMEM((rows, MIX_WIDTH), BF16),
        *_kv_scratch(nseq * SAMPLE_KEYS),
    ]
    return pl.pallas_call(
        functools.partial(body, cfg=cfg),
        grid=(batch // nseq,),
        in_specs=in_specs + acc_specs, out_specs=out_specs, out_shape=out_shape,
        input_output_aliases=aliases,
        scratch_shapes=scratch,
        compiler_params=pltpu.CompilerParams(
            dimension_semantics=("arbitrary",),
            vmem_limit_bytes=VMEM_LIMIT_BYTES),
        name=f"sample_layer{layer}",
    )(x2d, mod, g, w_in, w_out, cq, sq, ck, sk, decay, qdec, kdec, gl, sink, final_g,
      state, cache_k, cache_v, *acc_args)


def _rope_tables(start, n):
    d = RET_DK
    inv = 1.0 / (ROPE_BASE ** (jnp.arange(0, d, 2, dtype=F32) / d))
    ang_a = jnp.arange(start, start + n, CHUNK).astype(F32)[:, None] * inv[None, :]
    ang_b = jnp.arange(CHUNK).astype(F32)[:, None] * inv[None, :]
    ca, sa, cb, sb = lax.optimization_barrier((jnp.cos(ang_a), jnp.sin(ang_a), jnp.cos(ang_b), jnp.sin(ang_b)))
    cos = (ca[:, None, :] * cb[None] - sa[:, None, :] * sb[None]).reshape(n, d // 2)
    sin = (sa[:, None, :] * cb[None] + ca[:, None, :] * sb[None]).reshape(n, d // 2)
    cos2 = jnp.concatenate([cos, cos], axis=-1)
    sin2 = jnp.concatenate([-sin, sin], axis=-1)
    kscale = RET_DK ** -0.5
    return cos2, sin2, cos2 * kscale, sin2 * kscale


def _decay_tables(n):
    lg = jnp.log(1.0 - 2.0 ** (-5.0 - jnp.arange(RET_HEADS, dtype=F32)))
    idx = jnp.arange(n, dtype=F32)
    diff = idx[:, None] - idx[None, :]
    decay = jnp.where(diff[None] >= 0, jnp.exp(jnp.maximum(diff, 0.0)[None] * lg[:, None, None]), 0.0)
    qdec = jnp.exp((idx + 1.0)[None, :] * lg[:, None])
    kdec = jnp.exp((n - 1.0 - idx)[None, :] * lg[:, None])
    gl = jnp.exp(n * lg)
    bcast = lambda a: jnp.broadcast_to(a[:, :, None], (RET_HEADS, n, LANES))
    decay2 = jnp.concatenate([decay[0::2], decay[1::2]], axis=-1)
    return decay2, bcast(qdec), bcast(kdec), gl


def kernel(x_prompt, x_sample, c_prompt, c_sample, state_ret, cache_k, cache_v,
           norm_g, w_ada, b_ada, w_in, sink, w_out, final_g):
    batch, seq, _ = x_prompt.shape
    dbatch, dseq, _ = x_sample.shape

    c_all = jnp.concatenate([c_prompt, c_sample], axis=0)
    mod = _adaln(c_all, w_ada, b_ada).reshape(DEPTH, batch + dbatch, 1, 3 * D_MODEL)
    w_in_b = w_in.astype(BF16)
    w_out_b = w_out.astype(BF16)

    tabs_p = _rope_tables(0, seq)
    tabs_s = _rope_tables(PAST_LEN, dseq)
    dec_p = _decay_tables(PROMPT_BLOCK)
    dec_s = _decay_tables(dseq)
    ck4 = cache_k.reshape(DEPTH, dbatch, WINDOW, KV_WIDTH)
    cv4 = cache_v.reshape(DEPTH, dbatch, WINDOW, KV_WIDTH)
    fg = final_g.reshape(1, D_MODEL)
    g_all = norm_g.reshape(DEPTH, 1, D_MODEL)

    xp = x_prompt.reshape(batch * seq, D_MODEL)
    xs = x_sample.reshape(dbatch * dseq, D_MODEL)
    def state_acc(nb):
        return (jax.ShapeDtypeStruct((DEPTH, nb, RET_HEADS, RET_DK, RET_DV), F32),
                jax.ShapeDtypeStruct((DEPTH, nb, WINDOW, KV_WIDTH), F32),
                jax.ShapeDtypeStruct((DEPTH, nb, WINDOW, KV_WIDTH), F32))

    acc_p, acc_s = state_acc(batch), state_acc(dbatch)
    for l in range(DEPTH):
        last = l == DEPTH - 1
        xp, *acc_p = _prompt_layer(l, last, xp, mod, g_all, w_in_b, w_out_b, tabs_p, dec_p,
                                   sink, fg, acc_p, batch, seq)
        xs, *acc_s = _sample_layer(l, last, xs, mod, g_all, w_in_b, w_out_b, tabs_s, dec_s,
                                   sink, fg, state_ret, ck4, cv4, acc_s, dbatch, dseq, mod_row0=batch)

    kv_shape_p = (DEPTH, batch, WINDOW, ATT_KV_HEADS, ATT_HEAD_DIM)
    kv_shape_s = (DEPTH, dbatch, WINDOW, ATT_KV_HEADS, ATT_HEAD_DIM)
    return (xp.reshape(batch, seq, D_MODEL), xs.reshape(dbatch, dseq, D_MODEL),
            acc_p[0], acc_p[1].reshape(kv_shape_p), acc_p[2].reshape(kv_shape_p),
            acc_s[0], acc_s[1].reshape(kv_shape_s), acc_s[2].reshape(kv_shape_s))
```

```python
import functools
from typing import NamedTuple

import jax
import jax.numpy as jnp
from jax import lax
from jax.experimental import pallas as pl
from jax.experimental.pallas import tpu as pltpu

D_MODEL = 1024
DEPTH = 4
CHUNK = 64
PAST_LEN = 4096
RET_HEADS = 4
RET_DK = 128
RET_DV = 128
RET_WIDTH = RET_HEADS * RET_DV
ATT_HEAD_DIM = 64
ATT_Q_HEADS = 8
ATT_KV_HEADS = 2
ATT_WIDTH = ATT_Q_HEADS * ATT_HEAD_DIM
KV_WIDTH = ATT_KV_HEADS * ATT_HEAD_DIM
WINDOW = 128
MIX_WIDTH = RET_WIDTH + ATT_WIDTH
ROPE_BASE = 10000.0
NORM_EPS = 1e-6
GN_EPS = 1e-5

OFF_RQ = 0
OFF_RK = OFF_RQ + RET_HEADS * RET_DK
OFF_RV = OFF_RK + RET_HEADS * RET_DK
OFF_RG = OFF_RV + RET_WIDTH
OFF_AQ = OFF_RG + RET_WIDTH
OFF_AK = OFF_AQ + ATT_WIDTH
OFF_AV = OFF_AK + KV_WIDTH
OFF_AG = OFF_AV + KV_WIDTH
IN_WIDTH = OFF_AG + ATT_WIDTH

LANES = 128
MASKED = -1e30
LOG2E = 1.4426950408889634
VMEM_LIMIT_BYTES = 56 * 1024 * 1024
NORM_ROWS = 32
PROJ_COLS = 512

PROMPT_ROWS = 256
PROMPT_BLOCK = 128
PROMPT_STAGES = 4
SAMPLE_SEQS = 4
SAMPLE_KEYS = 2 * LANES

BF16 = jnp.bfloat16
F32 = jnp.float32


def _silu(g):
    return g * (1.0 / (1.0 + jnp.exp(-g)))


def _dot(a, b):
    return jnp.dot(a, b, preferred_element_type=F32)


def _dot_nt(a, b):
    return lax.dot_general(a, b, (((1,), (1,)), ((), ())), preferred_element_type=F32)


def _low_lanes():
    lane = lax.broadcasted_iota(jnp.int32, (1, LANES), 1)
    return lane < ATT_HEAD_DIM


def _adaln_kernel(c_ref, w_ref, b_ref, o_ref):
    s = _silu(c_ref[...]).astype(BF16)
    o_ref[0] = _dot(s, w_ref[0].astype(BF16)) + b_ref[0]


def _adaln(c_all, w_ada, b_ada):
    n = c_all.shape[0]
    tn = 3 * D_MODEL
    return pl.pallas_call(
        _adaln_kernel,
        grid=(DEPTH, 3 * D_MODEL // tn),
        in_specs=[
            pl.BlockSpec((n, D_MODEL), lambda l, j: (0, 0)),
            pl.BlockSpec((1, D_MODEL, tn), lambda l, j: (l, 0, j)),
            pl.BlockSpec((1, 1, tn), lambda l, j: (l, 0, j)),
        ],
        out_specs=pl.BlockSpec((1, n, tn), lambda l, j: (l, 0, j)),
        out_shape=jax.ShapeDtypeStruct((DEPTH, n, 3 * D_MODEL), F32),
        compiler_params=pltpu.CompilerParams(
            dimension_semantics=("arbitrary", "arbitrary"),
            vmem_limit_bytes=VMEM_LIMIT_BYTES),
        name="adaln",
    )(c_all, w_ada, b_ada.reshape(DEPTH, 1, 3 * D_MODEL))


def _emit(items):
    for it in items:
        it()


def _interleave(a, b):
    if not a:
        return list(b)
    out, nb = [], 0
    for i, it in enumerate(a):
        out.append(it)
        want = (i + 1) * len(b) // len(a)
        out.extend(b[nb:want])
        nb = want
    return out


def _norm_items(x_ref, h_ref, r0, n, g_ref, mod_ref, s):
    def item(a):
        mod_row = mod_ref[s]
        shift = mod_row[:, 0:D_MODEL]
        scale1 = 1.0 + mod_row[:, D_MODEL:2 * D_MODEL]
        xb = x_ref[a:a + NORM_ROWS, :]
        ms = jnp.mean(xb * xb, axis=-1, keepdims=True)
        y = xb * lax.rsqrt(ms + NORM_EPS) * g_ref[...]
        h_ref[a:a + NORM_ROWS, :] = (y * scale1 + shift).astype(BF16)
    return [functools.partial(item, a) for a in range(r0, r0 + n, NORM_ROWS)]


def _in_proj_items(h_ref, win_ref, z_ref):
    def item(c0):
        c1 = min(c0 + PROJ_COLS, IN_WIDTH)
        z_ref[:, c0:c1] = _dot(h_ref[...], win_ref[:, c0:c1])
    return [functools.partial(item, c0) for c0 in range(0, IN_WIDTH, PROJ_COLS)]


def _fill_kv(kv_bufs, dst0, n, k, v):
    ka_ref, kb_ref, kc_ref, kd_ref, va_ref, vb_ref, vc_ref, vd_ref = kv_bufs
    low = _low_lanes()
    kr = pltpu.roll(k, ATT_HEAD_DIM, 1)
    vr = pltpu.roll(v, ATT_HEAD_DIM, 1)
    d = slice(dst0, dst0 + n)
    ka_ref[d, :] = jnp.where(low, k, 0.0).astype(BF16)
    kb_ref[d, :] = jnp.where(low, 0.0, kr).astype(BF16)
    kc_ref[d, :] = jnp.where(low, kr, 0.0).astype(BF16)
    kd_ref[d, :] = jnp.where(low, 0.0, k).astype(BF16)
    one_lo = jnp.broadcast_to(jnp.where(low, 1.0, 0.0), (n, LANES))
    one_hi = 1.0 - one_lo
    va_ref[d, :] = jnp.concatenate([jnp.where(low, v, 0.0), one_lo], axis=1).astype(BF16)
    vb_ref[d, :] = jnp.concatenate([jnp.where(low, 0.0, vr), one_hi], axis=1).astype(BF16)
    vc_ref[d, :] = jnp.concatenate([jnp.where(low, vr, 0.0), one_lo], axis=1).astype(BF16)
    vd_ref[d, :] = jnp.concatenate([jnp.where(low, 0.0, v), one_hi], axis=1).astype(BF16)


class _MixRefs(NamedTuple):
    z: object
    mix: object
    tabs: object
    dec: tuple
    sink: object
    kv_bufs: tuple


class _ValueStash:
    def __init__(self):
        self.d = {}

    def put(self, key, *vals):
        self.d[key] = vals

    def get(self, key):
        return self.d[key]


class _RefStash:
    def __init__(self, refs, j, blk):
        self.refs, self.j, self.blk = refs, j, blk
        self.rws = slice(j * blk, (j + 1) * blk)

    def _slots(self, key):
        lhs_ref, kdt_ref, v_ref, gate_ref, p_ref, e_ref = self.refs
        kind, i = key
        rws, j, blk = self.rws, self.j, self.blk
        if kind == "ret":
            w = lhs_ref.shape[1] // 2
            r0 = (2 * j + i) * RET_DK
            return [(lhs_ref, rws, slice(i * w, (i + 1) * w)), (kdt_ref, slice(r0, r0 + RET_DK), slice(None)),
                    (v_ref, rws, slice(2 * i * RET_DV, 2 * (i + 1) * RET_DV))]
        if kind == "gate":
            return [(gate_ref, rws, slice(None))]
        r0 = (2 * j + i) * 2 * blk
        return [(p_ref, slice(r0, r0 + 2 * blk), slice(None)),
                (e_ref, rws, slice(2 * i * LANES, (2 * i + 1) * LANES)),
                (e_ref, rws, slice((2 * i + 1) * LANES, (2 * i + 2) * LANES))]

    def put(self, key, *vals):
        for (ref, r, c), v in zip(self._slots(key), vals):
            ref[r, c] = v

    def get(self, key):
        return tuple(ref[r, c] for ref, r, c in self._slots(key))


def _block_diag(a, b):
    za = jnp.zeros(a.shape, a.dtype)
    return jnp.concatenate([jnp.concatenate([a, za], axis=1), jnp.concatenate([za, b], axis=1)], axis=0)


def _front_items(m: _MixRefs, layer, r0, blk, trow0, win0, lk, valid_fn, stash):
    z_ref = m.z
    tab_ref = m.tabs
    decay_ref, qdec_ref, kdec_ref, _ = m.dec
    ka_ref, kb_ref, kc_ref, kd_ref = m.kv_bufs[:4]
    rws = slice(r0, r0 + blk)
    trs = slice(trow0, trow0 + blk)
    win = slice(win0, win0 + lk)

    def ret_front(hp):
        cq, sq, ck, sk = (tab_ref[trs, i * LANES:(i + 1) * LANES] for i in range(4))
        qs, ks, vs, qds, kds = [], [], [], [], []
        for hd in (2 * hp, 2 * hp + 1):
            c = hd * RET_DK
            q = z_ref[rws, OFF_RQ + c:OFF_RQ + c + RET_DK]
            k = z_ref[rws, OFF_RK + c:OFF_RK + c + RET_DK]
            q = q * cq + pltpu.roll(q, RET_DK // 2, 1) * sq
            k = k * ck + pltpu.roll(k, RET_DK // 2, 1) * sk
            qs.append(q.astype(BF16))
            ks.append(k.astype(BF16))
            vs.append(z_ref[rws, OFF_RV + c:OFF_RV + c + RET_DV].astype(BF16))
            qds.append((q * qdec_ref[hd]).astype(BF16))
            kds.append(k * kdec_ref[hd])
        s2 = _dot_nt(jnp.concatenate(qs, axis=1), _block_diag(*ks)) * decay_ref[hp]
        lhs = jnp.concatenate([s2.astype(BF16)] + qds, axis=1)
        kd_t = jnp.concatenate(kds, axis=0).T.astype(BF16)
        stash.put(("ret", hp), lhs, kd_t, jnp.concatenate(vs, axis=1))

    def gates():
        stash.put(("gate", 0), jnp.concatenate([_silu(z_ref[rws, OFF_RG:OFF_RG + RET_WIDTH]),
                                                _silu(z_ref[rws, OFF_AG:OFF_AG + ATT_WIDTH])], axis=1))

    def att_front(kv):
        low = _low_lanes()
        valid = None if valid_fn is None else valid_fn()
        slabs = (2 * kv, 2 * kv + 1)
        k_lo, k_hi = (ka_ref, kb_ref) if kv == 0 else (kc_ref, kd_ref)
        qf = jnp.concatenate([z_ref[rws, OFF_AQ + sl * LANES:OFF_AQ + (sl + 1) * LANES] for sl in slabs], axis=0)
        qf = (qf * (ATT_HEAD_DIM ** -0.5 * LOG2E)).astype(BF16)
        keys = jnp.concatenate([k_lo[win, :], k_hi[win, :]], axis=0)
        s = _dot_nt(qf, keys)
        ps, es = [], []
        for i, sl in enumerate(slabs):
            row_p, row_e = [], []
            for h in range(2):
                sh = s[i * blk:(i + 1) * blk, h * lk:(h + 1) * lk]
                if valid is not None:
                    sh = jnp.where(valid, sh, MASKED)
                sink = m.sink[layer, 2 * sl + h] * LOG2E
                mh = jnp.maximum(jnp.max(sh, axis=-1, keepdims=True), sink)
                row_p.append(jnp.exp2(sh - mh).astype(BF16))
                row_e.append(jnp.exp2(sink - mh))
            ps.append(jnp.concatenate(row_p, axis=1))
            es.append(jnp.broadcast_to(jnp.where(low, row_e[0], row_e[1]), (blk, LANES)))
        stash.put(("att", kv), jnp.concatenate(ps, axis=0), *es)

    return ([functools.partial(ret_front, hp) for hp in range(RET_HEADS // 2)] + [gates]
            + [functools.partial(att_front, kv) for kv in range(ATT_KV_HEADS)])


def _back_items(m: _MixRefs, r0, blk, win0, lk, r_get, r_set, stash):
    mix_ref = m.mix
    gl_ref = m.dec[3]
    va_ref, vb_ref, vc_ref, vd_ref = m.kv_bufs[4:]
    rws = slice(r0, r0 + blk)
    win = slice(win0, win0 + lk)

    def ret_back(hp):
        heads = (2 * hp, 2 * hp + 1)
        lhs, kd_t, v2 = stash.get(("ret", hp))
        gate = stash.get(("gate", 0))[0]
        r_old = [r_get(hd) for hd in heads]
        v_bd = _block_diag(v2[:, :RET_DV], v2[:, RET_DV:])
        r_bd = _block_diag(*[r.astype(BF16) for r in r_old])
        o2 = _dot(lhs, jnp.concatenate([v_bd, r_bd], axis=0))
        r_inc = _dot(kd_t, v_bd)
        for i, hd in enumerate(heads):
            c = hd * RET_DK
            r_set(hd, gl_ref[hd] * r_old[i] + r_inc[:, i * RET_DV:(i + 1) * RET_DV])
            o = o2[:, i * RET_DV:(i + 1) * RET_DV]
            mu = jnp.mean(o, axis=-1, keepdims=True)
            oc = o - mu
            var = jnp.mean(oc * oc, axis=-1, keepdims=True)
            on = oc * lax.rsqrt(var + GN_EPS)
            mix_ref[rws, c:c + RET_DV] = (on * gate[:, c:c + RET_DV]).astype(BF16)

    def att_back(kv):
        v_lo, v_hi = (va_ref, vb_ref) if kv == 0 else (vc_ref, vd_ref)
        p, *es = stash.get(("att", kv))
        gate = stash.get(("gate", 0))[0]
        vals = jnp.concatenate([v_lo[win, :], v_hi[win, :]], axis=0)
        acc = _dot(p, vals)
        for i, sl in enumerate((2 * kv, 2 * kv + 1)):
            c = RET_WIDTH + sl * LANES
            a = acc[i * blk:(i + 1) * blk, :]
            den = a[:, LANES:] + es[i]
            mix_ref[rws, c:c + LANES] = (a[:, :LANES] / den * gate[:, c:c + LANES]).astype(BF16)

    return ([functools.partial(ret_back, hp) for hp in range(RET_HEADS // 2)]
            + [functools.partial(att_back, kv) for kv in range(ATT_KV_HEADS)])


def _final_norm_rows(xo_ref, r0, n, fg_row):
    for a in range(r0, r0 + n, NORM_ROWS):
        xn = xo_ref[a:a + NORM_ROWS, :]
        ms = jnp.mean(xn * xn, axis=-1, keepdims=True)
        xo_ref[a:a + NORM_ROWS, :] = xn * lax.rsqrt(ms + NORM_EPS) * fg_row


class _PromptCfg(NamedTuple):
    rows: int
    nt: int
    ntiles: int
    last: bool
    layer: int


N_STASH = 6
N_KV = 8


def _prompt_kernel(xn_ref, xc_ref, modn_ref, modc_ref, g_ref, win_ref, wout_ref,
                   tab_ref, decay_ref, qdec_ref, kdec_ref, gl_ref, sink_ref, fg_ref,
                   _ro_prev, _ko_prev, _vo_prev,
                   xo_ref, ro_ref, ko_ref, vo_ref,
                   za_ref, zb_ref, h_ref, mixa_ref, mixb_ref, r_scr, *bufs,
                   cfg: _PromptCfg):
    rows, nt, ntiles = cfg.rows, cfg.nt, cfg.ntiles
    blk, lk = PROMPT_BLOCK, WINDOW + PROMPT_BLOCK
    stash_refs = (bufs[0:N_STASH], bufs[N_STASH:2 * N_STASH])
    kv_refs = (bufs[2 * N_STASH:2 * N_STASH + N_KV], bufs[2 * N_STASH + N_KV:2 * N_STASH + 2 * N_KV])
    z_refs = (za_ref, zb_ref)
    mix_refs = (mixa_ref, mixb_ref)
    g = pl.program_id(0)
    t = jnp.clip(g - 1, 0, ntiles - 1) % nt
    t3 = jnp.clip(g - 2, 0, ntiles - 1) % nt

    @pl.when(g == 0)
    def _():
        zb_ref[...] = jnp.zeros(zb_ref.shape, F32)
        mixb_ref[...] = jnp.zeros(mixb_ref.shape, BF16)
        for b in stash_refs[0] + kv_refs[0]:
            b[...] = jnp.zeros(b.shape, b.dtype)
        stash_refs[0][-1][...] = jnp.ones(stash_refs[0][-1].shape, F32)

    @pl.when(t3 == 0)
    def _():
        r_scr[...] = jnp.zeros(r_scr.shape, F32)

    def r_set(hd, val):
        r_scr[hd] = val

    def visible(j):
        col_i = lax.broadcasted_iota(jnp.int32, (blk, lk), 1)
        ok = None
        if blk == 2 * CHUNK:
            row_i = lax.broadcasted_iota(jnp.int32, (blk, lk), 0)
            first_key = jnp.where(row_i < CHUNK, 0, CHUNK)
            ok = (col_i >= first_key) & (col_i < first_key + WINDOW + CHUNK)
        if j * blk < WINDOW:
            started = col_i >= jnp.where(t == 0, WINDOW - j * blk, 0)
            ok = started if ok is None else ok & started
        return ok

    def step(p):
        a, b = p, 1 - p
        tabs = tab_ref
        dec = (decay_ref, qdec_ref, kdec_ref, gl_ref)

        norm = _norm_items(xn_ref, h_ref, 0, rows, g_ref, modn_ref, 0)
        proj = _in_proj_items(h_ref, win_ref, z_refs[a])

        def out_item(c0):
            cs = slice(c0, c0 + PROJ_COLS)
            gate_row = modc_ref[0][:, 2 * D_MODEL + c0:2 * D_MODEL + c0 + PROJ_COLS]
            xo_ref[:, cs] = xc_ref[:, cs] + gate_row * _dot(mix_refs[b][...], wout_ref[:, cs])
        outp = [functools.partial(out_item, c0) for c0 in range(0, D_MODEL, PROJ_COLS)]

        zr_ref = z_refs[b]
        for dst, src in zip(kv_refs[b], kv_refs[a]):
            dst[0:WINDOW, :] = src[rows:rows + WINDOW, :]
        for r0 in range(0, rows, CHUNK):
            _fill_kv(kv_refs[b], WINDOW + r0, CHUNK, zr_ref[r0:r0 + CHUNK, OFF_AK:OFF_AK + KV_WIDTH],
                     zr_ref[r0:r0 + CHUNK, OFF_AV:OFF_AV + KV_WIDTH])
        front = []
        for j in range(rows // blk):
            front += _front_items(_MixRefs(zr_ref, None, tabs, dec, sink_ref, kv_refs[b]), cfg.layer,
                                  j * blk, blk, j * blk, j * blk, lk, functools.partial(visible, j),
                                  _RefStash(stash_refs[b], j, blk))

        back = []
        for j in range(rows // blk):
            back += _back_items(_MixRefs(None, mix_refs[a], tabs, dec, sink_ref, kv_refs[a]),
                                j * blk, blk, j * blk, lk, lambda hd: r_scr[hd], r_set,
                                _RefStash(stash_refs[a], j, blk))

        _emit(_interleave(norm, outp))
        _emit(_interleave(_interleave(front, back), proj))
        if cfg.last:
            _final_norm_rows(xo_ref, 0, rows, fg_ref[...])

        @pl.when(t == nt - 1)
        def _():
            ko_ref[...] = zr_ref[rows - WINDOW:rows, OFF_AK:OFF_AK + KV_WIDTH]
            vo_ref[...] = zr_ref[rows - WINDOW:rows, OFF_AV:OFF_AV + KV_WIDTH]

    for parity in (0, 1):
        @pl.when(g % 2 == parity)
        def _():
            step(parity)

    @pl.when((t3 == nt - 1) & (g <= ntiles + 1))
    def _():
        ro_ref[...] = r_scr[...]


def _stash_scratch(rows, blk, lk):
    nblk = rows // blk
    return [
        pltpu.VMEM((rows, 2 * (2 * blk + 2 * RET_DK)), BF16),
        pltpu.VMEM((nblk * 2 * RET_DK, 2 * blk), BF16),
        pltpu.VMEM((rows, RET_WIDTH), BF16),
        pltpu.VMEM((rows, MIX_WIDTH), F32),
        pltpu.VMEM((nblk * 2 * 2 * blk, 2 * lk), BF16),
        pltpu.VMEM((rows, ATT_WIDTH), F32),
    ]


def _kv_scratch(krows):
    return ([pltpu.VMEM((krows, LANES), BF16) for _ in range(4)]
            + [pltpu.VMEM((krows, 2 * LANES), BF16) for _ in range(4)])


def _const_spec(shape):
    nd = len(shape)
    return pl.BlockSpec(shape, lambda g: (0,) * nd)


def _layer_spec(shape, layer):
    nd = len(shape)
    return pl.BlockSpec((None, *shape), lambda g: (layer,) + (0,) * nd)


def _smem_spec():
    return pl.BlockSpec(memory_space=pltpu.SMEM)


N_STATE = 3


def _state_alias(kernel, n_in, acc):
    out_shape = [jax.ShapeDtypeStruct(a.shape, a.dtype) for a in acc]
    if all(isinstance(a, jax.ShapeDtypeStruct) for a in acc):
        def first(*refs, **kw):
            return kernel(*refs[:n_in], *([None] * N_STATE), *refs[n_in:], **kw)
        return first, [], [], {}, out_shape
    in_specs = [pl.BlockSpec(memory_space=pl.ANY)] * N_STATE
    aliases = {n_in + i: 1 + i for i in range(N_STATE)}
    return kernel, list(acc), in_specs, aliases, out_shape


def _prompt_layer(layer, last, x2d, mod, g, w_in, w_out, tabs, dec, sink, final_g, acc, batch, seq):
    rows = PROMPT_ROWS
    blk, lk = PROMPT_BLOCK, WINDOW + PROMPT_BLOCK
    nt = seq // rows
    ntiles = batch * nt
    cfg = _PromptCfg(rows=rows, nt=nt, ntiles=ntiles, last=last, layer=layer)
    decay, qdec, kdec, gl = dec

    nxt = lambda g: jnp.minimum(g, ntiles - 1)
    cur = lambda g: jnp.clip(g - 1, 0, ntiles - 1)
    bak = lambda g: jnp.clip(g - 2, 0, ntiles - 1)
    fin = lambda g: jnp.maximum(g - 3, 0)
    tab_spec = pl.BlockSpec((rows, tabs.shape[1]), lambda g: (cur(g) % nt, 0))
    in_specs = [
        pl.BlockSpec((rows, D_MODEL), lambda g: (nxt(g), 0)),
        pl.BlockSpec((rows, D_MODEL), lambda g: (fin(g), 0)),
        pl.BlockSpec((None, 1, 1, 3 * D_MODEL), lambda g: (layer, nxt(g) // nt, 0, 0)),
        pl.BlockSpec((None, 1, 1, 3 * D_MODEL), lambda g: (layer, fin(g) // nt, 0, 0)),
        _layer_spec((1, D_MODEL), layer),
        _layer_spec((D_MODEL, IN_WIDTH), layer),
        _layer_spec((MIX_WIDTH, D_MODEL), layer),
        tab_spec,
        _const_spec(decay.shape), _const_spec(qdec.shape), _const_spec(kdec.shape),
        _smem_spec(), _smem_spec(),
        _const_spec((1, D_MODEL)),
    ]
    body, acc_args, acc_specs, aliases, acc_shape = _state_alias(_prompt_kernel, len(in_specs), acc)
    out_specs = [
        pl.BlockSpec((rows, D_MODEL), lambda g: (fin(g), 0)),
        pl.BlockSpec((None, None, RET_HEADS, RET_DK, RET_DV), lambda g: (layer, bak(g) // nt, 0, 0, 0)),
        pl.BlockSpec((None, None, WINDOW, KV_WIDTH), lambda g: (layer, cur(g) // nt, 0, 0)),
        pl.BlockSpec((None, None, WINDOW, KV_WIDTH), lambda g: (layer, cur(g) // nt, 0, 0)),
    ]
    out_shape = [jax.ShapeDtypeStruct((batch * seq, D_MODEL), F32)] + acc_shape
    scratch = [
        pltpu.VMEM((rows, IN_WIDTH), F32),
        pltpu.VMEM((rows, IN_WIDTH), F32),
        pltpu.VMEM((rows, D_MODEL), BF16),
        pltpu.VMEM((rows, MIX_WIDTH), BF16),
        pltpu.VMEM((rows, MIX_WIDTH), BF16),
        pltpu.VMEM((RET_HEADS, RET_DK, RET_DV), F32),
        *_stash_scratch(rows, blk, lk), *_stash_scratch(rows, blk, lk),
        *_kv_scratch(WINDOW + rows), *_kv_scratch(WINDOW + rows),
    ]
    return pl.pallas_call(
        functools.partial(body, cfg=cfg),
        grid=(ntiles + PROMPT_STAGES - 1,),
        in_specs=in_specs + acc_specs, out_specs=out_specs, out_shape=out_shape,
        input_output_aliases=aliases,
        scratch_shapes=scratch,
        compiler_params=pltpu.CompilerParams(
            dimension_semantics=("arbitrary",),
            vmem_limit_bytes=VMEM_LIMIT_BYTES),
        name=f"prompt_layer{layer}",
    )(x2d, x2d, mod, mod, g, w_in, w_out, tabs, decay, qdec, kdec, gl, sink, final_g, *acc_args)


class _SampleCfg(NamedTuple):
    nseq: int
    seq: int
    last: bool
    layer: int


def _sample_kernel(x_ref, mod_ref, g_ref, win_ref, wout_ref,
                   tab_ref, decay_ref, qdec_ref, kdec_ref, gl_ref, sink_ref, fg_ref,
                   st_ref, cachek_ref, cachev_ref,
                   _ro_prev, _ko_prev, _vo_prev,
                   xo_ref, ro_ref, ko_ref, vo_ref,
                   z_ref, h_ref, mix_ref, *kv_bufs,
                   cfg: _SampleCfg):
    nseq, seq = cfg.nseq, cfg.seq
    lk = SAMPLE_KEYS
    pad = lk - WINDOW - seq

    for s in range(nseq):
        _emit(_norm_items(x_ref, h_ref, s * seq, seq, g_ref, mod_ref, s))
    _emit(_in_proj_items(h_ref, win_ref, z_ref))

    def visible():
        return lax.broadcasted_iota(jnp.int32, (seq, lk), 1) >= pad

    m = _MixRefs(z_ref, mix_ref, tab_ref,
                 (decay_ref, qdec_ref, kdec_ref, gl_ref), sink_ref, kv_bufs)
    for s in range(nseq):
        rws = slice(s * seq, (s + 1) * seq)
        k_new = z_ref[rws, OFF_AK:OFF_AK + KV_WIDTH]
        v_new = z_ref[rws, OFF_AV:OFF_AV + KV_WIDTH]
        for b in kv_bufs:
            b[s * lk:s * lk + pad, :] = jnp.zeros((pad, b.shape[1]), BF16)
        _fill_kv(kv_bufs, s * lk + pad, WINDOW, cachek_ref[s], cachev_ref[s])
        _fill_kv(kv_bufs, s * lk + pad + WINDOW, seq, k_new, v_new)

        def r_set(hd, val, s=s):
            ro_ref[s, hd] = val

        stash = _ValueStash()
        _emit(_front_items(m, cfg.layer, s * seq, seq, 0, s * lk, lk, visible, stash))
        _emit(_back_items(m, s * seq, seq, s * lk, lk, lambda hd, s=s: st_ref[s, hd], r_set, stash))
        ko_ref[s, 0:WINDOW - seq, :] = cachek_ref[s, seq:WINDOW, :]
        vo_ref[s, 0:WINDOW - seq, :] = cachev_ref[s, seq:WINDOW, :]
        ko_ref[s, WINDOW - seq:WINDOW, :] = k_new
        vo_ref[s, WINDOW - seq:WINDOW, :] = v_new

    xo_ref[...] = _dot(mix_ref[...], wout_ref[...])
    for s in range(nseq):
        rws = slice(s * seq, (s + 1) * seq)
        gate_row = mod_ref[s][:, 2 * D_MODEL:3 * D_MODEL]
        xo_ref[rws, :] = x_ref[rws, :] + gate_row * xo_ref[rws, :]
    if cfg.last:
        _final_norm_rows(xo_ref, 0, nseq * seq, fg_ref[...])


def _sample_layer(layer, last, x2d, mod, g, w_in, w_out, tabs, dec, sink, final_g,
                  state, cache_k, cache_v, acc, batch, seq, mod_row0):
    nseq = SAMPLE_SEQS
    rows = nseq * seq
    cfg = _SampleCfg(nseq=nseq, seq=seq, last=last, layer=layer)
    decay, qdec, kdec, gl = dec
    mod_blk0 = mod_row0 // nseq
    in_specs = [
        pl.BlockSpec((rows, D_MODEL), lambda i: (i, 0)),
        pl.BlockSpec((None, nseq, 1, 3 * D_MODEL), lambda i: (layer, mod_blk0 + i, 0, 0)),
        _layer_spec((1, D_MODEL), layer),
        _layer_spec((D_MODEL, IN_WIDTH), layer),
        _layer_spec((MIX_WIDTH, D_MODEL), layer),
        _const_spec(tabs.shape),
        _const_spec(decay.shape), _const_spec(qdec.shape), _const_spec(kdec.shape),
        _smem_spec(), _smem_spec(),
        _const_spec((1, D_MODEL)),
        pl.BlockSpec((None, nseq, RET_HEADS, RET_DK, RET_DV), lambda i: (layer, i, 0, 0, 0)),
        pl.BlockSpec((None, nseq, WINDOW, KV_WIDTH), lambda i: (layer, i, 0, 0)),
        pl.BlockSpec((None, nseq, WINDOW, KV_WIDTH), lambda i: (layer, i, 0, 0)),
    ]
    body, acc_args, acc_specs, aliases, acc_shape = _state_alias(_sample_kernel, len(in_specs), acc)
    out_specs = [
        pl.BlockSpec((rows, D_MODEL), lambda i: (i, 0)),
        pl.BlockSpec((None, nseq, RET_HEADS, RET_DK, RET_DV), lambda i: (layer, i, 0, 0, 0)),
        pl.BlockSpec((None, nseq, WINDOW, KV_WIDTH), lambda i: (layer, i, 0, 0)),
        pl.BlockSpec((None, nseq, WINDOW, KV_WIDTH), lambda i: (layer, i, 0, 0)),
    ]
    out_shape = [jax.ShapeDtypeStruct((batch * seq, D_MODEL), F32)] + acc_shape
    scratch = [
        pltpu.VMEM((rows, IN_WIDTH), F32),
        pltpu.VMEM((rows, D_MODEL), BF16),
        pltpu.VMEM((rows, MIX_WIDTH), BF16),
        *_kv_scratch(nseq * SAMPLE_KEYS),
    ]
    return pl.pallas_call(
        functools.partial(body, cfg=cfg),
        grid=(batch // nseq,),
        in_specs=in_specs + acc_specs, out_specs=out_specs, out_shape=out_shape,
        input_output_aliases=aliases,
        scratch_shapes=scratch,
        compiler_params=pltpu.CompilerParams(
            dimension_semantics=("arbitrary",),
            vmem_limit_bytes=VMEM_LIMIT_BYTES),
        name=f"sample_layer{layer}",
    )(x2d, mod, g, w_in, w_out, tabs, decay, qdec, kdec, gl, sink, final_g,
      state, cache_k, cache_v, *acc_args)


def _rope_tables(start, n):
    d = RET_DK
    inv = 1.0 / (ROPE_BASE ** (jnp.arange(0, d, 2, dtype=F32) / d))
    ang_a = jnp.arange(start, start + n, CHUNK).astype(F32)[:, None] * inv[None, :]
    ang_b = jnp.arange(CHUNK).astype(F32)[:, None] * inv[None, :]
    ca, sa, cb, sb = lax.optimization_barrier((jnp.cos(ang_a), jnp.sin(ang_a), jnp.cos(ang_b), jnp.sin(ang_b)))
    cos = (ca[:, None, :] * cb[None] - sa[:, None, :] * sb[None]).reshape(n, d // 2)
    sin = (sa[:, None, :] * cb[None] + ca[:, None, :] * sb[None]).reshape(n, d // 2)
    cos2 = jnp.concatenate([cos, cos], axis=-1)
    sin2 = jnp.concatenate([-sin, sin], axis=-1)
    kscale = RET_DK ** -0.5
    return jnp.concatenate([cos2, sin2, cos2 * kscale, sin2 * kscale], axis=-1)


def _decay_tables(n):
    lg = jnp.log(1.0 - 2.0 ** (-5.0 - jnp.arange(RET_HEADS, dtype=F32)))
    idx = jnp.arange(n, dtype=F32)
    diff = idx[:, None] - idx[None, :]
    decay = jnp.where(diff[None] >= 0, jnp.exp(jnp.maximum(diff, 0.0)[None] * lg[:, None, None]), 0.0)
    qdec = jnp.exp((idx + 1.0)[None, :] * lg[:, None])
    kdec = jnp.exp((n - 1.0 - idx)[None, :] * lg[:, None])
    gl = jnp.exp(n * lg)
    bcast = lambda a: jnp.broadcast_to(a[:, :, None], (RET_HEADS, n, LANES))
    decay2 = jnp.concatenate([decay[0::2], decay[1::2]], axis=-1)
    return decay2, bcast(qdec), bcast(kdec), gl


def kernel(x_prompt, x_sample, c_prompt, c_sample, state_ret, cache_k, cache_v,
           norm_g, w_ada, b_ada, w_in, sink, w_out, final_g):
    batch, seq, _ = x_prompt.shape
    dbatch, dseq, _ = x_sample.shape

    c_all = jnp.concatenate([c_prompt, c_sample], axis=0)
    mod = _adaln(c_all, w_ada, b_ada).reshape(DEPTH, batch + dbatch, 1, 3 * D_MODEL)
    w_in_b = w_in.astype(BF16)
    w_out_b = w_out.astype(BF16)

    tabs_p = _rope_tables(0, seq)
    tabs_s = _rope_tables(PAST_LEN, dseq)
    dec_p = _decay_tables(PROMPT_BLOCK)
    dec_s = _decay_tables(dseq)
    ck4 = cache_k.reshape(DEPTH, dbatch, WINDOW, KV_WIDTH)
    cv4 = cache_v.reshape(DEPTH, dbatch, WINDOW, KV_WIDTH)
    fg = final_g.reshape(1, D_MODEL)
    g_all = norm_g.reshape(DEPTH, 1, D_MODEL)

    xp = x_prompt.reshape(batch * seq, D_MODEL)
    xs = x_sample.reshape(dbatch * dseq, D_MODEL)
    def state_acc(nb):
        return (jax.ShapeDtypeStruct((DEPTH, nb, RET_HEADS, RET_DK, RET_DV), F32),
                jax.ShapeDtypeStruct((DEPTH, nb, WINDOW, KV_WIDTH), F32),
                jax.ShapeDtypeStruct((DEPTH, nb, WINDOW, KV_WIDTH), F32))

    acc_p, acc_s = state_acc(batch), state_acc(dbatch)
    for l in range(DEPTH):
        last = l == DEPTH - 1
        xp, *acc_p = _prompt_layer(l, last, xp, mod, g_all, w_in_b, w_out_b, tabs_p, dec_p,
                                   sink, fg, acc_p, batch, seq)
        xs, *acc_s = _sample_layer(l, last, xs, mod, g_all, w_in_b, w_out_b, tabs_s, dec_s,
                                   sink, fg, state_ret, ck4, cv4, acc_s, dbatch, dseq, mod_row0=batch)

    kv_shape_p = (DEPTH, batch, WINDOW, ATT_KV_HEADS, ATT_HEAD_DIM)
    kv_shape_s = (DEPTH, dbatch, WINDOW, ATT_KV_HEADS, ATT_HEAD_DIM)
    return (xp.reshape(batch, seq, D_MODEL), xs.reshape(dbatch, dseq, D_MODEL),
            acc_p[0], acc_p[1].reshape(kv_shape_p), acc_p[2].reshape(kv_shape_p),
            acc_s[0], acc_s[1].reshape(kv_shape_s), acc_s[2].reshape(kv_shape_s))
```

```python
import functools
from typing import NamedTuple

import jax
import jax.numpy as jnp
from jax import lax
from jax.experimental import pallas as pl
from jax.experimental.pallas import tpu as pltpu

D_MODEL = 1024
DEPTH = 4
CHUNK = 64
PAST_LEN = 4096
RET_HEADS = 4
RET_DK = 128
RET_DV = 128
RET_WIDTH = RET_HEADS * RET_DV
ATT_HEAD_DIM = 64
ATT_Q_HEADS = 8
ATT_KV_HEADS = 2
ATT_WIDTH = ATT_Q_HEADS * ATT_HEAD_DIM
KV_WIDTH = ATT_KV_HEADS * ATT_HEAD_DIM
WINDOW = 128
MIX_WIDTH = RET_WIDTH + ATT_WIDTH
ROPE_BASE = 10000.0
NORM_EPS = 1e-6
GN_EPS = 1e-5

OFF_RQ = 0
OFF_RK = OFF_RQ + RET_HEADS * RET_DK
OFF_RV = OFF_RK + RET_HEADS * RET_DK
OFF_RG = OFF_RV + RET_WIDTH
OFF_AQ = OFF_RG + RET_WIDTH
OFF_AK = OFF_AQ + ATT_WIDTH
OFF_AV = OFF_AK + KV_WIDTH
OFF_AG = OFF_AV + KV_WIDTH
IN_WIDTH = OFF_AG + ATT_WIDTH

LANES = 128
ATT_SLABS = ATT_WIDTH // LANES
MASKED = -1e30
LOG2E = 1.4426950408889634
VMEM_LIMIT_BYTES = 56 * 1024 * 1024
NORM_ROWS = 32
PROJ_COLS = 512

PROMPT_ROWS = 256
PROMPT_BLOCK = 128
PROMPT_STAGES = 4
SAMPLE_SEQS = 4
SAMPLE_KEYS = 2 * LANES

BF16 = jnp.bfloat16
F32 = jnp.float32


def _silu(g):
    return g * (1.0 / (1.0 + jnp.exp(-g)))


def _dot(a, b):
    return jnp.dot(a, b, preferred_element_type=F32)


def _dot_nt(a, b):
    return lax.dot_general(a, b, (((1,), (1,)), ((), ())), preferred_element_type=F32)


def _low_lanes():
    lane = lax.broadcasted_iota(jnp.int32, (1, LANES), 1)
    return lane < ATT_HEAD_DIM


def _adaln_kernel(c_ref, w_ref, b_ref, o_ref):
    s = _silu(c_ref[...]).astype(BF16)
    o_ref[0] = _dot(s, w_ref[0].astype(BF16)) + b_ref[0]


def _adaln(c_all, w_ada, b_ada):
    n = c_all.shape[0]
    tn = D_MODEL
    return pl.pallas_call(
        _adaln_kernel,
        grid=(DEPTH, 3 * D_MODEL // tn),
        in_specs=[
            pl.BlockSpec((n, D_MODEL), lambda l, j: (0, 0)),
            pl.BlockSpec((1, D_MODEL, tn), lambda l, j: (l, 0, j)),
            pl.BlockSpec((1, 1, tn), lambda l, j: (l, 0, j)),
        ],
        out_specs=pl.BlockSpec((1, n, tn), lambda l, j: (l, 0, j)),
        out_shape=jax.ShapeDtypeStruct((DEPTH, n, 3 * D_MODEL), F32),
        compiler_params=pltpu.CompilerParams(
            dimension_semantics=("arbitrary", "arbitrary"),
            vmem_limit_bytes=VMEM_LIMIT_BYTES),
        name="adaln",
    )(c_all, w_ada, b_ada.reshape(DEPTH, 1, 3 * D_MODEL))


def _emit(items):
    for it in items:
        it()


def _interleave(a, b):
    if not a:
        return list(b)
    out, nb = [], 0
    for i, it in enumerate(a):
        out.append(it)
        want = (i + 1) * len(b) // len(a)
        out.extend(b[nb:want])
        nb = want
    return out


def _norm_items(x_ref, h_ref, r0, n, g_ref, mod_ref, s):
    def item(a):
        mod_row = mod_ref[s]
        shift = mod_row[:, 0:D_MODEL]
        gain = g_ref[...] * (1.0 + mod_row[:, D_MODEL:2 * D_MODEL])
        xb = x_ref[a:a + NORM_ROWS, :]
        ms = jnp.mean(xb * xb, axis=-1, keepdims=True)
        h_ref[a:a + NORM_ROWS, :] = (xb * lax.rsqrt(ms + NORM_EPS) * gain + shift).astype(BF16)
    return [functools.partial(item, a) for a in range(r0, r0 + n, NORM_ROWS)]


def _in_proj_items(h_ref, win_ref, z_ref):
    def item(c0):
        c1 = min(c0 + PROJ_COLS, IN_WIDTH)
        z_ref[:, c0:c1] = _dot(h_ref[...], win_ref[:, c0:c1])
    return [functools.partial(item, c0) for c0 in range(0, IN_WIDTH, PROJ_COLS)]


def _fill_kv(kv_bufs, dst0, n, k, v):
    ka_ref, kb_ref, kc_ref, kd_ref, va_ref, vb_ref, vc_ref, vd_ref = kv_bufs
    low = _low_lanes()
    kr = pltpu.roll(k, ATT_HEAD_DIM, 1)
    vr = pltpu.roll(v, ATT_HEAD_DIM, 1)
    d = slice(dst0, dst0 + n)
    ka_ref[d, :] = jnp.where(low, k, 0.0).astype(BF16)
    kb_ref[d, :] = jnp.where(low, 0.0, kr).astype(BF16)
    kc_ref[d, :] = jnp.where(low, kr, 0.0).astype(BF16)
    kd_ref[d, :] = jnp.where(low, 0.0, k).astype(BF16)
    va_ref[d, 0:LANES] = jnp.where(low, v, 0.0).astype(BF16)
    vb_ref[d, 0:LANES] = jnp.where(low, 0.0, vr).astype(BF16)
    vc_ref[d, 0:LANES] = jnp.where(low, vr, 0.0).astype(BF16)
    vd_ref[d, 0:LANES] = jnp.where(low, 0.0, v).astype(BF16)


def _init_kv_ones(kv_bufs):
    low = _low_lanes()
    for i, ref in enumerate(kv_bufs[4:]):
        ones = jnp.where(low, 1.0, 0.0) if i % 2 == 0 else jnp.where(low, 0.0, 1.0)
        ref[:, LANES:] = jnp.broadcast_to(ones, (ref.shape[0], LANES)).astype(BF16)


class _MixRefs(NamedTuple):
    z: object
    mix: object
    tabs: tuple
    dec: tuple
    sink: object
    kv_bufs: tuple


class _ValueStash:
    def __init__(self):
        self.d = {}

    def put(self, key, *vals):
        self.d[key] = vals

    def get(self, key):
        return self.d[key]


class _RefStash:
    def __init__(self, refs, j, blk):
        self.refs, self.j, self.blk = refs, j, blk
        self.rws = slice(j * blk, (j + 1) * blk)

    def _slots(self, key):
        lhs_ref, kdt_ref, v_ref, gate_ref, p_ref, e_ref = self.refs
        kind, i = key
        rws, j, blk = self.rws, self.j, self.blk
        if kind == "ret":
            w = lhs_ref.shape[1] // 2
            r0 = (2 * j + i) * RET_DK
            return [(lhs_ref, rws, slice(i * w, (i + 1) * w)), (kdt_ref, slice(r0, r0 + RET_DK), slice(None)),
                    (v_ref, rws, slice(2 * i * RET_DV, 2 * (i + 1) * RET_DV))]
        if kind == "gate":
            return [(gate_ref, rws, slice(None))]
        r0 = (2 * j + i) * 2 * blk
        return [(p_ref, slice(r0, r0 + 2 * blk), slice(None)),
                (e_ref, rws, slice(2 * i * LANES, (2 * i + 1) * LANES)),
                (e_ref, rws, slice((2 * i + 1) * LANES, (2 * i + 2) * LANES))]

    def put(self, key, *vals):
        for (ref, r, c), v in zip(self._slots(key), vals):
            ref[r, c] = v

    def get(self, key):
        return tuple(ref[r, c] for ref, r, c in self._slots(key))


def _block_diag(a, b):
    za = jnp.zeros(a.shape, a.dtype)
    return jnp.concatenate([jnp.concatenate([a, za], axis=1), jnp.concatenate([za, b], axis=1)], axis=0)


def _front_items(m: _MixRefs, layer, r0, blk, trow0, win0, lk, valid_fn, stash):
    z_ref = m.z
    cq_ref, sq_ref, ck_ref, sk_ref = m.tabs
    decay_ref, qdec_ref, kdec_ref, _ = m.dec
    ka_ref, kb_ref, kc_ref, kd_ref = m.kv_bufs[:4]
    rws = slice(r0, r0 + blk)
    trs = slice(trow0, trow0 + blk)
    win = slice(win0, win0 + lk)

    def ret_front(hp):
        cq, sq, ck, sk = cq_ref[trs, :], sq_ref[trs, :], ck_ref[trs, :], sk_ref[trs, :]
        qs, ks, vs, qds, kds = [], [], [], [], []
        for hd in (2 * hp, 2 * hp + 1):
            c = hd * RET_DK
            q = z_ref[rws, OFF_RQ + c:OFF_RQ + c + RET_DK]
            k = z_ref[rws, OFF_RK + c:OFF_RK + c + RET_DK]
            q = q * cq + pltpu.roll(q, RET_DK // 2, 1) * sq
            k = k * ck + pltpu.roll(k, RET_DK // 2, 1) * sk
            qs.append(q.astype(BF16))
            ks.append(k.astype(BF16))
            vs.append(z_ref[rws, OFF_RV + c:OFF_RV + c + RET_DV].astype(BF16))
            qds.append((q * qdec_ref[hd]).astype(BF16))
            kds.append(k * kdec_ref[hd])
        s2 = _dot_nt(jnp.concatenate(qs, axis=1), _block_diag(*ks)) * decay_ref[hp]
        lhs = jnp.concatenate([s2.astype(BF16)] + qds, axis=1)
        kd_t = jnp.concatenate(kds, axis=0).T.astype(BF16)
        stash.put(("ret", hp), lhs, kd_t, jnp.concatenate(vs, axis=1))

    def gates():
        stash.put(("gate", 0), jnp.concatenate([_silu(z_ref[rws, OFF_RG:OFF_RG + RET_WIDTH]),
                                                _silu(z_ref[rws, OFF_AG:OFF_AG + ATT_WIDTH])], axis=1))

    def att_front(kv):
        low = _low_lanes()
        valid = None if valid_fn is None else valid_fn()
        slabs = (2 * kv, 2 * kv + 1)
        k_lo, k_hi = (ka_ref, kb_ref) if kv == 0 else (kc_ref, kd_ref)
        qf = jnp.concatenate([z_ref[rws, OFF_AQ + sl * LANES:OFF_AQ + (sl + 1) * LANES] for sl in slabs], axis=0)
        qf = (qf * (ATT_HEAD_DIM ** -0.5 * LOG2E)).astype(BF16)
        keys = jnp.concatenate([k_lo[win, :], k_hi[win, :]], axis=0)
        s = _dot_nt(qf, keys)
        ps, es = [], []
        for i, sl in enumerate(slabs):
            row_p, row_e = [], []
            for h in range(2):
                sh = s[i * blk:(i + 1) * blk, h * lk:(h + 1) * lk]
                if valid is not None:
                    sh = jnp.where(valid, sh, MASKED)
                sink = m.sink[layer, 2 * sl + h] * LOG2E
                mh = jnp.maximum(jnp.max(sh, axis=-1, keepdims=True), sink)
                row_p.append(jnp.exp2(sh - mh).astype(BF16))
                row_e.append(jnp.exp2(sink - mh))
            ps.append(jnp.concatenate(row_p, axis=1))
            es.append(jnp.broadcast_to(jnp.where(low, row_e[0], row_e[1]), (blk, LANES)))
        stash.put(("att", kv), jnp.concatenate(ps, axis=0), *es)

    return ([functools.partial(ret_front, hp) for hp in range(RET_HEADS // 2)] + [gates]
            + [functools.partial(att_front, kv) for kv in range(ATT_KV_HEADS)])


def _back_items(m: _MixRefs, r0, blk, win0, lk, r_get, r_set, stash):
    mix_ref = m.mix
    gl_ref = m.dec[3]
    va_ref, vb_ref, vc_ref, vd_ref = m.kv_bufs[4:]
    rws = slice(r0, r0 + blk)
    win = slice(win0, win0 + lk)

    def ret_back(hp):
        heads = (2 * hp, 2 * hp + 1)
        lhs, kd_t, v2 = stash.get(("ret", hp))
        gate = stash.get(("gate", 0))[0]
        r_old = [r_get(hd) for hd in heads]
        v_bd = _block_diag(v2[:, :RET_DV], v2[:, RET_DV:])
        r_bd = _block_diag(*[r.astype(BF16) for r in r_old])
        o2 = _dot(lhs, jnp.concatenate([v_bd, r_bd], axis=0))
        r_inc = _dot(kd_t, v_bd)
        for i, hd in enumerate(heads):
            c = hd * RET_DK
            r_set(hd, gl_ref[hd] * r_old[i] + r_inc[:, i * RET_DV:(i + 1) * RET_DV])
            o = o2[:, i * RET_DV:(i + 1) * RET_DV]
            mu = jnp.mean(o, axis=-1, keepdims=True)
            oc = o - mu
            var = jnp.mean(oc * oc, axis=-1, keepdims=True)
            on = oc * lax.rsqrt(var + GN_EPS)
            mix_ref[rws, c:c + RET_DV] = (on * gate[:, c:c + RET_DV]).astype(BF16)

    def att_back(kv):
        v_lo, v_hi = (va_ref, vb_ref) if kv == 0 else (vc_ref, vd_ref)
        p, *es = stash.get(("att", kv))
        gate = stash.get(("gate", 0))[0]
        vals = jnp.concatenate([v_lo[win, :], v_hi[win, :]], axis=0)
        acc = _dot(p, vals)
        for i, sl in enumerate((2 * kv, 2 * kv + 1)):
            c = RET_WIDTH + sl * LANES
            a = acc[i * blk:(i + 1) * blk, :]
            den = a[:, LANES:] + es[i]
            mix_ref[rws, c:c + LANES] = (a[:, :LANES] / den * gate[:, c:c + LANES]).astype(BF16)

    return ([functools.partial(ret_back, hp) for hp in range(RET_HEADS // 2)]
            + [functools.partial(att_back, kv) for kv in range(ATT_KV_HEADS)])


def _final_norm_rows(xo_ref, r0, n, fg_row):
    for a in range(r0, r0 + n, NORM_ROWS):
        xn = xo_ref[a:a + NORM_ROWS, :]
        ms = jnp.mean(xn * xn, axis=-1, keepdims=True)
        xo_ref[a:a + NORM_ROWS, :] = xn * lax.rsqrt(ms + NORM_EPS) * fg_row


class _PromptCfg(NamedTuple):
    rows: int
    nt: int
    ntiles: int
    last: bool
    layer: int


N_STASH = 6
N_KV = 8


def _prompt_kernel(xn_ref, xc_ref, modn_ref, modc_ref, g_ref, win_ref, wout_ref,
                   cq_ref, sq_ref, ck_ref, sk_ref, decay_ref, qdec_ref, kdec_ref, gl_ref, sink_ref, fg_ref,
                   _ro_prev, _ko_prev, _vo_prev,
                   xo_ref, ro_ref, ko_ref, vo_ref,
                   za_ref, zb_ref, h_ref, mixa_ref, mixb_ref, r_scr, *bufs,
                   cfg: _PromptCfg):
    rows, nt, ntiles = cfg.rows, cfg.nt, cfg.ntiles
    blk, lk = PROMPT_BLOCK, WINDOW + PROMPT_BLOCK
    stash_refs = (bufs[0:N_STASH], bufs[N_STASH:2 * N_STASH])
    kv_refs = (bufs[2 * N_STASH:2 * N_STASH + N_KV], bufs[2 * N_STASH + N_KV:2 * N_STASH + 2 * N_KV])
    z_refs = (za_ref, zb_ref)
    mix_refs = (mixa_ref, mixb_ref)
    g = pl.program_id(0)
    t = jnp.clip(g - 1, 0, ntiles - 1) % nt
    t3 = jnp.clip(g - 2, 0, ntiles - 1) % nt

    @pl.when(g == 0)
    def _():
        zb_ref[...] = jnp.zeros(zb_ref.shape, F32)
        mixb_ref[...] = jnp.zeros(mixb_ref.shape, BF16)
        for b in stash_refs[0] + kv_refs[0]:
            b[...] = jnp.zeros(b.shape, b.dtype)
        stash_refs[0][-1][...] = jnp.ones(stash_refs[0][-1].shape, F32)
        _init_kv_ones(kv_refs[0])
        _init_kv_ones(kv_refs[1])

    @pl.when(t3 == 0)
    def _():
        r_scr[...] = jnp.zeros(r_scr.shape, F32)

    def r_set(hd, val):
        r_scr[hd] = val

    def visible(j):
        col_i = lax.broadcasted_iota(jnp.int32, (blk, lk), 1)
        ok = None
        if blk == 2 * CHUNK:
            row_i = lax.broadcasted_iota(jnp.int32, (blk, lk), 0)
            first_key = jnp.where(row_i < CHUNK, 0, CHUNK)
            ok = (col_i >= first_key) & (col_i < first_key + WINDOW + CHUNK)
        if j * blk < WINDOW:
            started = col_i >= jnp.where(t == 0, WINDOW - j * blk, 0)
            ok = started if ok is None else ok & started
        return ok

    def step(p):
        a, b = p, 1 - p
        tabs = (cq_ref, sq_ref, ck_ref, sk_ref)
        dec = (decay_ref, qdec_ref, kdec_ref, gl_ref)

        norm = _norm_items(xn_ref, h_ref, 0, rows, g_ref, modn_ref, 0)
        proj = _in_proj_items(h_ref, win_ref, z_refs[a])

        def out_item(c0):
            cs = slice(c0, c0 + PROJ_COLS)
            gate_row = modc_ref[0][:, 2 * D_MODEL + c0:2 * D_MODEL + c0 + PROJ_COLS]
            xo_ref[:, cs] = xc_ref[:, cs] + gate_row * _dot(mix_refs[b][...], wout_ref[:, cs])
        outp = [functools.partial(out_item, c0) for c0 in range(0, D_MODEL, PROJ_COLS)]

        zr_ref = z_refs[b]
        for dst, src in zip(kv_refs[b], kv_refs[a]):
            dst[0:WINDOW, :] = src[rows:rows + WINDOW, :]
        for r0 in range(0, rows, CHUNK):
            _fill_kv(kv_refs[b], WINDOW + r0, CHUNK, zr_ref[r0:r0 + CHUNK, OFF_AK:OFF_AK + KV_WIDTH],
                     zr_ref[r0:r0 + CHUNK, OFF_AV:OFF_AV + KV_WIDTH])
        front = []
        for j in range(rows // blk):
            front += _front_items(_MixRefs(zr_ref, None, tabs, dec, sink_ref, kv_refs[b]), cfg.layer,
                                  j * blk, blk, j * blk, j * blk, lk, functools.partial(visible, j),
                                  _RefStash(stash_refs[b], j, blk))

        back = []
        for j in range(rows // blk):
            back += _back_items(_MixRefs(None, mix_refs[a], tabs, dec, sink_ref, kv_refs[a]),
                                j * blk, blk, j * blk, lk, lambda hd: r_scr[hd], r_set,
                                _RefStash(stash_refs[a], j, blk))

        _emit(_interleave(norm, outp))
        _emit(_interleave(_interleave(front, back), proj))
        if cfg.last:
            _final_norm_rows(xo_ref, 0, rows, fg_ref[...])

        @pl.when(t == nt - 1)
        def _():
            ko_ref[...] = zr_ref[rows - WINDOW:rows, OFF_AK:OFF_AK + KV_WIDTH]
            vo_ref[...] = zr_ref[rows - WINDOW:rows, OFF_AV:OFF_AV + KV_WIDTH]

    for parity in (0, 1):
        @pl.when(g % 2 == parity)
        def _():
            step(parity)

    @pl.when((t3 == nt - 1) & (g <= ntiles + 1))
    def _():
        ro_ref[...] = r_scr[...]


def _stash_scratch(rows, blk, lk):
    nblk = rows // blk
    return [
        pltpu.VMEM((rows, 2 * (2 * blk + 2 * RET_DK)), BF16),
        pltpu.VMEM((nblk * 2 * RET_DK, 2 * blk), BF16),
        pltpu.VMEM((rows, RET_WIDTH), BF16),
        pltpu.VMEM((rows, MIX_WIDTH), F32),
        pltpu.VMEM((nblk * 2 * 2 * blk, 2 * lk), BF16),
        pltpu.VMEM((rows, ATT_WIDTH), F32),
    ]


def _kv_scratch(krows):
    return ([pltpu.VMEM((krows, LANES), BF16) for _ in range(4)]
            + [pltpu.VMEM((krows, 2 * LANES), BF16) for _ in range(4)])


def _const_spec(shape):
    nd = len(shape)
    return pl.BlockSpec(shape, lambda g: (0,) * nd)


def _layer_spec(shape, layer):
    nd = len(shape)
    return pl.BlockSpec((None, *shape), lambda g: (layer,) + (0,) * nd)


def _smem_spec():
    return pl.BlockSpec(memory_space=pltpu.SMEM)


N_STATE = 3


def _state_alias(kernel, n_in, acc):
    out_shape = [jax.ShapeDtypeStruct(a.shape, a.dtype) for a in acc]
    if all(isinstance(a, jax.ShapeDtypeStruct) for a in acc):
        def first(*refs, **kw):
            return kernel(*refs[:n_in], *([None] * N_STATE), *refs[n_in:], **kw)
        return first, [], [], {}, out_shape
    in_specs = [pl.BlockSpec(memory_space=pl.ANY)] * N_STATE
    aliases = {n_in + i: 1 + i for i in range(N_STATE)}
    return kernel, list(acc), in_specs, aliases, out_shape


def _prompt_layer(layer, last, x2d, mod, g, w_in, w_out, tabs, dec, sink, final_g, acc, batch, seq):
    rows = PROMPT_ROWS
    blk, lk = PROMPT_BLOCK, WINDOW + PROMPT_BLOCK
    nt = seq // rows
    ntiles = batch * nt
    cfg = _PromptCfg(rows=rows, nt=nt, ntiles=ntiles, last=last, layer=layer)
    cq, sq, ck, sk = tabs
    decay, qdec, kdec, gl = dec

    nxt = lambda g: jnp.minimum(g, ntiles - 1)
    cur = lambda g: jnp.clip(g - 1, 0, ntiles - 1)
    bak = lambda g: jnp.clip(g - 2, 0, ntiles - 1)
    fin = lambda g: jnp.maximum(g - 3, 0)
    tab_spec = pl.BlockSpec((rows, LANES), lambda g: (cur(g) % nt, 0))
    in_specs = [
        pl.BlockSpec((rows, D_MODEL), lambda g: (nxt(g), 0)),
        pl.BlockSpec((rows, D_MODEL), lambda g: (fin(g), 0)),
        pl.BlockSpec((None, 1, 1, 3 * D_MODEL), lambda g: (layer, nxt(g) // nt, 0, 0)),
        pl.BlockSpec((None, 1, 1, 3 * D_MODEL), lambda g: (layer, fin(g) // nt, 0, 0)),
        _layer_spec((1, D_MODEL), layer),
        _layer_spec((D_MODEL, IN_WIDTH), layer),
        _layer_spec((MIX_WIDTH, D_MODEL), layer),
        tab_spec, tab_spec, tab_spec, tab_spec,
        _const_spec(decay.shape), _const_spec(qdec.shape), _const_spec(kdec.shape),
        _smem_spec(), _smem_spec(),
        _const_spec((1, D_MODEL)),
    ]
    body, acc_args, acc_specs, aliases, acc_shape = _state_alias(_prompt_kernel, len(in_specs), acc)
    out_specs = [
        pl.BlockSpec((rows, D_MODEL), lambda g: (fin(g), 0)),
        pl.BlockSpec((None, None, RET_HEADS, RET_DK, RET_DV), lambda g: (layer, bak(g) // nt, 0, 0, 0)),
        pl.BlockSpec((None, None, WINDOW, KV_WIDTH), lambda g: (layer, cur(g) // nt, 0, 0)),
        pl.BlockSpec((None, None, WINDOW, KV_WIDTH), lambda g: (layer, cur(g) // nt, 0, 0)),
    ]
    out_shape = [jax.ShapeDtypeStruct((batch * seq, D_MODEL), F32)] + acc_shape
    scratch = [
        pltpu.VMEM((rows, IN_WIDTH), F32),
        pltpu.VMEM((rows, IN_WIDTH), F32),
        pltpu.VMEM((rows, D_MODEL), BF16),
        pltpu.VMEM((rows, MIX_WIDTH), BF16),
        pltpu.VMEM((rows, MIX_WIDTH), BF16),
        pltpu.VMEM((RET_HEADS, RET_DK, RET_DV), F32),
        *_stash_scratch(rows, blk, lk), *_stash_scratch(rows, blk, lk),
        *_kv_scratch(WINDOW + rows), *_kv_scratch(WINDOW + rows),
    ]
    return pl.pallas_call(
        functools.partial(body, cfg=cfg),
        grid=(ntiles + PROMPT_STAGES - 1,),
        in_specs=in_specs + acc_specs, out_specs=out_specs, out_shape=out_shape,
        input_output_aliases=aliases,
        scratch_shapes=scratch,
        compiler_params=pltpu.CompilerParams(
            dimension_semantics=("arbitrary",),
            vmem_limit_bytes=VMEM_LIMIT_BYTES),
        name=f"prompt_layer{layer}",
    )(x2d, x2d, mod, mod, g, w_in, w_out, cq, sq, ck, sk, decay, qdec, kdec, gl, sink, final_g, *acc_args)


class _SampleCfg(NamedTuple):
    nseq: int
    seq: int
    last: bool
    layer: int


def _sample_kernel(x_ref, mod_ref, g_ref, win_ref, wout_ref,
                   cq_ref, sq_ref, ck_ref, sk_ref, decay_ref, qdec_ref, kdec_ref, gl_ref, sink_ref, fg_ref,
                   st_ref, cachek_ref, cachev_ref,
                   _ro_prev, _ko_prev, _vo_prev,
                   xo_ref, ro_ref, ko_ref, vo_ref,
                   z_ref, h_ref, mix_ref, *kv_bufs,
                   cfg: _SampleCfg):
    nseq, seq = cfg.nseq, cfg.seq
    lk = SAMPLE_KEYS
    pad = lk - WINDOW - seq

    @pl.when(pl.program_id(0) == 0)
    def _():
        _init_kv_ones(kv_bufs)

    for s in range(nseq):
        _emit(_norm_items(x_ref, h_ref, s * seq, seq, g_ref, mod_ref, s))
    _emit(_in_proj_items(h_ref, win_ref, z_ref))

    def visible():
        return lax.broadcasted_iota(jnp.int32, (seq, lk), 1) >= pad

    m = _MixRefs(z_ref, mix_ref, (cq_ref, sq_ref, ck_ref, sk_ref),
                 (decay_ref, qdec_ref, kdec_ref, gl_ref), sink_ref, kv_bufs)
    for s in range(nseq):
        rws = slice(s * seq, (s + 1) * seq)
        k_new = z_ref[rws, OFF_AK:OFF_AK + KV_WIDTH]
        v_new = z_ref[rws, OFF_AV:OFF_AV + KV_WIDTH]
        for b in kv_bufs:
            b[s * lk:s * lk + pad, 0:LANES] = jnp.zeros((pad, LANES), BF16)
        _fill_kv(kv_bufs, s * lk + pad, WINDOW, cachek_ref[s], cachev_ref[s])
        _fill_kv(kv_bufs, s * lk + pad + WINDOW, seq, k_new, v_new)

        def r_set(hd, val, s=s):
            ro_ref[s, hd] = val

        stash = _ValueStash()
        _emit(_front_items(m, cfg.layer, s * seq, seq, 0, s * lk, lk, visible, stash))
        _emit(_back_items(m, s * seq, seq, s * lk, lk, lambda hd, s=s: st_ref[s, hd], r_set, stash))
        ko_ref[s, 0:WINDOW - seq, :] = cachek_ref[s, seq:WINDOW, :]
        vo_ref[s, 0:WINDOW - seq, :] = cachev_ref[s, seq:WINDOW, :]
        ko_ref[s, WINDOW - seq:WINDOW, :] = k_new
        vo_ref[s, WINDOW - seq:WINDOW, :] = v_new

    xo_ref[...] = _dot(mix_ref[...], wout_ref[...])
    for s in range(nseq):
        rws = slice(s * seq, (s + 1) * seq)
        gate_row = mod_ref[s][:, 2 * D_MODEL:3 * D_MODEL]
        xo_ref[rws, :] = x_ref[rws, :] + gate_row * xo_ref[rws, :]
    if cfg.last:
        _final_norm_rows(xo_ref, 0, nseq * seq, fg_ref[...])


def _sample_layer(layer, last, x2d, mod, g, w_in, w_out, tabs, dec, sink, final_g,
                  state, cache_k, cache_v, acc, batch, seq, mod_row0):
    nseq = SAMPLE_SEQS
    rows = nseq * seq
    cfg = _SampleCfg(nseq=nseq, seq=seq, last=last, layer=layer)
    cq, sq, ck, sk = tabs
    decay, qdec, kdec, gl = dec
    mod_blk0 = mod_row0 // nseq
    in_specs = [
        pl.BlockSpec((rows, D_MODEL), lambda i: (i, 0)),
        pl.BlockSpec((None, nseq, 1, 3 * D_MODEL), lambda i: (layer, mod_blk0 + i, 0, 0)),
        _layer_spec((1, D_MODEL), layer),
        _layer_spec((D_MODEL, IN_WIDTH), layer),
        _layer_spec((MIX_WIDTH, D_MODEL), layer),
        _const_spec(cq.shape), _const_spec(sq.shape), _const_spec(ck.shape), _const_spec(sk.shape),
        _const_spec(decay.shape), _const_spec(qdec.shape), _const_spec(kdec.shape),
        _smem_spec(), _smem_spec(),
        _const_spec((1, D_MODEL)),
        pl.BlockSpec((None, nseq, RET_HEADS, RET_DK, RET_DV), lambda i: (layer, i, 0, 0, 0)),
        pl.BlockSpec((None, nseq, WINDOW, KV_WIDTH), lambda i: (layer, i, 0, 0)),
        pl.BlockSpec((None, nseq, WINDOW, KV_WIDTH), lambda i: (layer, i, 0, 0)),
    ]
    body, acc_args, acc_specs, aliases, acc_shape = _state_alias(_sample_kernel, len(in_specs), acc)
    out_specs = [
        pl.BlockSpec((rows, D_MODEL), lambda i: (i, 0)),
        pl.BlockSpec((None, nseq, RET_HEADS, RET_DK, RET_DV), lambda i: (layer, i, 0, 0, 0)),
        pl.BlockSpec((None, nseq, WINDOW, KV_WIDTH), lambda i: (layer, i, 0, 0)),
        pl.BlockSpec((None, nseq, WINDOW, KV_WIDTH), lambda i: (layer, i, 0, 0)),
    ]
    out_shape = [jax.ShapeDtypeStruct((batch * seq, D_MODEL), F32)] + acc_shape
    scratch = [
        pltpu.VMEM((rows, IN_WIDTH), F32),
        pltpu.VMEM((rows, D_MODEL), BF16),
        pltpu.VMEM((rows, MIX_WIDTH), BF16),
        *_kv_scratch(nseq * SAMPLE_KEYS),
    ]
    return pl.pallas_call(
        functools.partial(body, cfg=cfg),
        grid=(batch // nseq,),
        in_specs=in_specs + acc_specs, out_specs=out_specs, out_shape=out_shape,
        input_output_aliases=aliases,
        scratch_shapes=scratch,
        compiler_params=pltpu.CompilerParams(
            dimension_semantics=("arbitrary",),
            vmem_limit_bytes=VMEM_LIMIT_BYTES),
        name=f"sample_layer{layer}",
    )(x2d, mod, g, w_in, w_out, cq, sq, ck, sk, decay, qdec, kdec, gl, sink, final_g,
      state, cache_k, cache_v, *acc_args)


def _rope_tables(start, n):
    d = RET_DK
    inv = 1.0 / (ROPE_BASE ** (jnp.arange(0, d, 2, dtype=F32) / d))
    ang_a = jnp.arange(start, start + n, CHUNK).astype(F32)[:, None] * inv[None, :]
    ang_b = jnp.arange(CHUNK).astype(F32)[:, None] * inv[None, :]
    ca, sa, cb, sb = lax.optimization_barrier((jnp.cos(ang_a), jnp.sin(ang_a), jnp.cos(ang_b), jnp.sin(ang_b)))
    cos = (ca[:, None, :] * cb[None] - sa[:, None, :] * sb[None]).reshape(n, d // 2)
    sin = (sa[:, None, :] * cb[None] + ca[:, None, :] * sb[None]).reshape(n, d // 2)
    cos2 = jnp.concatenate([cos, cos], axis=-1)
    sin2 = jnp.concatenate([-sin, sin], axis=-1)
    kscale = RET_DK ** -0.5
    return cos2, sin2, cos2 * kscale, sin2 * kscale


def _decay_tables(n):
    lg = jnp.log(1.0 - 2.0 ** (-5.0 - jnp.arange(RET_HEADS, dtype=F32)))
    idx = jnp.arange(n, dtype=F32)
    diff = idx[:, None] - idx[None, :]
    decay = jnp.where(diff[None] >= 0, jnp.exp(jnp.maximum(diff, 0.0)[None] * lg[:, None, None]), 0.0)
    qdec = jnp.exp((idx + 1.0)[None, :] * lg[:, None])
    kdec = jnp.exp((n - 1.0 - idx)[None, :] * lg[:, None])
    gl = jnp.exp(n * lg)
    bcast = lambda a: jnp.broadcast_to(a[:, :, None], (RET_HEADS, n, LANES))
    decay2 = jnp.concatenate([decay[0::2], decay[1::2]], axis=-1)
    return decay2, bcast(qdec), bcast(kdec), gl


def kernel(x_prompt, x_sample, c_prompt, c_sample, state_ret, cache_k, cache_v,
           norm_g, w_ada, b_ada, w_in, sink, w_out, final_g):
    batch, seq, _ = x_prompt.shape
    dbatch, dseq, _ = x_sample.shape

    c_all = jnp.concatenate([c_prompt, c_sample], axis=0)
    mod = _adaln(c_all, w_ada, b_ada).reshape(DEPTH, batch + dbatch, 1, 3 * D_MODEL)
    w_in_b = w_in.astype(BF16)
    w_out_b = w_out.astype(BF16)

    tabs_p = _rope_tables(0, seq)
    tabs_s = _rope_tables(PAST_LEN, dseq)
    dec_p = _decay_tables(PROMPT_BLOCK)
    dec_s = _decay_tables(dseq)
    ck4 = cache_k.reshape(DEPTH, dbatch, WINDOW, KV_WIDTH)
    cv4 = cache_v.reshape(DEPTH, dbatch, WINDOW, KV_WIDTH)
    fg = final_g.reshape(1, D_MODEL)
    g_all = norm_g.reshape(DEPTH, 1, D_MODEL)

    xp = x_prompt.reshape(batch * seq, D_MODEL)
    xs = x_sample.reshape(dbatch * dseq, D_MODEL)
    def state_acc(nb):
        return (jax.ShapeDtypeStruct((DEPTH, nb, RET_HEADS, RET_DK, RET_DV), F32),
                jax.ShapeDtypeStruct((DEPTH, nb, WINDOW, KV_WIDTH), F32),
                jax.ShapeDtypeStruct((DEPTH, nb, WINDOW, KV_WIDTH), F32))

    acc_p, acc_s = state_acc(batch), state_acc(dbatch)
    for l in range(DEPTH):
        last = l == DEPTH - 1
        xp, *acc_p = _prompt_layer(l, last, xp, mod, g_all, w_in_b, w_out_b, tabs_p, dec_p,
                                   sink, fg, acc_p, batch, seq)
        xs, *acc_s = _sample_layer(l, last, xs, mod, g_all, w_in_b, w_out_b, tabs_s, dec_s,
                                   sink, fg, state_ret, ck4, cv4, acc_s, dbatch, dseq, mod_row0=batch)

    kv_shape_p = (DEPTH, batch, WINDOW, ATT_KV_HEADS, ATT_HEAD_DIM)
    kv_shape_s = (DEPTH, dbatch, WINDOW, ATT_KV_HEADS, ATT_HEAD_DIM)
    return (xp.reshape(batch, seq, D_MODEL), xs.reshape(dbatch, dseq, D_MODEL),
            acc_p[0], acc_p[1].reshape(kv_shape_p), acc_p[2].reshape(kv_shape_p),
            acc_s[0], acc_s[1].reshape(kv_shape_s), acc_s[2].reshape(kv_shape_s))
```

```python
import functools
from typing import NamedTuple

import jax
import jax.numpy as jnp
from jax import lax
from jax.experimental import pallas as pl
from jax.experimental.pallas import tpu as pltpu

D_MODEL = 1024
DEPTH = 4
CHUNK = 64
PAST_LEN = 4096
RET_HEADS = 4
RET_DK = 128
RET_DV = 128
RET_WIDTH = RET_HEADS * RET_DV
ATT_HEAD_DIM = 64
ATT_Q_HEADS = 8
ATT_KV_HEADS = 2
ATT_WIDTH = ATT_Q_HEADS * ATT_HEAD_DIM
KV_WIDTH = ATT_KV_HEADS * ATT_HEAD_DIM
WINDOW = 128
MIX_WIDTH = RET_WIDTH + ATT_WIDTH
ROPE_BASE = 10000.0
NORM_EPS = 1e-6
GN_EPS = 1e-5

OFF_RQ = 0
OFF_RK = OFF_RQ + RET_HEADS * RET_DK
OFF_RV = OFF_RK + RET_HEADS * RET_DK
OFF_RG = OFF_RV + RET_WIDTH
OFF_AQ = OFF_RG + RET_WIDTH
OFF_AK = OFF_AQ + ATT_WIDTH
OFF_AV = OFF_AK + KV_WIDTH
OFF_AG = OFF_AV + KV_WIDTH
IN_WIDTH = OFF_AG + ATT_WIDTH

LANES = 128
ATT_SLABS = ATT_WIDTH // LANES
MASKED = -1e30
LOG2E = 1.4426950408889634
VMEM_LIMIT_BYTES = 56 * 1024 * 1024
NORM_ROWS = 32
PROJ_COLS = 512

PROMPT_ROWS = 256
PROMPT_BLOCK = 128
PROMPT_STAGES = 4
SAMPLE_SEQS = 4
SAMPLE_KEYS = 2 * LANES

BF16 = jnp.bfloat16
F32 = jnp.float32


def _silu(g):
    h = 0.5 * g
    return h + h * jnp.tanh(h)


def _dot(a, b):
    return jnp.dot(a, b, preferred_element_type=F32)


def _dot_nt(a, b):
    return lax.dot_general(a, b, (((1,), (1,)), ((), ())), preferred_element_type=F32)


def _low_lanes():
    lane = lax.broadcasted_iota(jnp.int32, (1, LANES), 1)
    return lane < ATT_HEAD_DIM


def _adaln_kernel(c_ref, w_ref, b_ref, o_ref):
    s = _silu(c_ref[...]).astype(BF16)
    o_ref[0] = _dot(s, w_ref[0].astype(BF16)) + b_ref[0]


def _adaln(c_all, w_ada, b_ada):
    n = c_all.shape[0]
    tn = D_MODEL
    return pl.pallas_call(
        _adaln_kernel,
        grid=(DEPTH, 3 * D_MODEL // tn),
        in_specs=[
            pl.BlockSpec((n, D_MODEL), lambda l, j: (0, 0)),
            pl.BlockSpec((1, D_MODEL, tn), lambda l, j: (l, 0, j)),
            pl.BlockSpec((1, 1, tn), lambda l, j: (l, 0, j)),
        ],
        out_specs=pl.BlockSpec((1, n, tn), lambda l, j: (l, 0, j)),
        out_shape=jax.ShapeDtypeStruct((DEPTH, n, 3 * D_MODEL), F32),
        compiler_params=pltpu.CompilerParams(
            dimension_semantics=("arbitrary", "arbitrary"),
            vmem_limit_bytes=VMEM_LIMIT_BYTES),
        name="adaln",
    )(c_all, w_ada, b_ada.reshape(DEPTH, 1, 3 * D_MODEL))


def _emit(items):
    for it in items:
        it()


def _interleave(a, b):
    if not a:
        return list(b)
    out, nb = [], 0
    for i, it in enumerate(a):
        out.append(it)
        want = (i + 1) * len(b) // len(a)
        out.extend(b[nb:want])
        nb = want
    return out


def _norm_items(x_ref, h_ref, r0, n, g_ref, mod_ref, s):
    def item(a):
        mod_row = mod_ref[s]
        shift = mod_row[:, 0:D_MODEL]
        gain = g_ref[...] * (1.0 + mod_row[:, D_MODEL:2 * D_MODEL])
        xb = x_ref[a:a + NORM_ROWS, :]
        ms = jnp.mean(xb * xb, axis=-1, keepdims=True)
        h_ref[a:a + NORM_ROWS, :] = (xb * lax.rsqrt(ms + NORM_EPS) * gain + shift).astype(BF16)
    return [functools.partial(item, a) for a in range(r0, r0 + n, NORM_ROWS)]


def _in_proj_items(h_ref, win_ref, z_ref):
    def item(c0):
        c1 = min(c0 + PROJ_COLS, IN_WIDTH)
        z_ref[:, c0:c1] = _dot(h_ref[...], win_ref[:, c0:c1])
    return [functools.partial(item, c0) for c0 in range(0, IN_WIDTH, PROJ_COLS)]


def _fill_kv(kv_bufs, dst0, n, k, v):
    ka_ref, kb_ref, kc_ref, kd_ref, va_ref, vb_ref, vc_ref, vd_ref = kv_bufs
    low = _low_lanes()
    kr = pltpu.roll(k, ATT_HEAD_DIM, 1)
    vr = pltpu.roll(v, ATT_HEAD_DIM, 1)
    d = slice(dst0, dst0 + n)
    ka_ref[d, :] = jnp.where(low, k, 0.0).astype(BF16)
    kb_ref[d, :] = jnp.where(low, 0.0, kr).astype(BF16)
    kc_ref[d, :] = jnp.where(low, kr, 0.0).astype(BF16)
    kd_ref[d, :] = jnp.where(low, 0.0, k).astype(BF16)
    va_ref[d, 0:LANES] = jnp.where(low, v, 0.0).astype(BF16)
    vb_ref[d, 0:LANES] = jnp.where(low, 0.0, vr).astype(BF16)
    vc_ref[d, 0:LANES] = jnp.where(low, vr, 0.0).astype(BF16)
    vd_ref[d, 0:LANES] = jnp.where(low, 0.0, v).astype(BF16)


def _init_kv_ones(kv_bufs):
    low = _low_lanes()
    for i, ref in enumerate(kv_bufs[4:]):
        ones = jnp.where(low, 1.0, 0.0) if i % 2 == 0 else jnp.where(low, 0.0, 1.0)
        ref[:, LANES:] = jnp.broadcast_to(ones, (ref.shape[0], LANES)).astype(BF16)


class _MixRefs(NamedTuple):
    z: object
    mix: object
    tabs: tuple
    dec: tuple
    sink: object
    kv_bufs: tuple


class _ValueStash:
    def __init__(self):
        self.d = {}

    def put(self, key, *vals):
        self.d[key] = vals

    def get(self, key):
        return self.d[key]


class _RefStash:
    def __init__(self, refs, j, blk):
        self.refs, self.j, self.blk = refs, j, blk
        self.rws = slice(j * blk, (j + 1) * blk)

    def _slots(self, key):
        lhs_ref, kdt_ref, v_ref, gate_ref, p_ref, e_ref = self.refs
        kind, i = key
        rws, j, blk = self.rws, self.j, self.blk
        if kind == "ret":
            w = lhs_ref.shape[1] // 2
            r0 = (2 * j + i) * RET_DK
            return [(lhs_ref, rws, slice(i * w, (i + 1) * w)), (kdt_ref, slice(r0, r0 + RET_DK), slice(None)),
                    (v_ref, rws, slice(2 * i * RET_DV, 2 * (i + 1) * RET_DV))]
        if kind == "gate":
            return [(gate_ref, rws, slice(None))]
        r0 = (2 * j + i) * 2 * blk
        return [(p_ref, slice(r0, r0 + 2 * blk), slice(None)),
                (e_ref, rws, slice(2 * i * LANES, (2 * i + 1) * LANES)),
                (e_ref, rws, slice((2 * i + 1) * LANES, (2 * i + 2) * LANES))]

    def put(self, key, *vals):
        for (ref, r, c), v in zip(self._slots(key), vals):
            ref[r, c] = v

    def get(self, key):
        return tuple(ref[r, c] for ref, r, c in self._slots(key))


def _block_diag(a, b):
    za = jnp.zeros(a.shape, a.dtype)
    return jnp.concatenate([jnp.concatenate([a, za], axis=1), jnp.concatenate([za, b], axis=1)], axis=0)


def _front_items(m: _MixRefs, layer, r0, blk, trow0, win0, lk, valid_fn, stash):
    z_ref = m.z
    cq_ref, sq_ref, ck_ref, sk_ref = m.tabs
    decay_ref, qdec_ref, kdec_ref, _ = m.dec
    ka_ref, kb_ref, kc_ref, kd_ref = m.kv_bufs[:4]
    rws = slice(r0, r0 + blk)
    trs = slice(trow0, trow0 + blk)
    win = slice(win0, win0 + lk)

    def ret_front(hp):
        cq, sq, ck, sk = cq_ref[trs, :], sq_ref[trs, :], ck_ref[trs, :], sk_ref[trs, :]
        qs, ks, vs, qds, kds = [], [], [], [], []
        for hd in (2 * hp, 2 * hp + 1):
            c = hd * RET_DK
            q = z_ref[rws, OFF_RQ + c:OFF_RQ + c + RET_DK]
            k = z_ref[rws, OFF_RK + c:OFF_RK + c + RET_DK]
            q = q * cq + pltpu.roll(q, RET_DK // 2, 1) * sq
            k = k * ck + pltpu.roll(k, RET_DK // 2, 1) * sk
            qs.append(q.astype(BF16))
            ks.append(k.astype(BF16))
            vs.append(z_ref[rws, OFF_RV + c:OFF_RV + c + RET_DV].astype(BF16))
            qds.append((q * qdec_ref[hd]).astype(BF16))
            kds.append(k * kdec_ref[hd])
        s2 = _dot_nt(jnp.concatenate(qs, axis=1), _block_diag(*ks)) * decay_ref[hp]
        lhs = jnp.concatenate([s2.astype(BF16)] + qds, axis=1)
        kd_t = jnp.concatenate(kds, axis=0).T.astype(BF16)
        stash.put(("ret", hp), lhs, kd_t, jnp.concatenate(vs, axis=1))

    def gates():
        stash.put(("gate", 0), jnp.concatenate([_silu(z_ref[rws, OFF_RG:OFF_RG + RET_WIDTH]),
                                                _silu(z_ref[rws, OFF_AG:OFF_AG + ATT_WIDTH])], axis=1))

    def att_front(kv):
        low = _low_lanes()
        valid = None if valid_fn is None else valid_fn()
        slabs = (2 * kv, 2 * kv + 1)
        k_lo, k_hi = (ka_ref, kb_ref) if kv == 0 else (kc_ref, kd_ref)
        qf = jnp.concatenate([z_ref[rws, OFF_AQ + sl * LANES:OFF_AQ + (sl + 1) * LANES] for sl in slabs], axis=0)
        qf = (qf * (ATT_HEAD_DIM ** -0.5 * LOG2E)).astype(BF16)
        keys = jnp.concatenate([k_lo[win, :], k_hi[win, :]], axis=0)
        s = _dot_nt(qf, keys)
        ps, es = [], []
        for i, sl in enumerate(slabs):
            row_p, row_e = [], []
            for h in range(2):
                sh = s[i * blk:(i + 1) * blk, h * lk:(h + 1) * lk]
                if valid is not None:
                    sh = jnp.where(valid, sh, MASKED)
                sink = m.sink[layer, 2 * sl + h] * LOG2E
                mh = jnp.maximum(jnp.max(sh, axis=-1, keepdims=True), sink)
                row_p.append(jnp.exp2(sh - mh).astype(BF16))
                row_e.append(jnp.exp2(sink - mh))
            ps.append(jnp.concatenate(row_p, axis=1))
            es.append(jnp.broadcast_to(jnp.where(low, row_e[0], row_e[1]), (blk, LANES)))
        stash.put(("att", kv), jnp.concatenate(ps, axis=0), *es)

    return ([functools.partial(ret_front, hp) for hp in range(RET_HEADS // 2)] + [gates]
            + [functools.partial(att_front, kv) for kv in range(ATT_KV_HEADS)])


def _back_items(m: _MixRefs, r0, blk, win0, lk, r_get, r_set, stash):
    mix_ref = m.mix
    gl_ref = m.dec[3]
    va_ref, vb_ref, vc_ref, vd_ref = m.kv_bufs[4:]
    rws = slice(r0, r0 + blk)
    win = slice(win0, win0 + lk)

    def ret_back(hp):
        heads = (2 * hp, 2 * hp + 1)
        lhs, kd_t, v2 = stash.get(("ret", hp))
        gate = stash.get(("gate", 0))[0]
        r_old = [r_get(hd) for hd in heads]
        v_bd = _block_diag(v2[:, :RET_DV], v2[:, RET_DV:])
        r_bd = _block_diag(*[r.astype(BF16) for r in r_old])
        o2 = _dot(lhs, jnp.concatenate([v_bd, r_bd], axis=0))
        r_inc = _dot(kd_t, v_bd)
        for i, hd in enumerate(heads):
            c = hd * RET_DK
            r_set(hd, gl_ref[hd] * r_old[i] + r_inc[:, i * RET_DV:(i + 1) * RET_DV])
            o = o2[:, i * RET_DV:(i + 1) * RET_DV]
            mu = jnp.mean(o, axis=-1, keepdims=True)
            oc = o - mu
            var = jnp.mean(oc * oc, axis=-1, keepdims=True)
            on = oc * lax.rsqrt(var + GN_EPS)
            mix_ref[rws, c:c + RET_DV] = (on * gate[:, c:c + RET_DV]).astype(BF16)

    def att_back(kv):
        v_lo, v_hi = (va_ref, vb_ref) if kv == 0 else (vc_ref, vd_ref)
        p, *es = stash.get(("att", kv))
        gate = stash.get(("gate", 0))[0]
        vals = jnp.concatenate([v_lo[win, :], v_hi[win, :]], axis=0)
        acc = _dot(p, vals)
        for i, sl in enumerate((2 * kv, 2 * kv + 1)):
            c = RET_WIDTH + sl * LANES
            a = acc[i * blk:(i + 1) * blk, :]
            den = a[:, LANES:] + es[i]
            mix_ref[rws, c:c + LANES] = (a[:, :LANES] / den * gate[:, c:c + LANES]).astype(BF16)

    return ([functools.partial(ret_back, hp) for hp in range(RET_HEADS // 2)]
            + [functools.partial(att_back, kv) for kv in range(ATT_KV_HEADS)])


def _final_norm_rows(xo_ref, r0, n, fg_row):
    for a in range(r0, r0 + n, NORM_ROWS):
        xn = xo_ref[a:a + NORM_ROWS, :]
        ms = jnp.mean(xn * xn, axis=-1, keepdims=True)
        xo_ref[a:a + NORM_ROWS, :] = xn * lax.rsqrt(ms + NORM_EPS) * fg_row


class _PromptCfg(NamedTuple):
    rows: int
    nt: int
    ntiles: int
    last: bool
    layer: int


N_STASH = 6
N_KV = 8


def _prompt_kernel(xn_ref, xc_ref, modn_ref, modc_ref, g_ref, win_ref, wout_ref,
                   cq_ref, sq_ref, ck_ref, sk_ref, decay_ref, qdec_ref, kdec_ref, gl_ref, sink_ref, fg_ref,
                   _ro_prev, _ko_prev, _vo_prev,
                   xo_ref, ro_ref, ko_ref, vo_ref,
                   za_ref, zb_ref, h_ref, mixa_ref, mixb_ref, r_scr, *bufs,
                   cfg: _PromptCfg):
    rows, nt, ntiles = cfg.rows, cfg.nt, cfg.ntiles
    blk, lk = PROMPT_BLOCK, WINDOW + PROMPT_BLOCK
    stash_refs = (bufs[0:N_STASH], bufs[N_STASH:2 * N_STASH])
    kv_refs = (bufs[2 * N_STASH:2 * N_STASH + N_KV], bufs[2 * N_STASH + N_KV:2 * N_STASH + 2 * N_KV])
    z_refs = (za_ref, zb_ref)
    mix_refs = (mixa_ref, mixb_ref)
    g = pl.program_id(0)
    t = jnp.clip(g - 1, 0, ntiles - 1) % nt
    t3 = jnp.clip(g - 2, 0, ntiles - 1) % nt

    @pl.when(g == 0)
    def _():
        zb_ref[...] = jnp.zeros(zb_ref.shape, F32)
        mixb_ref[...] = jnp.zeros(mixb_ref.shape, BF16)
        for b in stash_refs[0] + kv_refs[0]:
            b[...] = jnp.zeros(b.shape, b.dtype)
        stash_refs[0][-1][...] = jnp.ones(stash_refs[0][-1].shape, F32)
        _init_kv_ones(kv_refs[0])
        _init_kv_ones(kv_refs[1])

    @pl.when(t3 == 0)
    def _():
        r_scr[...] = jnp.zeros(r_scr.shape, F32)

    def r_set(hd, val):
        r_scr[hd] = val

    def visible(j):
        col_i = lax.broadcasted_iota(jnp.int32, (blk, lk), 1)
        ok = None
        if blk == 2 * CHUNK:
            row_i = lax.broadcasted_iota(jnp.int32, (blk, lk), 0)
            first_key = jnp.where(row_i < CHUNK, 0, CHUNK)
            ok = (col_i >= first_key) & (col_i < first_key + WINDOW + CHUNK)
        if j * blk < WINDOW:
            started = col_i >= jnp.where(t == 0, WINDOW - j * blk, 0)
            ok = started if ok is None else ok & started
        return ok

    def step(p):
        a, b = p, 1 - p
        tabs = (cq_ref, sq_ref, ck_ref, sk_ref)
        dec = (decay_ref, qdec_ref, kdec_ref, gl_ref)

        norm = _norm_items(xn_ref, h_ref, 0, rows, g_ref, modn_ref, 0)
        proj = _in_proj_items(h_ref, win_ref, z_refs[a])

        def out_item(c0):
            cs = slice(c0, c0 + PROJ_COLS)
            gate_row = modc_ref[0][:, 2 * D_MODEL + c0:2 * D_MODEL + c0 + PROJ_COLS]
            xo_ref[:, cs] = xc_ref[:, cs] + gate_row * _dot(mix_refs[b][...], wout_ref[:, cs])
        outp = [functools.partial(out_item, c0) for c0 in range(0, D_MODEL, PROJ_COLS)]

        zr_ref = z_refs[b]
        for dst, src in zip(kv_refs[b], kv_refs[a]):
            dst[0:WINDOW, :] = src[rows:rows + WINDOW, :]
        for r0 in range(0, rows, CHUNK):
            _fill_kv(kv_refs[b], WINDOW + r0, CHUNK, zr_ref[r0:r0 + CHUNK, OFF_AK:OFF_AK + KV_WIDTH],
                     zr_ref[r0:r0 + CHUNK, OFF_AV:OFF_AV + KV_WIDTH])
        front = []
        for j in range(rows // blk):
            front += _front_items(_MixRefs(zr_ref, None, tabs, dec, sink_ref, kv_refs[b]), cfg.layer,
                                  j * blk, blk, j * blk, j * blk, lk, functools.partial(visible, j),
                                  _RefStash(stash_refs[b], j, blk))

        back = []
        for j in range(rows // blk):
            back += _back_items(_MixRefs(None, mix_refs[a], tabs, dec, sink_ref, kv_refs[a]),
                                j * blk, blk, j * blk, lk, lambda hd: r_scr[hd], r_set,
                                _RefStash(stash_refs[a], j, blk))

        _emit(_interleave(norm, outp))
        _emit(_interleave(_interleave(front, back), proj))
        if cfg.last:
            _final_norm_rows(xo_ref, 0, rows, fg_ref[...])

        @pl.when(t == nt - 1)
        def _():
            ko_ref[...] = zr_ref[rows - WINDOW:rows, OFF_AK:OFF_AK + KV_WIDTH]
            vo_ref[...] = zr_ref[rows - WINDOW:rows, OFF_AV:OFF_AV + KV_WIDTH]

    for parity in (0, 1):
        @pl.when(g % 2 == parity)
        def _():
            step(parity)

    @pl.when((t3 == nt - 1) & (g <= ntiles + 1))
    def _():
        ro_ref[...] = r_scr[...]


def _stash_scratch(rows, blk, lk):
    nblk = rows // blk
    return [
        pltpu.VMEM((rows, 2 * (2 * blk + 2 * RET_DK)), BF16),
        pltpu.VMEM((nblk * 2 * RET_DK, 2 * blk), BF16),
        pltpu.VMEM((rows, RET_WIDTH), BF16),
        pltpu.VMEM((rows, MIX_WIDTH), F32),
        pltpu.VMEM((nblk * 2 * 2 * blk, 2 * lk), BF16),
        pltpu.VMEM((rows, ATT_WIDTH), F32),
    ]


def _kv_scratch(krows):
    return ([pltpu.VMEM((krows, LANES), BF16) for _ in range(4)]
            + [pltpu.VMEM((krows, 2 * LANES), BF16) for _ in range(4)])


def _const_spec(shape):
    nd = len(shape)
    return pl.BlockSpec(shape, lambda g: (0,) * nd)


def _layer_spec(shape, layer):
    nd = len(shape)
    return pl.BlockSpec((None, *shape), lambda g: (layer,) + (0,) * nd)


def _smem_spec():
    return pl.BlockSpec(memory_space=pltpu.SMEM)


N_STATE = 3


def _state_alias(kernel, n_in, acc):
    out_shape = [jax.ShapeDtypeStruct(a.shape, a.dtype) for a in acc]
    if all(isinstance(a, jax.ShapeDtypeStruct) for a in acc):
        def first(*refs, **kw):
            return kernel(*refs[:n_in], *([None] * N_STATE), *refs[n_in:], **kw)
        return first, [], [], {}, out_shape
    in_specs = [pl.BlockSpec(memory_space=pl.ANY)] * N_STATE
    aliases = {n_in + i: 1 + i for i in range(N_STATE)}
    return kernel, list(acc), in_specs, aliases, out_shape


def _prompt_layer(layer, last, x2d, mod, g, w_in, w_out, tabs, dec, sink, final_g, acc, batch, seq):
    rows = PROMPT_ROWS
    blk, lk = PROMPT_BLOCK, WINDOW + PROMPT_BLOCK
    nt = seq // rows
    ntiles = batch * nt
    cfg = _PromptCfg(rows=rows, nt=nt, ntiles=ntiles, last=last, layer=layer)
    cq, sq, ck, sk = tabs
    decay, qdec, kdec, gl = dec

    nxt = lambda g: jnp.minimum(g, ntiles - 1)
    cur = lambda g: jnp.clip(g - 1, 0, ntiles - 1)
    bak = lambda g: jnp.clip(g - 2, 0, ntiles - 1)
    fin = lambda g: jnp.maximum(g - 3, 0)
    tab_spec = pl.BlockSpec((rows, LANES), lambda g: (cur(g) % nt, 0))
    in_specs = [
        pl.BlockSpec((rows, D_MODEL), lambda g: (nxt(g), 0)),
        pl.BlockSpec((rows, D_MODEL), lambda g: (fin(g), 0)),
        pl.BlockSpec((None, 1, 1, 3 * D_MODEL), lambda g: (layer, nxt(g) // nt, 0, 0)),
        pl.BlockSpec((None, 1, 1, 3 * D_MODEL), lambda g: (layer, fin(g) // nt, 0, 0)),
        _layer_spec((1, D_MODEL), layer),
        _layer_spec((D_MODEL, IN_WIDTH), layer),
        _layer_spec((MIX_WIDTH, D_MODEL), layer),
        tab_spec, tab_spec, tab_spec, tab_spec,
        _const_spec(decay.shape), _const_spec(qdec.shape), _const_spec(kdec.shape),
        _smem_spec(), _smem_spec(),
        _const_spec((1, D_MODEL)),
    ]
    body, acc_args, acc_specs, aliases, acc_shape = _state_alias(_prompt_kernel, len(in_specs), acc)
    out_specs = [
        pl.BlockSpec((rows, D_MODEL), lambda g: (fin(g), 0)),
        pl.BlockSpec((None, None, RET_HEADS, RET_DK, RET_DV), lambda g: (layer, bak(g) // nt, 0, 0, 0)),
        pl.BlockSpec((None, None, WINDOW, KV_WIDTH), lambda g: (layer, cur(g) // nt, 0, 0)),
        pl.BlockSpec((None, None, WINDOW, KV_WIDTH), lambda g: (layer, cur(g) // nt, 0, 0)),
    ]
    out_shape = [jax.ShapeDtypeStruct((batch * seq, D_MODEL), F32)] + acc_shape
    scratch = [
        pltpu.VMEM((rows, IN_WIDTH), F32),
        pltpu.VMEM((rows, IN_WIDTH), F32),
        pltpu.VMEM((rows, D_MODEL), BF16),
        pltpu.VMEM((rows, MIX_WIDTH), BF16),
        pltpu.VMEM((rows, MIX_WIDTH), BF16),
        pltpu.VMEM((RET_HEADS, RET_DK, RET_DV), F32),
        *_stash_scratch(rows, blk, lk), *_stash_scratch(rows, blk, lk),
        *_kv_scratch(WINDOW + rows), *_kv_scratch(WINDOW + rows),
    ]
    return pl.pallas_call(
        functools.partial(body, cfg=cfg),
        grid=(ntiles + PROMPT_STAGES - 1,),
        in_specs=in_specs + acc_specs, out_specs=out_specs, out_shape=out_shape,
        input_output_aliases=aliases,
        scratch_shapes=scratch,
        compiler_params=pltpu.CompilerParams(
            dimension_semantics=("arbitrary",),
            vmem_limit_bytes=VMEM_LIMIT_BYTES),
        name=f"prompt_layer{layer}",
    )(x2d, x2d, mod, mod, g, w_in, w_out, cq, sq, ck, sk, decay, qdec, kdec, gl, sink, final_g, *acc_args)


class _SampleCfg(NamedTuple):
    nseq: int
    seq: int
    last: bool
    layer: int


def _sample_kernel(x_ref, mod_ref, g_ref, win_ref, wout_ref,
                   cq_ref, sq_ref, ck_ref, sk_ref, decay_ref, qdec_ref, kdec_ref, gl_ref, sink_ref, fg_ref,
                   st_ref, cachek_ref, cachev_ref,
                   _ro_prev, _ko_prev, _vo_prev,
                   xo_ref, ro_ref, ko_ref, vo_ref,
                   z_ref, h_ref, mix_ref, *kv_bufs,
                   cfg: _SampleCfg):
    nseq, seq = cfg.nseq, cfg.seq
    lk = SAMPLE_KEYS
    pad = lk - WINDOW - seq

    @pl.when(pl.program_id(0) == 0)
    def _():
        _init_kv_ones(kv_bufs)

    for s in range(nseq):
        _emit(_norm_items(x_ref, h_ref, s * seq, seq, g_ref, mod_ref, s))
    _emit(_in_proj_items(h_ref, win_ref, z_ref))

    def visible():
        return lax.broadcasted_iota(jnp.int32, (seq, lk), 1) >= pad

    m = _MixRefs(z_ref, mix_ref, (cq_ref, sq_ref, ck_ref, sk_ref),
                 (decay_ref, qdec_ref, kdec_ref, gl_ref), sink_ref, kv_bufs)
    for s in range(nseq):
        rws = slice(s * seq, (s + 1) * seq)
        k_new = z_ref[rws, OFF_AK:OFF_AK + KV_WIDTH]
        v_new = z_ref[rws, OFF_AV:OFF_AV + KV_WIDTH]
        for b in kv_bufs:
            b[s * lk:s * lk + pad, 0:LANES] = jnp.zeros((pad, LANES), BF16)
        _fill_kv(kv_bufs, s * lk + pad, WINDOW, cachek_ref[s], cachev_ref[s])
        _fill_kv(kv_bufs, s * lk + pad + WINDOW, seq, k_new, v_new)

        def r_set(hd, val, s=s):
            ro_ref[s, hd] = val

        stash = _ValueStash()
        _emit(_front_items(m, cfg.layer, s * seq, seq, 0, s * lk, lk, visible, stash))
        _emit(_back_items(m, s * seq, seq, s * lk, lk, lambda hd, s=s: st_ref[s, hd], r_set, stash))
        ko_ref[s, 0:WINDOW - seq, :] = cachek_ref[s, seq:WINDOW, :]
        vo_ref[s, 0:WINDOW - seq, :] = cachev_ref[s, seq:WINDOW, :]
        ko_ref[s, WINDOW - seq:WINDOW, :] = k_new
        vo_ref[s, WINDOW - seq:WINDOW, :] = v_new

    xo_ref[...] = _dot(mix_ref[...], wout_ref[...])
    for s in range(nseq):
        rws = slice(s * seq, (s + 1) * seq)
        gate_row = mod_ref[s][:, 2 * D_MODEL:3 * D_MODEL]
        xo_ref[rws, :] = x_ref[rws, :] + gate_row * xo_ref[rws, :]
    if cfg.last:
        _final_norm_rows(xo_ref, 0, nseq * seq, fg_ref[...])


def _sample_layer(layer, last, x2d, mod, g, w_in, w_out, tabs, dec, sink, final_g,
                  state, cache_k, cache_v, acc, batch, seq, mod_row0):
    nseq = SAMPLE_SEQS
    rows = nseq * seq
    cfg = _SampleCfg(nseq=nseq, seq=seq, last=last, layer=layer)
    cq, sq, ck, sk = tabs
    decay, qdec, kdec, gl = dec
    mod_blk0 = mod_row0 // nseq
    in_specs = [
        pl.BlockSpec((rows, D_MODEL), lambda i: (i, 0)),
        pl.BlockSpec((None, nseq, 1, 3 * D_MODEL), lambda i: (layer, mod_blk0 + i, 0, 0)),
        _layer_spec((1, D_MODEL), layer),
        _layer_spec((D_MODEL, IN_WIDTH), layer),
        _layer_spec((MIX_WIDTH, D_MODEL), layer),
        _const_spec(cq.shape), _const_spec(sq.shape), _const_spec(ck.shape), _const_spec(sk.shape),
        _const_spec(decay.shape), _const_spec(qdec.shape), _const_spec(kdec.shape),
        _smem_spec(), _smem_spec(),
        _const_spec((1, D_MODEL)),
        pl.BlockSpec((None, nseq, RET_HEADS, RET_DK, RET_DV), lambda i: (layer, i, 0, 0, 0)),
        pl.BlockSpec((None, nseq, WINDOW, KV_WIDTH), lambda i: (layer, i, 0, 0)),
        pl.BlockSpec((None, nseq, WINDOW, KV_WIDTH), lambda i: (layer, i, 0, 0)),
    ]
    body, acc_args, acc_specs, aliases, acc_shape = _state_alias(_sample_kernel, len(in_specs), acc)
    out_specs = [
        pl.BlockSpec((rows, D_MODEL), lambda i: (i, 0)),
        pl.BlockSpec((None, nseq, RET_HEADS, RET_DK, RET_DV), lambda i: (layer, i, 0, 0, 0)),
        pl.BlockSpec((None, nseq, WINDOW, KV_WIDTH), lambda i: (layer, i, 0, 0)),
        pl.BlockSpec((None, nseq, WINDOW, KV_WIDTH), lambda i: (layer, i, 0, 0)),
    ]
    out_shape = [jax.ShapeDtypeStruct((batch * seq, D_MODEL), F32)] + acc_shape
    scratch = [
        pltpu.VMEM((rows, IN_WIDTH), F32),
        pltpu.VMEM((rows, D_MODEL), BF16),
        pltpu.VMEM((rows, MIX_WIDTH), BF16),
        *_kv_scratch(nseq * SAMPLE_KEYS),
    ]
    return pl.pallas_call(
        functools.partial(body, cfg=cfg),
        grid=(batch // nseq,),
        in_specs=in_specs + acc_specs, out_specs=out_specs, out_shape=out_shape,
        input_output_aliases=aliases,
        scratch_shapes=scratch,
        compiler_params=pltpu.CompilerParams(
            dimension_semantics=("arbitrary",),
            vmem_limit_bytes=VMEM_LIMIT_BYTES),
        name=f"sample_layer{layer}",
    )(x2d, mod, g, w_in, w_out, cq, sq, ck, sk, decay, qdec, kdec, gl, sink, final_g,
      state, cache_k, cache_v, *acc_args)


def _rope_tables(start, n):
    d = RET_DK
    inv = 1.0 / (ROPE_BASE ** (jnp.arange(0, d, 2, dtype=F32) / d))
    ang_a = jnp.arange(start, start + n, CHUNK).astype(F32)[:, None] * inv[None, :]
    ang_b = jnp.arange(CHUNK).astype(F32)[:, None] * inv[None, :]
    ca, sa, cb, sb = lax.optimization_barrier((jnp.cos(ang_a), jnp.sin(ang_a), jnp.cos(ang_b), jnp.sin(ang_b)))
    cos = (ca[:, None, :] * cb[None] - sa[:, None, :] * sb[None]).reshape(n, d // 2)
    sin = (sa[:, None, :] * cb[None] + ca[:, None, :] * sb[None]).reshape(n, d // 2)
    cos2 = jnp.concatenate([cos, cos], axis=-1)
    sin2 = jnp.concatenate([-sin, sin], axis=-1)
    kscale = RET_DK ** -0.5
    return cos2, sin2, cos2 * kscale, sin2 * kscale


def _decay_tables(n):
    lg = jnp.log(1.0 - 2.0 ** (-5.0 - jnp.arange(RET_HEADS, dtype=F32)))
    idx = jnp.arange(n, dtype=F32)
    diff = idx[:, None] - idx[None, :]
    decay = jnp.where(diff[None] >= 0, jnp.exp(jnp.maximum(diff, 0.0)[None] * lg[:, None, None]), 0.0)
    qdec = jnp.exp((idx + 1.0)[None, :] * lg[:, None])
    kdec = jnp.exp((n - 1.0 - idx)[None, :] * lg[:, None])
    gl = jnp.exp(n * lg)
    bcast = lambda a: jnp.broadcast_to(a[:, :, None], (RET_HEADS, n, LANES))
    decay2 = jnp.concatenate([decay[0::2], decay[1::2]], axis=-1)
    return decay2, bcast(qdec), bcast(kdec), gl


def kernel(x_prompt, x_sample, c_prompt, c_sample, state_ret, cache_k, cache_v,
           norm_g, w_ada, b_ada, w_in, sink, w_out, final_g):
    batch, seq, _ = x_prompt.shape
    dbatch, dseq, _ = x_sample.shape

    c_all = jnp.concatenate([c_prompt, c_sample], axis=0)
    mod = _adaln(c_all, w_ada, b_ada).reshape(DEPTH, batch + dbatch, 1, 3 * D_MODEL)
    w_in_b = w_in.astype(BF16)
    w_out_b = w_out.astype(BF16)

    tabs_p = _rope_tables(0, seq)
    tabs_s = _rope_tables(PAST_LEN, dseq)
    dec_p = _decay_tables(PROMPT_BLOCK)
    dec_s = _decay_tables(dseq)
    ck4 = cache_k.reshape(DEPTH, dbatch, WINDOW, KV_WIDTH)
    cv4 = cache_v.reshape(DEPTH, dbatch, WINDOW, KV_WIDTH)
    fg = final_g.reshape(1, D_MODEL)
    g_all = norm_g.reshape(DEPTH, 1, D_MODEL)

    xp = x_prompt.reshape(batch * seq, D_MODEL)
    xs = x_sample.reshape(dbatch * dseq, D_MODEL)
    def state_acc(nb):
        return (jax.ShapeDtypeStruct((DEPTH, nb, RET_HEADS, RET_DK, RET_DV), F32),
                jax.ShapeDtypeStruct((DEPTH, nb, WINDOW, KV_WIDTH), F32),
                jax.ShapeDtypeStruct((DEPTH, nb, WINDOW, KV_WIDTH), F32))

    acc_p, acc_s = state_acc(batch), state_acc(dbatch)
    for l in range(DEPTH):
        last = l == DEPTH - 1
        xp, *acc_p = _prompt_layer(l, last, xp, mod, g_all, w_in_b, w_out_b, tabs_p, dec_p,
                                   sink, fg, acc_p, batch, seq)
        xs, *acc_s = _sample_layer(l, last, xs, mod, g_all, w_in_b, w_out_b, tabs_s, dec_s,
                                   sink, fg, state_ret, ck4, cv4, acc_s, dbatch, dseq, mod_row0=batch)

    kv_shape_p = (DEPTH, batch, WINDOW, ATT_KV_HEADS, ATT_HEAD_DIM)
    kv_shape_s = (DEPTH, dbatch, WINDOW, ATT_KV_HEADS, ATT_HEAD_DIM)
    return (xp.reshape(batch, seq, D_MODEL), xs.reshape(dbatch, dseq, D_MODEL),
            acc_p[0], acc_p[1].reshape(kv_shape_p), acc_p[2].reshape(kv_shape_p),
            acc_s[0], acc_s[1].reshape(kv_shape_s), acc_s[2].reshape(kv_shape_s))
```

```python
import functools
from typing import NamedTuple

import jax
import jax.numpy as jnp
from jax import lax
from jax.experimental import pallas as pl
from jax.experimental.pallas import tpu as pltpu

D_MODEL = 1024
DEPTH = 4
CHUNK = 64
PAST_LEN = 4096
RET_HEADS = 4
RET_DK = 128
RET_DV = 128
RET_WIDTH = RET_HEADS * RET_DV
ATT_HEAD_DIM = 64
ATT_Q_HEADS = 8
ATT_KV_HEADS = 2
ATT_WIDTH = ATT_Q_HEADS * ATT_HEAD_DIM
KV_WIDTH = ATT_KV_HEADS * ATT_HEAD_DIM
WINDOW = 128
MIX_WIDTH = RET_WIDTH + ATT_WIDTH
ROPE_BASE = 10000.0
NORM_EPS = 1e-6
GN_EPS = 1e-5

OFF_RQ = 0
OFF_RK = OFF_RQ + RET_HEADS * RET_DK
OFF_RV = OFF_RK + RET_HEADS * RET_DK
OFF_RG = OFF_RV + RET_WIDTH
OFF_AQ = OFF_RG + RET_WIDTH
OFF_AK = OFF_AQ + ATT_WIDTH
OFF_AV = OFF_AK + KV_WIDTH
OFF_AG = OFF_AV + KV_WIDTH
IN_WIDTH = OFF_AG + ATT_WIDTH

LANES = 128
ATT_SLABS = ATT_WIDTH // LANES
MASKED = -1e30
LOG2E = 1.4426950408889634
VMEM_LIMIT_BYTES = 56 * 1024 * 1024
NORM_ROWS = 32
PROJ_COLS = 512

PROMPT_ROWS = 256
PROMPT_BLOCK = 128
PROMPT_STAGES = 4
SAMPLE_SEQS = 4
SAMPLE_KEYS = 2 * LANES

BF16 = jnp.bfloat16
F32 = jnp.float32


def _silu(g):
    h = 0.5 * g
    return h + h * jnp.tanh(h)


def _dot(a, b):
    return jnp.dot(a, b, preferred_element_type=F32)


def _dot_nt(a, b):
    return lax.dot_general(a, b, (((1,), (1,)), ((), ())), preferred_element_type=F32)


def _low_lanes():
    lane = lax.broadcasted_iota(jnp.int32, (1, LANES), 1)
    return lane < ATT_HEAD_DIM


def _adaln_kernel(c_ref, w_ref, b_ref, o_ref):
    s = _silu(c_ref[...]).astype(BF16)
    o_ref[0] = _dot(s, w_ref[0].astype(BF16)) + b_ref[0]


def _adaln(c_all, w_ada, b_ada):
    n = c_all.shape[0]
    tn = D_MODEL
    return pl.pallas_call(
        _adaln_kernel,
        grid=(DEPTH, 3 * D_MODEL // tn),
        in_specs=[
            pl.BlockSpec((n, D_MODEL), lambda l, j: (0, 0)),
            pl.BlockSpec((1, D_MODEL, tn), lambda l, j: (l, 0, j)),
            pl.BlockSpec((1, 1, tn), lambda l, j: (l, 0, j)),
        ],
        out_specs=pl.BlockSpec((1, n, tn), lambda l, j: (l, 0, j)),
        out_shape=jax.ShapeDtypeStruct((DEPTH, n, 3 * D_MODEL), F32),
        compiler_params=pltpu.CompilerParams(
            dimension_semantics=("arbitrary", "arbitrary"),
            vmem_limit_bytes=VMEM_LIMIT_BYTES),
        name="adaln",
    )(c_all, w_ada, b_ada.reshape(DEPTH, 1, 3 * D_MODEL))


def _emit(items):
    for it in items:
        it()


def _interleave(a, b):
    if not a:
        return list(b)
    out, nb = [], 0
    for i, it in enumerate(a):
        out.append(it)
        want = (i + 1) * len(b) // len(a)
        out.extend(b[nb:want])
        nb = want
    return out


def _norm_items(x_ref, h_ref, r0, n, g_ref, mod_ref, s):
    def item(a):
        mod_row = mod_ref[s]
        shift = mod_row[:, 0:D_MODEL]
        gain = g_ref[...] * (1.0 + mod_row[:, D_MODEL:2 * D_MODEL])
        xb = x_ref[a:a + NORM_ROWS, :]
        ms = jnp.mean(xb * xb, axis=-1, keepdims=True)
        h_ref[a:a + NORM_ROWS, :] = (xb * lax.rsqrt(ms + NORM_EPS) * gain + shift).astype(BF16)
    return [functools.partial(item, a) for a in range(r0, r0 + n, NORM_ROWS)]


def _in_proj_items(h_ref, win_ref, z_ref):
    def item(c0):
        c1 = min(c0 + PROJ_COLS, IN_WIDTH)
        z_ref[:, c0:c1] = _dot(h_ref[...], win_ref[:, c0:c1])
    return [functools.partial(item, c0) for c0 in range(0, IN_WIDTH, PROJ_COLS)]


def _fill_kv(kv_bufs, dst0, n, k, v):
    ka_ref, kb_ref, kc_ref, kd_ref, va_ref, vb_ref, vc_ref, vd_ref = kv_bufs
    low = _low_lanes()
    kr = pltpu.roll(k, ATT_HEAD_DIM, 1)
    vr = pltpu.roll(v, ATT_HEAD_DIM, 1)
    d = slice(dst0, dst0 + n)
    ka_ref[d, :] = jnp.where(low, k, 0.0).astype(BF16)
    kb_ref[d, :] = jnp.where(low, 0.0, kr).astype(BF16)
    kc_ref[d, :] = jnp.where(low, kr, 0.0).astype(BF16)
    kd_ref[d, :] = jnp.where(low, 0.0, k).astype(BF16)
    va_ref[d, 0:LANES] = jnp.where(low, v, 0.0).astype(BF16)
    vb_ref[d, 0:LANES] = jnp.where(low, 0.0, vr).astype(BF16)
    vc_ref[d, 0:LANES] = jnp.where(low, vr, 0.0).astype(BF16)
    vd_ref[d, 0:LANES] = jnp.where(low, 0.0, v).astype(BF16)


def _init_kv_ones(kv_bufs):
    low = _low_lanes()
    for i, ref in enumerate(kv_bufs[4:]):
        ones = jnp.where(low, 1.0, 0.0) if i % 2 == 0 else jnp.where(low, 0.0, 1.0)
        ref[:, LANES:] = jnp.broadcast_to(ones, (ref.shape[0], LANES)).astype(BF16)


class _MixRefs(NamedTuple):
    z: object
    mix: object
    tabs: tuple
    dec: tuple
    sink: object
    kv_bufs: tuple


class _ValueStash:
    def __init__(self):
        self.d = {}

    def put(self, key, *vals):
        self.d[key] = vals

    def get(self, key):
        return self.d[key]


class _RefStash:
    def __init__(self, refs, j, blk):
        self.refs, self.j, self.blk = refs, j, blk
        self.rws = slice(j * blk, (j + 1) * blk)

    def _slots(self, key):
        lhs_ref, kdt_ref, v_ref, gate_ref, p_ref, e_ref = self.refs
        kind, i = key
        rws, j, blk = self.rws, self.j, self.blk
        if kind == "ret":
            w = lhs_ref.shape[1] // 2
            r0 = (2 * j + i) * RET_DK
            return [(lhs_ref, rws, slice(i * w, (i + 1) * w)), (kdt_ref, slice(r0, r0 + RET_DK), slice(None)),
                    (v_ref, rws, slice(2 * i * RET_DV, 2 * (i + 1) * RET_DV))]
        if kind == "gate":
            return [(gate_ref, rws, slice(None))]
        r0 = (2 * j + i) * 2 * blk
        return [(p_ref, slice(r0, r0 + 2 * blk), slice(None)),
                (e_ref, rws, slice(2 * i * LANES, (2 * i + 1) * LANES)),
                (e_ref, rws, slice((2 * i + 1) * LANES, (2 * i + 2) * LANES))]

    def put(self, key, *vals):
        for (ref, r, c), v in zip(self._slots(key), vals):
            ref[r, c] = v

    def get(self, key):
        return tuple(ref[r, c] for ref, r, c in self._slots(key))


def _block_diag(a, b):
    za = jnp.zeros(a.shape, a.dtype)
    return jnp.concatenate([jnp.concatenate([a, za], axis=1), jnp.concatenate([za, b], axis=1)], axis=0)


def _front_items(m: _MixRefs, layer, r0, blk, trow0, win0, lk, valid_fn, stash):
    z_ref = m.z
    cq_ref, sq_ref, ck_ref, sk_ref = m.tabs
    decay_ref, qdec_ref, kdec_ref, _ = m.dec
    ka_ref, kb_ref, kc_ref, kd_ref = m.kv_bufs[:4]
    rws = slice(r0, r0 + blk)
    trs = slice(trow0, trow0 + blk)
    win = slice(win0, win0 + lk)

    def ret_front(hp):
        cq, sq, ck, sk = cq_ref[trs, :], sq_ref[trs, :], ck_ref[trs, :], sk_ref[trs, :]
        qs, ks, vs, qds, kds = [], [], [], [], []
        for hd in (2 * hp, 2 * hp + 1):
            c = hd * RET_DK
            q = z_ref[rws, OFF_RQ + c:OFF_RQ + c + RET_DK]
            k = z_ref[rws, OFF_RK + c:OFF_RK + c + RET_DK]
            q = q * cq + pltpu.roll(q, RET_DK // 2, 1) * sq
            k = k * ck + pltpu.roll(k, RET_DK // 2, 1) * sk
            qs.append(q.astype(BF16))
            ks.append(k.astype(BF16))
            vs.append(z_ref[rws, OFF_RV + c:OFF_RV + c + RET_DV].astype(BF16))
            qds.append((q * qdec_ref[hd]).astype(BF16))
            kds.append(k * kdec_ref[hd])
        s2 = _dot_nt(jnp.concatenate(qs, axis=1), _block_diag(*ks)) * decay_ref[hp]
        lhs = jnp.concatenate([s2.astype(BF16)] + qds, axis=1)
        kd_t = jnp.concatenate(kds, axis=0).T.astype(BF16)
        stash.put(("ret", hp), lhs, kd_t, jnp.concatenate(vs, axis=1))

    def gates():
        stash.put(("gate", 0), jnp.concatenate([_silu(z_ref[rws, OFF_RG:OFF_RG + RET_WIDTH]),
                                                _silu(z_ref[rws, OFF_AG:OFF_AG + ATT_WIDTH])], axis=1))

    def att_front(kv):
        low = _low_lanes()
        valid = None if valid_fn is None else valid_fn()
        slabs = (2 * kv, 2 * kv + 1)
        k_lo, k_hi = (ka_ref, kb_ref) if kv == 0 else (kc_ref, kd_ref)
        qf = jnp.concatenate([z_ref[rws, OFF_AQ + sl * LANES:OFF_AQ + (sl + 1) * LANES] for sl in slabs], axis=0)
        qf = (qf * (ATT_HEAD_DIM ** -0.5 * LOG2E)).astype(BF16)
        keys = jnp.concatenate([k_lo[win, :], k_hi[win, :]], axis=0)
        s = _dot_nt(qf, keys)
        ps, es = [], []
        for i, sl in enumerate(slabs):
            row_p, row_e = [], []
            for h in range(2):
                sh = s[i * blk:(i + 1) * blk, h * lk:(h + 1) * lk]
                if valid is not None:
                    sh = jnp.where(valid, sh, MASKED)
                sink = m.sink[layer, 2 * sl + h] * LOG2E
                mh = jnp.maximum(jnp.max(sh, axis=-1, keepdims=True), sink)
                row_p.append(jnp.exp2(sh - mh).astype(BF16))
                row_e.append(jnp.exp2(sink - mh))
            ps.append(jnp.concatenate(row_p, axis=1))
            es.append(jnp.broadcast_to(jnp.where(low, row_e[0], row_e[1]), (blk, LANES)))
        stash.put(("att", kv), jnp.concatenate(ps, axis=0), *es)

    return ([functools.partial(ret_front, hp) for hp in range(RET_HEADS // 2)] + [gates]
            + [functools.partial(att_front, kv) for kv in range(ATT_KV_HEADS)])


def _back_items(m: _MixRefs, r0, blk, win0, lk, r_get, r_set, stash):
    mix_ref = m.mix
    gl_ref = m.dec[3]
    va_ref, vb_ref, vc_ref, vd_ref = m.kv_bufs[4:]
    rws = slice(r0, r0 + blk)
    win = slice(win0, win0 + lk)

    def ret_back(hp):
        heads = (2 * hp, 2 * hp + 1)
        lhs, kd_t, v2 = stash.get(("ret", hp))
        gate = stash.get(("gate", 0))[0]
        r_old = [r_get(hd) for hd in heads]
        v_bd = _block_diag(v2[:, :RET_DV], v2[:, RET_DV:])
        r_bd = _block_diag(*[r.astype(BF16) for r in r_old])
        o2 = _dot(lhs, jnp.concatenate([v_bd, r_bd], axis=0))
        r_inc = _dot(kd_t, v_bd)
        for i, hd in enumerate(heads):
            c = hd * RET_DK
            r_set(hd, gl_ref[hd] * r_old[i] + r_inc[:, i * RET_DV:(i + 1) * RET_DV])
            o = o2[:, i * RET_DV:(i + 1) * RET_DV]
            mu = jnp.mean(o, axis=-1, keepdims=True)
            oc = o - mu
            var = jnp.mean(oc * oc, axis=-1, keepdims=True)
            on = oc * lax.rsqrt(var + GN_EPS)
            mix_ref[rws, c:c + RET_DV] = (on * gate[:, c:c + RET_DV]).astype(BF16)

    def att_back(kv):
        v_lo, v_hi = (va_ref, vb_ref) if kv == 0 else (vc_ref, vd_ref)
        p, *es = stash.get(("att", kv))
        gate = stash.get(("gate", 0))[0]
        vals = jnp.concatenate([v_lo[win, :], v_hi[win, :]], axis=0)
        acc = _dot(p, vals)
        for i, sl in enumerate((2 * kv, 2 * kv + 1)):
            c = RET_WIDTH + sl * LANES
            a = acc[i * blk:(i + 1) * blk, :]
            den = a[:, LANES:] + es[i]
            mix_ref[rws, c:c + LANES] = (a[:, :LANES] / den * gate[:, c:c + LANES]).astype(BF16)

    return ([functools.partial(ret_back, hp) for hp in range(RET_HEADS // 2)]
            + [functools.partial(att_back, kv) for kv in range(ATT_KV_HEADS)])


def _final_norm_rows(xo_ref, r0, n, fg_row):
    for a in range(r0, r0 + n, NORM_ROWS):
        xn = xo_ref[a:a + NORM_ROWS, :]
        ms = jnp.mean(xn * xn, axis=-1, keepdims=True)
        xo_ref[a:a + NORM_ROWS, :] = xn * lax.rsqrt(ms + NORM_EPS) * fg_row


class _PromptCfg(NamedTuple):
    rows: int
    nt: int
    ntiles: int
    last: bool
    layer: int
    first: bool


N_STASH = 6
N_KV = 8


def _prompt_kernel(xn_ref, xc_ref, modn_ref, modc_ref, g_ref, win_ref, wout_ref,
                   cq_ref, sq_ref, ck_ref, sk_ref, decay_ref, qdec_ref, kdec_ref, gl_ref, sink_ref, fg_ref,
                   _ro_prev, _ko_prev, _vo_prev,
                   xo_ref, ro_ref, ko_ref, vo_ref,
                   za_ref, zb_ref, h_ref, mixa_ref, mixb_ref, r_scr, *bufs,
                   cfg: _PromptCfg):
    rows, nt, ntiles = cfg.rows, cfg.nt, cfg.ntiles
    blk, lk = PROMPT_BLOCK, WINDOW + PROMPT_BLOCK
    stash_refs = (bufs[0:N_STASH], bufs[N_STASH:2 * N_STASH])
    kv_refs = (bufs[2 * N_STASH:2 * N_STASH + N_KV], bufs[2 * N_STASH + N_KV:2 * N_STASH + 2 * N_KV])
    z_refs = (za_ref, zb_ref)
    mix_refs = (mixa_ref, mixb_ref)
    g = pl.program_id(0)
    t = jnp.clip(g - 1, 0, ntiles - 1) % nt
    t3 = jnp.clip(g - 2, 0, ntiles - 1) % nt

    @pl.when(g == 0)
    def _():
        zb_ref[...] = jnp.zeros(zb_ref.shape, F32)
        mixb_ref[...] = jnp.zeros(mixb_ref.shape, BF16)
        for b in stash_refs[0] + kv_refs[0]:
            b[...] = jnp.zeros(b.shape, b.dtype)
        stash_refs[0][-1][...] = jnp.ones(stash_refs[0][-1].shape, F32)
        _init_kv_ones(kv_refs[0])
        _init_kv_ones(kv_refs[1])

    @pl.when(t3 == 0)
    def _():
        r_scr[...] = jnp.zeros(r_scr.shape, F32)

    (ro_ref, ko_ref, vo_ref), clear_other_layers = _own_layer((ro_ref, ko_ref, vo_ref), cfg.layer, cfg.first)

    @pl.when((t == 0) | (t3 == 0))
    def _():
        clear_other_layers()

    def r_set(hd, val):
        r_scr[hd] = val

    def visible(j):
        col_i = lax.broadcasted_iota(jnp.int32, (blk, lk), 1)
        ok = None
        if blk == 2 * CHUNK:
            row_i = lax.broadcasted_iota(jnp.int32, (blk, lk), 0)
            first_key = jnp.where(row_i < CHUNK, 0, CHUNK)
            ok = (col_i >= first_key) & (col_i < first_key + WINDOW + CHUNK)
        if j * blk < WINDOW:
            started = col_i >= jnp.where(t == 0, WINDOW - j * blk, 0)
            ok = started if ok is None else ok & started
        return ok

    def step(p):
        a, b = p, 1 - p
        tabs = (cq_ref, sq_ref, ck_ref, sk_ref)
        dec = (decay_ref, qdec_ref, kdec_ref, gl_ref)

        norm = _norm_items(xn_ref, h_ref, 0, rows, g_ref, modn_ref, 0)
        proj = _in_proj_items(h_ref, win_ref, z_refs[a])

        def out_item(c0):
            cs = slice(c0, c0 + PROJ_COLS)
            gate_row = modc_ref[0][:, 2 * D_MODEL + c0:2 * D_MODEL + c0 + PROJ_COLS]
            xo_ref[:, cs] = xc_ref[:, cs] + gate_row * _dot(mix_refs[b][...], wout_ref[:, cs])
        outp = [functools.partial(out_item, c0) for c0 in range(0, D_MODEL, PROJ_COLS)]

        zr_ref = z_refs[b]
        for dst, src in zip(kv_refs[b], kv_refs[a]):
            dst[0:WINDOW, :] = src[rows:rows + WINDOW, :]
        for r0 in range(0, rows, CHUNK):
            _fill_kv(kv_refs[b], WINDOW + r0, CHUNK, zr_ref[r0:r0 + CHUNK, OFF_AK:OFF_AK + KV_WIDTH],
                     zr_ref[r0:r0 + CHUNK, OFF_AV:OFF_AV + KV_WIDTH])
        front = []
        for j in range(rows // blk):
            front += _front_items(_MixRefs(zr_ref, None, tabs, dec, sink_ref, kv_refs[b]), cfg.layer,
                                  j * blk, blk, j * blk, j * blk, lk, functools.partial(visible, j),
                                  _RefStash(stash_refs[b], j, blk))

        back = []
        for j in range(rows // blk):
            back += _back_items(_MixRefs(None, mix_refs[a], tabs, dec, sink_ref, kv_refs[a]),
                                j * blk, blk, j * blk, lk, lambda hd: r_scr[hd], r_set,
                                _RefStash(stash_refs[a], j, blk))

        _emit(_interleave(norm, outp))
        _emit(_interleave(_interleave(front, back), proj))
        if cfg.last:
            _final_norm_rows(xo_ref, 0, rows, fg_ref[...])

        @pl.when(t == nt - 1)
        def _():
            ko_ref[...] = zr_ref[rows - WINDOW:rows, OFF_AK:OFF_AK + KV_WIDTH]
            vo_ref[...] = zr_ref[rows - WINDOW:rows, OFF_AV:OFF_AV + KV_WIDTH]

    for parity in (0, 1):
        @pl.when(g % 2 == parity)
        def _():
            step(parity)

    @pl.when((t3 == nt - 1) & (g <= ntiles + 1))
    def _():
        ro_ref[...] = r_scr[...]


def _stash_scratch(rows, blk, lk):
    nblk = rows // blk
    return [
        pltpu.VMEM((rows, 2 * (2 * blk + 2 * RET_DK)), BF16),
        pltpu.VMEM((nblk * 2 * RET_DK, 2 * blk), BF16),
        pltpu.VMEM((rows, RET_WIDTH), BF16),
        pltpu.VMEM((rows, MIX_WIDTH), F32),
        pltpu.VMEM((nblk * 2 * 2 * blk, 2 * lk), BF16),
        pltpu.VMEM((rows, ATT_WIDTH), F32),
    ]


def _kv_scratch(krows):
    return ([pltpu.VMEM((krows, LANES), BF16) for _ in range(4)]
            + [pltpu.VMEM((krows, 2 * LANES), BF16) for _ in range(4)])


def _const_spec(shape):
    nd = len(shape)
    return pl.BlockSpec(shape, lambda g: (0,) * nd)


def _layer_spec(shape, layer):
    nd = len(shape)
    return pl.BlockSpec((None, *shape), lambda g: (layer,) + (0,) * nd)


def _smem_spec():
    return pl.BlockSpec(memory_space=pltpu.SMEM)


N_STATE = 3


def _is_first(acc):
    return all(isinstance(a, jax.ShapeDtypeStruct) for a in acc)


def _own_layer(state_refs, layer, first):
    if not first:
        return state_refs, lambda: None

    def clear():
        for ref in state_refs:
            for l in range(DEPTH):
                if l != layer:
                    ref[l] = jnp.zeros(ref.shape[1:], ref.dtype)
    return [ref.at[layer] for ref in state_refs], clear


def _state_alias(kernel, n_in, acc):
    out_shape = [jax.ShapeDtypeStruct(a.shape, a.dtype) for a in acc]
    if _is_first(acc):
        def first(*refs, **kw):
            return kernel(*refs[:n_in], *([None] * N_STATE), *refs[n_in:], **kw)
        return first, [], [], {}, out_shape
    in_specs = [pl.BlockSpec(memory_space=pl.ANY)] * N_STATE
    aliases = {n_in + i: 1 + i for i in range(N_STATE)}
    return kernel, list(acc), in_specs, aliases, out_shape


def _prompt_layer(layer, last, x2d, mod, g, w_in, w_out, tabs, dec, sink, final_g, acc, batch, seq):
    rows = PROMPT_ROWS
    blk, lk = PROMPT_BLOCK, WINDOW + PROMPT_BLOCK
    nt = seq // rows
    ntiles = batch * nt
    first = _is_first(acc)
    cfg = _PromptCfg(rows=rows, nt=nt, ntiles=ntiles, last=last, layer=layer, first=first)
    cq, sq, ck, sk = tabs
    decay, qdec, kdec, gl = dec

    nxt = lambda g: jnp.minimum(g, ntiles - 1)
    cur = lambda g: jnp.clip(g - 1, 0, ntiles - 1)
    bak = lambda g: jnp.clip(g - 2, 0, ntiles - 1)
    fin = lambda g: jnp.maximum(g - 3, 0)
    tab_spec = pl.BlockSpec((rows, LANES), lambda g: (cur(g) % nt, 0))
    in_specs = [
        pl.BlockSpec((rows, D_MODEL), lambda g: (nxt(g), 0)),
        pl.BlockSpec((rows, D_MODEL), lambda g: (fin(g), 0)),
        pl.BlockSpec((None, 1, 1, 3 * D_MODEL), lambda g: (layer, nxt(g) // nt, 0, 0)),
        pl.BlockSpec((None, 1, 1, 3 * D_MODEL), lambda g: (layer, fin(g) // nt, 0, 0)),
        _layer_spec((1, D_MODEL), layer),
        _layer_spec((D_MODEL, IN_WIDTH), layer),
        _layer_spec((MIX_WIDTH, D_MODEL), layer),
        tab_spec, tab_spec, tab_spec, tab_spec,
        _const_spec(decay.shape), _const_spec(qdec.shape), _const_spec(kdec.shape),
        _smem_spec(), _smem_spec(),
        _const_spec((1, D_MODEL)),
    ]
    body, acc_args, acc_specs, aliases, acc_shape = _state_alias(_prompt_kernel, len(in_specs), acc)
    lyr, lidx = (DEPTH, 0) if first else (None, layer)
    out_specs = [
        pl.BlockSpec((rows, D_MODEL), lambda g: (fin(g), 0)),
        pl.BlockSpec((lyr, None, RET_HEADS, RET_DK, RET_DV), lambda g: (lidx, bak(g) // nt, 0, 0, 0)),
        pl.BlockSpec((lyr, None, WINDOW, KV_WIDTH), lambda g: (lidx, cur(g) // nt, 0, 0)),
        pl.BlockSpec((lyr, None, WINDOW, KV_WIDTH), lambda g: (lidx, cur(g) // nt, 0, 0)),
    ]
    out_shape = [jax.ShapeDtypeStruct((batch * seq, D_MODEL), F32)] + acc_shape
    scratch = [
        pltpu.VMEM((rows, IN_WIDTH), F32),
        pltpu.VMEM((rows, IN_WIDTH), F32),
        pltpu.VMEM((rows, D_MODEL), BF16),
        pltpu.VMEM((rows, MIX_WIDTH), BF16),
        pltpu.VMEM((rows, MIX_WIDTH), BF16),
        pltpu.VMEM((RET_HEADS, RET_DK, RET_DV), F32),
        *_stash_scratch(rows, blk, lk), *_stash_scratch(rows, blk, lk),
        *_kv_scratch(WINDOW + rows), *_kv_scratch(WINDOW + rows),
    ]
    return pl.pallas_call(
        functools.partial(body, cfg=cfg),
        grid=(ntiles + PROMPT_STAGES - 1,),
        in_specs=in_specs + acc_specs, out_specs=out_specs, out_shape=out_shape,
        input_output_aliases=aliases,
        scratch_shapes=scratch,
        compiler_params=pltpu.CompilerParams(
            dimension_semantics=("arbitrary",),
            vmem_limit_bytes=VMEM_LIMIT_BYTES),
        name=f"prompt_layer{layer}",
    )(x2d, x2d, mod, mod, g, w_in, w_out, cq, sq, ck, sk, decay, qdec, kdec, gl, sink, final_g, *acc_args)


class _SampleCfg(NamedTuple):
    nseq: int
    seq: int
    last: bool
    layer: int
    first: bool


def _sample_kernel(x_ref, mod_ref, g_ref, win_ref, wout_ref,
                   cq_ref, sq_ref, ck_ref, sk_ref, decay_ref, qdec_ref, kdec_ref, gl_ref, sink_ref, fg_ref,
                   st_ref, cachek_ref, cachev_ref,
                   _ro_prev, _ko_prev, _vo_prev,
                   xo_ref, ro_ref, ko_ref, vo_ref,
                   z_ref, h_ref, mix_ref, *kv_bufs,
                   cfg: _SampleCfg):
    nseq, seq = cfg.nseq, cfg.seq
    lk = SAMPLE_KEYS
    pad = lk - WINDOW - seq

    @pl.when(pl.program_id(0) == 0)
    def _():
        _init_kv_ones(kv_bufs)

    (ro_ref, ko_ref, vo_ref), clear_other_layers = _own_layer((ro_ref, ko_ref, vo_ref), cfg.layer, cfg.first)
    clear_other_layers()

    for s in range(nseq):
        _emit(_norm_items(x_ref, h_ref, s * seq, seq, g_ref, mod_ref, s))
    _emit(_in_proj_items(h_ref, win_ref, z_ref))

    def visible():
        return lax.broadcasted_iota(jnp.int32, (seq, lk), 1) >= pad

    m = _MixRefs(z_ref, mix_ref, (cq_ref, sq_ref, ck_ref, sk_ref),
                 (decay_ref, qdec_ref, kdec_ref, gl_ref), sink_ref, kv_bufs)
    for s in range(nseq):
        rws = slice(s * seq, (s + 1) * seq)
        k_new = z_ref[rws, OFF_AK:OFF_AK + KV_WIDTH]
        v_new = z_ref[rws, OFF_AV:OFF_AV + KV_WIDTH]
        for b in kv_bufs:
            b[s * lk:s * lk + pad, 0:LANES] = jnp.zeros((pad, LANES), BF16)
        _fill_kv(kv_bufs, s * lk + pad, WINDOW, cachek_ref[s], cachev_ref[s])
        _fill_kv(kv_bufs, s * lk + pad + WINDOW, seq, k_new, v_new)

        def r_set(hd, val, s=s):
            ro_ref[s, hd] = val

        stash = _ValueStash()
        _emit(_front_items(m, cfg.layer, s * seq, seq, 0, s * lk, lk, visible, stash))
        _emit(_back_items(m, s * seq, seq, s * lk, lk, lambda hd, s=s: st_ref[s, hd], r_set, stash))
        ko_ref[s, 0:WINDOW - seq, :] = cachek_ref[s, seq:WINDOW, :]
        vo_ref[s, 0:WINDOW - seq, :] = cachev_ref[s, seq:WINDOW, :]
        ko_ref[s, WINDOW - seq:WINDOW, :] = k_new
        vo_ref[s, WINDOW - seq:WINDOW, :] = v_new

    xo_ref[...] = _dot(mix_ref[...], wout_ref[...])
    for s in range(nseq):
        rws = slice(s * seq, (s + 1) * seq)
        gate_row = mod_ref[s][:, 2 * D_MODEL:3 * D_MODEL]
        xo_ref[rws, :] = x_ref[rws, :] + gate_row * xo_ref[rws, :]
    if cfg.last:
        _final_norm_rows(xo_ref, 0, nseq * seq, fg_ref[...])


def _sample_layer(layer, last, x2d, mod, g, w_in, w_out, tabs, dec, sink, final_g,
                  state, cache_k, cache_v, acc, batch, seq, mod_row0):
    nseq = SAMPLE_SEQS
    rows = nseq * seq
    first = _is_first(acc)
    cfg = _SampleCfg(nseq=nseq, seq=seq, last=last, layer=layer, first=first)
    cq, sq, ck, sk = tabs
    decay, qdec, kdec, gl = dec
    mod_blk0 = mod_row0 // nseq
    in_specs = [
        pl.BlockSpec((rows, D_MODEL), lambda i: (i, 0)),
        pl.BlockSpec((None, nseq, 1, 3 * D_MODEL), lambda i: (layer, mod_blk0 + i, 0, 0)),
        _layer_spec((1, D_MODEL), layer),
        _layer_spec((D_MODEL, IN_WIDTH), layer),
        _layer_spec((MIX_WIDTH, D_MODEL), layer),
        _const_spec(cq.shape), _const_spec(sq.shape), _const_spec(ck.shape), _const_spec(sk.shape),
        _const_spec(decay.shape), _const_spec(qdec.shape), _const_spec(kdec.shape),
        _smem_spec(), _smem_spec(),
        _const_spec((1, D_MODEL)),
        pl.BlockSpec((None, nseq, RET_HEADS, RET_DK, RET_DV), lambda i: (layer, i, 0, 0, 0)),
        pl.BlockSpec((None, nseq, WINDOW, KV_WIDTH), lambda i: (layer, i, 0, 0)),
        pl.BlockSpec((None, nseq, WINDOW, KV_WIDTH), lambda i: (layer, i, 0, 0)),
    ]
    body, acc_args, acc_specs, aliases, acc_shape = _state_alias(_sample_kernel, len(in_specs), acc)
    lyr, lidx = (DEPTH, 0) if first else (None, layer)
    out_specs = [
        pl.BlockSpec((rows, D_MODEL), lambda i: (i, 0)),
        pl.BlockSpec((lyr, nseq, RET_HEADS, RET_DK, RET_DV), lambda i: (lidx, i, 0, 0, 0)),
        pl.BlockSpec((lyr, nseq, WINDOW, KV_WIDTH), lambda i: (lidx, i, 0, 0)),
        pl.BlockSpec((lyr, nseq, WINDOW, KV_WIDTH), lambda i: (lidx, i, 0, 0)),
    ]
    out_shape = [jax.ShapeDtypeStruct((batch * seq, D_MODEL), F32)] + acc_shape
    scratch = [
        pltpu.VMEM((rows, IN_WIDTH), F32),
        pltpu.VMEM((rows, D_MODEL), BF16),
        pltpu.VMEM((rows, MIX_WIDTH), BF16),
        *_kv_scratch(nseq * SAMPLE_KEYS),
    ]
    return pl.pallas_call(
        functools.partial(body, cfg=cfg),
        grid=(batch // nseq,),
        in_specs=in_specs + acc_specs, out_specs=out_specs, out_shape=out_shape,
        input_output_aliases=aliases,
        scratch_shapes=scratch,
        compiler_params=pltpu.CompilerParams(
            dimension_semantics=("arbitrary",),
            vmem_limit_bytes=VMEM_LIMIT_BYTES),
        name=f"sample_layer{layer}",
    )(x2d, mod, g, w_in, w_out, cq, sq, ck, sk, decay, qdec, kdec, gl, sink, final_g,
      state, cache_k, cache_v, *acc_args)


def _rope_tables(start, n):
    d = RET_DK
    inv = 1.0 / (ROPE_BASE ** (jnp.arange(0, d, 2, dtype=F32) / d))
    ang_a = jnp.arange(start, start + n, CHUNK).astype(F32)[:, None] * inv[None, :]
    ang_b = jnp.arange(CHUNK).astype(F32)[:, None] * inv[None, :]
    ca, sa, cb, sb = lax.optimization_barrier((jnp.cos(ang_a), jnp.sin(ang_a), jnp.cos(ang_b), jnp.sin(ang_b)))
    cos = (ca[:, None, :] * cb[None] - sa[:, None, :] * sb[None]).reshape(n, d // 2)
    sin = (sa[:, None, :] * cb[None] + ca[:, None, :] * sb[None]).reshape(n, d // 2)
    cos2 = jnp.concatenate([cos, cos], axis=-1)
    sin2 = jnp.concatenate([-sin, sin], axis=-1)
    kscale = RET_DK ** -0.5
    return cos2, sin2, cos2 * kscale, sin2 * kscale


def _decay_tables(n):
    lg = jnp.log(1.0 - 2.0 ** (-5.0 - jnp.arange(RET_HEADS, dtype=F32)))
    idx = jnp.arange(n, dtype=F32)
    diff = idx[:, None] - idx[None, :]
    decay = jnp.where(diff[None] >= 0, jnp.exp(jnp.maximum(diff, 0.0)[None] * lg[:, None, None]), 0.0)
    qdec = jnp.exp((idx + 1.0)[None, :] * lg[:, None])
    kdec = jnp.exp((n - 1.0 - idx)[None, :] * lg[:, None])
    gl = jnp.exp(n * lg)
    bcast = lambda a: jnp.broadcast_to(a[:, :, None], (RET_HEADS, n, LANES))
    decay2 = jnp.concatenate([decay[0::2], decay[1::2]], axis=-1)
    return decay2, bcast(qdec), bcast(kdec), gl


def kernel(x_prompt, x_sample, c_prompt, c_sample, state_ret, cache_k, cache_v,
           norm_g, w_ada, b_ada, w_in, sink, w_out, final_g):
    batch, seq, _ = x_prompt.shape
    dbatch, dseq, _ = x_sample.shape

    c_all = jnp.concatenate([c_prompt, c_sample], axis=0)
    mod = _adaln(c_all, w_ada, b_ada).reshape(DEPTH, batch + dbatch, 1, 3 * D_MODEL)
    w_in_b = w_in.astype(BF16)
    w_out_b = w_out.astype(BF16)

    tabs_p = _rope_tables(0, seq)
    tabs_s = _rope_tables(PAST_LEN, dseq)
    dec_p = _decay_tables(PROMPT_BLOCK)
    dec_s = _decay_tables(dseq)
    ck4 = cache_k.reshape(DEPTH, dbatch, WINDOW, KV_WIDTH)
    cv4 = cache_v.reshape(DEPTH, dbatch, WINDOW, KV_WIDTH)
    fg = final_g.reshape(1, D_MODEL)
    g_all = norm_g.reshape(DEPTH, 1, D_MODEL)

    xp = x_prompt.reshape(batch * seq, D_MODEL)
    xs = x_sample.reshape(dbatch * dseq, D_MODEL)
    def state_acc(nb):
        return (jax.ShapeDtypeStruct((DEPTH, nb, RET_HEADS, RET_DK, RET_DV), F32),
                jax.ShapeDtypeStruct((DEPTH, nb, WINDOW, KV_WIDTH), F32),
                jax.ShapeDtypeStruct((DEPTH, nb, WINDOW, KV_WIDTH), F32))

    acc_p, acc_s = state_acc(batch), state_acc(dbatch)
    for l in range(DEPTH):
        last = l == DEPTH - 1
        xp, *acc_p = _prompt_layer(l, last, xp, mod, g_all, w_in_b, w_out_b, tabs_p, dec_p,
                                   sink, fg, acc_p, batch, seq)
        xs, *acc_s = _sample_layer(l, last, xs, mod, g_all, w_in_b, w_out_b, tabs_s, dec_s,
                                   sink, fg, state_ret, ck4, cv4, acc_s, dbatch, dseq, mod_row0=batch)

    kv_shape_p = (DEPTH, batch, WINDOW, ATT_KV_HEADS, ATT_HEAD_DIM)
    kv_shape_s = (DEPTH, dbatch, WINDOW, ATT_KV_HEADS, ATT_HEAD_DIM)
    return (xp.reshape(batch, seq, D_MODEL), xs.reshape(dbatch, dseq, D_MODEL),
            acc_p[0], acc_p[1].reshape(kv_shape_p), acc_p[2].reshape(kv_shape_p),
            acc_s[0], acc_s[1].reshape(kv_shape_s), acc_s[2].reshape(kv_shape_s))
```

```python
import functools
from typing import NamedTuple

import jax
import jax.numpy as jnp
from jax import lax
from jax.experimental import pallas as pl
from jax.experimental.pallas import tpu as pltpu

D_MODEL = 1024
DEPTH = 4
CHUNK = 64
PAST_LEN = 4096
RET_HEADS = 4
RET_DK = 128
RET_DV = 128
RET_WIDTH = RET_HEADS * RET_DV
ATT_HEAD_DIM = 64
ATT_Q_HEADS = 8
ATT_KV_HEADS = 2
ATT_WIDTH = ATT_Q_HEADS * ATT_HEAD_DIM
KV_WIDTH = ATT_KV_HEADS * ATT_HEAD_DIM
WINDOW = 128
MIX_WIDTH = RET_WIDTH + ATT_WIDTH
ROPE_BASE = 10000.0
NORM_EPS = 1e-6
GN_EPS = 1e-5

OFF_RQ = 0
OFF_RK = OFF_RQ + RET_HEADS * RET_DK
OFF_RV = OFF_RK + RET_HEADS * RET_DK
OFF_RG = OFF_RV + RET_WIDTH
OFF_AQ = OFF_RG + RET_WIDTH
OFF_AK = OFF_AQ + ATT_WIDTH
OFF_AV = OFF_AK + KV_WIDTH
OFF_AG = OFF_AV + KV_WIDTH
IN_WIDTH = OFF_AG + ATT_WIDTH

LANES = 128
ATT_SLABS = ATT_WIDTH // LANES
MASKED = -1e30
LOG2E = 1.4426950408889634
VMEM_LIMIT_BYTES = 56 * 1024 * 1024
NORM_ROWS = 32
PROJ_COLS = 512

PROMPT_ROWS = 256
PROMPT_BLOCK = 128
PROMPT_STAGES = 4
SAMPLE_SEQS = 4
SAMPLE_KEYS = 2 * LANES

BF16 = jnp.bfloat16
F32 = jnp.float32


def _silu(g):
    h = 0.5 * g
    return h + h * jnp.tanh(h)


def _dot(a, b):
    return jnp.dot(a, b, preferred_element_type=F32)


def _dot_nt(a, b):
    return lax.dot_general(a, b, (((1,), (1,)), ((), ())), preferred_element_type=F32)


def _low_lanes():
    lane = lax.broadcasted_iota(jnp.int32, (1, LANES), 1)
    return lane < ATT_HEAD_DIM


def _adaln_kernel(c_ref, w_ref, b_ref, o_ref):
    s = _silu(c_ref[...]).astype(BF16)
    o_ref[0] = _dot(s, w_ref[0].astype(BF16)) + b_ref[0]


def _adaln(c_all, w_ada, b_ada):
    n = c_all.shape[0]
    tn = D_MODEL
    return pl.pallas_call(
        _adaln_kernel,
        grid=(DEPTH, 3 * D_MODEL // tn),
        in_specs=[
            pl.BlockSpec((n, D_MODEL), lambda l, j: (0, 0)),
            pl.BlockSpec((1, D_MODEL, tn), lambda l, j: (l, 0, j)),
            pl.BlockSpec((1, 1, tn), lambda l, j: (l, 0, j)),
        ],
        out_specs=pl.BlockSpec((1, n, tn), lambda l, j: (l, 0, j)),
        out_shape=jax.ShapeDtypeStruct((DEPTH, n, 3 * D_MODEL), F32),
        compiler_params=pltpu.CompilerParams(
            dimension_semantics=("arbitrary", "arbitrary"),
            vmem_limit_bytes=VMEM_LIMIT_BYTES),
        name="adaln",
    )(c_all, w_ada, b_ada.reshape(DEPTH, 1, 3 * D_MODEL))


def _emit(items):
    for it in items:
        it()


def _interleave(a, b):
    if not a:
        return list(b)
    out, nb = [], 0
    for i, it in enumerate(a):
        out.append(it)
        want = (i + 1) * len(b) // len(a)
        out.extend(b[nb:want])
        nb = want
    return out


def _norm_items(x_ref, h_ref, r0, n, g_ref, mod_ref, s):
    def item(a):
        mod_row = mod_ref[s]
        shift = mod_row[:, 0:D_MODEL]
        gain = g_ref[...] * (1.0 + mod_row[:, D_MODEL:2 * D_MODEL])
        xb = x_ref[a:a + NORM_ROWS, :]
        ms = jnp.mean(xb * xb, axis=-1, keepdims=True)
        h_ref[a:a + NORM_ROWS, :] = (xb * lax.rsqrt(ms + NORM_EPS) * gain + shift).astype(BF16)
    return [functools.partial(item, a) for a in range(r0, r0 + n, NORM_ROWS)]


def _in_proj_items(h_ref, win_ref, z_ref):
    def item(c0):
        c1 = min(c0 + PROJ_COLS, IN_WIDTH)
        z_ref[:, c0:c1] = _dot(h_ref[...], win_ref[:, c0:c1])
    return [functools.partial(item, c0) for c0 in range(0, IN_WIDTH, PROJ_COLS)]


def _fill_kv(kv_bufs, dst0, n, k, v):
    ka_ref, kb_ref, kc_ref, kd_ref, va_ref, vb_ref, vc_ref, vd_ref = kv_bufs
    low = _low_lanes()
    kr = pltpu.roll(k, ATT_HEAD_DIM, 1)
    vr = pltpu.roll(v, ATT_HEAD_DIM, 1)
    d = slice(dst0, dst0 + n)
    ka_ref[d, :] = jnp.where(low, k, 0.0).astype(BF16)
    kb_ref[d, :] = jnp.where(low, 0.0, kr).astype(BF16)
    kc_ref[d, :] = jnp.where(low, kr, 0.0).astype(BF16)
    kd_ref[d, :] = jnp.where(low, 0.0, k).astype(BF16)
    va_ref[d, 0:LANES] = jnp.where(low, v, 0.0).astype(BF16)
    vb_ref[d, 0:LANES] = jnp.where(low, 0.0, vr).astype(BF16)
    vc_ref[d, 0:LANES] = jnp.where(low, vr, 0.0).astype(BF16)
    vd_ref[d, 0:LANES] = jnp.where(low, 0.0, v).astype(BF16)


def _init_kv_ones(kv_bufs):
    low = _low_lanes()
    for i, ref in enumerate(kv_bufs[4:]):
        ones = jnp.where(low, 1.0, 0.0) if i % 2 == 0 else jnp.where(low, 0.0, 1.0)
        ref[:, LANES:] = jnp.broadcast_to(ones, (ref.shape[0], LANES)).astype(BF16)


class _MixRefs(NamedTuple):
    z: object
    mix: object
    tabs: tuple
    dec: tuple
    sink: object
    kv_bufs: tuple


class _ValueStash:
    def __init__(self):
        self.d = {}

    def put(self, key, *vals):
        self.d[key] = vals

    def get(self, key):
        return self.d[key]


class _RefStash:
    def __init__(self, refs, j, blk):
        self.refs, self.j, self.blk = refs, j, blk
        self.rws = slice(j * blk, (j + 1) * blk)

    def _slots(self, key):
        lhs_ref, kdt_ref, v_ref, gate_ref, p_ref, e_ref = self.refs
        kind, i = key
        rws, j, blk = self.rws, self.j, self.blk
        if kind == "ret":
            w = lhs_ref.shape[1] // 2
            r0 = (2 * j + i) * RET_DK
            return [(lhs_ref, rws, slice(i * w, (i + 1) * w)), (kdt_ref, slice(r0, r0 + RET_DK), slice(None)),
                    (v_ref, rws, slice(2 * i * RET_DV, 2 * (i + 1) * RET_DV))]
        if kind == "gate":
            return [(gate_ref, rws, slice(None))]
        r0 = (2 * j + i) * 2 * blk
        return [(p_ref, slice(r0, r0 + 2 * blk), slice(None)),
                (e_ref, rws, slice(2 * i * LANES, (2 * i + 1) * LANES)),
                (e_ref, rws, slice((2 * i + 1) * LANES, (2 * i + 2) * LANES))]

    def put(self, key, *vals):
        for (ref, r, c), v in zip(self._slots(key), vals):
            ref[r, c] = v

    def get(self, key):
        return tuple(ref[r, c] for ref, r, c in self._slots(key))


def _block_diag(a, b):
    za = jnp.zeros(a.shape, a.dtype)
    return jnp.concatenate([jnp.concatenate([a, za], axis=1), jnp.concatenate([za, b], axis=1)], axis=0)


def _front_items(m: _MixRefs, layer, r0, blk, trow0, win0, lk, valid_fn, stash):
    z_ref = m.z
    cq_ref, sq_ref, ck_ref, sk_ref = m.tabs
    decay_ref, qdec_ref, kdec_ref, _ = m.dec
    ka_ref, kb_ref, kc_ref, kd_ref = m.kv_bufs[:4]
    rws = slice(r0, r0 + blk)
    trs = slice(trow0, trow0 + blk)
    win = slice(win0, win0 + lk)

    def ret_front(hp):
        cq, sq, ck, sk = cq_ref[trs, :], sq_ref[trs, :], ck_ref[trs, :], sk_ref[trs, :]
        qs, ks, vs, qds, kds = [], [], [], [], []
        for hd in (2 * hp, 2 * hp + 1):
            c = hd * RET_DK
            q = z_ref[rws, OFF_RQ + c:OFF_RQ + c + RET_DK]
            k = z_ref[rws, OFF_RK + c:OFF_RK + c + RET_DK]
            q = q * cq + pltpu.roll(q, RET_DK // 2, 1) * sq
            k = k * ck + pltpu.roll(k, RET_DK // 2, 1) * sk
            qs.append(q.astype(BF16))
            ks.append(k.astype(BF16))
            vs.append(z_ref[rws, OFF_RV + c:OFF_RV + c + RET_DV].astype(BF16))
            qds.append((q * qdec_ref[hd]).astype(BF16))
            kds.append(k * kdec_ref[hd])
        s2 = _dot_nt(jnp.concatenate(qs, axis=1), _block_diag(*ks)) * decay_ref[hp]
        lhs = jnp.concatenate([s2.astype(BF16)] + qds, axis=1)
        kd_t = jnp.concatenate(kds, axis=0).T.astype(BF16)
        stash.put(("ret", hp), lhs, kd_t, jnp.concatenate(vs, axis=1))

    def gates():
        stash.put(("gate", 0), jnp.concatenate([_silu(z_ref[rws, OFF_RG:OFF_RG + RET_WIDTH]),
                                                _silu(z_ref[rws, OFF_AG:OFF_AG + ATT_WIDTH])], axis=1))

    def att_front(kv):
        low = _low_lanes()
        valid = None if valid_fn is None else valid_fn()
        slabs = (2 * kv, 2 * kv + 1)
        k_lo, k_hi = (ka_ref, kb_ref) if kv == 0 else (kc_ref, kd_ref)
        qf = jnp.concatenate([z_ref[rws, OFF_AQ + sl * LANES:OFF_AQ + (sl + 1) * LANES] for sl in slabs], axis=0)
        qf = (qf * (ATT_HEAD_DIM ** -0.5 * LOG2E)).astype(BF16)
        keys = jnp.concatenate([k_lo[win, :], k_hi[win, :]], axis=0)
        s = _dot_nt(qf, keys)
        ps, es = [], []
        for i, sl in enumerate(slabs):
            row_p, row_e = [], []
            for h in range(2):
                sh = s[i * blk:(i + 1) * blk, h * lk:(h + 1) * lk]
                if valid is not None:
                    sh = jnp.where(valid, sh, MASKED)
                sink = m.sink[layer, 2 * sl + h] * LOG2E
                mh = jnp.maximum(jnp.max(sh, axis=-1, keepdims=True), sink)
                row_p.append(jnp.exp2(sh - mh).astype(BF16))
                row_e.append(jnp.exp2(sink - mh))
            ps.append(jnp.concatenate(row_p, axis=1))
            es.append(jnp.broadcast_to(jnp.where(low, row_e[0], row_e[1]), (blk, LANES)))
        stash.put(("att", kv), jnp.concatenate(ps, axis=0), *es)

    return ([functools.partial(ret_front, hp) for hp in range(RET_HEADS // 2)] + [gates]
            + [functools.partial(att_front, kv) for kv in range(ATT_KV_HEADS)])


def _back_items(m: _MixRefs, r0, blk, win0, lk, r_get, r_set, stash):
    mix_ref = m.mix
    gl_ref = m.dec[3]
    va_ref, vb_ref, vc_ref, vd_ref = m.kv_bufs[4:]
    rws = slice(r0, r0 + blk)
    win = slice(win0, win0 + lk)

    def ret_back(hp):
        heads = (2 * hp, 2 * hp + 1)
        lhs, kd_t, v2 = stash.get(("ret", hp))
        gate = stash.get(("gate", 0))[0]
        r_old = [r_get(hd) for hd in heads]
        v_bd = _block_diag(v2[:, :RET_DV], v2[:, RET_DV:])
        r_bd = _block_diag(*[r.astype(BF16) for r in r_old])
        o2 = _dot(lhs, jnp.concatenate([v_bd, r_bd], axis=0))
        r_inc = _dot(kd_t, v_bd)
        for i, hd in enumerate(heads):
            c = hd * RET_DK
            r_set(hd, gl_ref[hd] * r_old[i] + r_inc[:, i * RET_DV:(i + 1) * RET_DV])
            o = o2[:, i * RET_DV:(i + 1) * RET_DV]
            mu = jnp.mean(o, axis=-1, keepdims=True)
            oc = o - mu
            var = jnp.mean(oc * oc, axis=-1, keepdims=True)
            on = oc * lax.rsqrt(var + GN_EPS)
            mix_ref[rws, c:c + RET_DV] = (on * gate[:, c:c + RET_DV]).astype(BF16)

    def att_back(kv):
        v_lo, v_hi = (va_ref, vb_ref) if kv == 0 else (vc_ref, vd_ref)
        p, *es = stash.get(("att", kv))
        gate = stash.get(("gate", 0))[0]
        vals = jnp.concatenate([v_lo[win, :], v_hi[win, :]], axis=0)
        acc = _dot(p, vals)
        for i, sl in enumerate((2 * kv, 2 * kv + 1)):
            c = RET_WIDTH + sl * LANES
            a = acc[i * blk:(i + 1) * blk, :]
            den = a[:, LANES:] + es[i]
            mix_ref[rws, c:c + LANES] = (a[:, :LANES] / den * gate[:, c:c + LANES]).astype(BF16)

    return ([functools.partial(ret_back, hp) for hp in range(RET_HEADS // 2)]
            + [functools.partial(att_back, kv) for kv in range(ATT_KV_HEADS)])


def _final_norm_rows(xo_ref, r0, n, fg_row):
    for a in range(r0, r0 + n, NORM_ROWS):
        xn = xo_ref[a:a + NORM_ROWS, :]
        ms = jnp.mean(xn * xn, axis=-1, keepdims=True)
        xo_ref[a:a + NORM_ROWS, :] = xn * lax.rsqrt(ms + NORM_EPS) * fg_row


class _PromptCfg(NamedTuple):
    rows: int
    nt: int
    ntiles: int
    last: bool
    layer: int
    first: bool


N_STASH = 6
N_KV = 8


def _prompt_kernel(xn_ref, xc_ref, modn_ref, modc_ref, g_ref, win_ref, wout_ref,
                   cq_ref, sq_ref, ck_ref, sk_ref, decay_ref, qdec_ref, kdec_ref, gl_ref, sink_ref, fg_ref,
                   _ro_prev, _ko_prev, _vo_prev,
                   xo_ref, ro_ref, ko_ref, vo_ref,
                   za_ref, zb_ref, h_ref, mixa_ref, mixb_ref, r_scr, *bufs,
                   cfg: _PromptCfg):
    rows, nt, ntiles = cfg.rows, cfg.nt, cfg.ntiles
    blk, lk = PROMPT_BLOCK, WINDOW + PROMPT_BLOCK
    stash_refs = (bufs[0:N_STASH], bufs[N_STASH:2 * N_STASH])
    kv_refs = (bufs[2 * N_STASH:2 * N_STASH + N_KV], bufs[2 * N_STASH + N_KV:2 * N_STASH + 2 * N_KV])
    z_refs = (za_ref, zb_ref)
    mix_refs = (mixa_ref, mixb_ref)
    g = pl.program_id(0)
    t = jnp.clip(g - 1, 0, ntiles - 1) % nt
    t3 = jnp.clip(g - 2, 0, ntiles - 1) % nt

    @pl.when(g == 0)
    def _():
        zb_ref[...] = jnp.zeros(zb_ref.shape, F32)
        mixb_ref[...] = jnp.zeros(mixb_ref.shape, BF16)
        for b in stash_refs[0] + kv_refs[0]:
            b[...] = jnp.zeros(b.shape, b.dtype)
        stash_refs[0][-1][...] = jnp.ones(stash_refs[0][-1].shape, F32)
        _init_kv_ones(kv_refs[0])
        _init_kv_ones(kv_refs[1])

    @pl.when(t3 == 0)
    def _():
        r_scr[...] = jnp.zeros(r_scr.shape, F32)

    (ro_ref, ko_ref, vo_ref), clear_other_layers = _own_layer((ro_ref, ko_ref, vo_ref), cfg.layer, cfg.first)

    @pl.when((t == 0) | (t3 == 0))
    def _():
        clear_other_layers()

    def r_set(hd, val):
        r_scr[hd] = val

    def visible(j):
        col_i = lax.broadcasted_iota(jnp.int32, (blk, lk), 1)
        ok = None
        if blk == 2 * CHUNK:
            row_i = lax.broadcasted_iota(jnp.int32, (blk, lk), 0)
            first_key = jnp.where(row_i < CHUNK, 0, CHUNK)
            ok = (col_i >= first_key) & (col_i < first_key + WINDOW + CHUNK)
        if j * blk < WINDOW:
            started = col_i >= jnp.where(t == 0, WINDOW - j * blk, 0)
            ok = started if ok is None else ok & started
        return ok

    def step(p):
        a, b = p, 1 - p
        tabs = (cq_ref, sq_ref, ck_ref, sk_ref)
        dec = (decay_ref, qdec_ref, kdec_ref, gl_ref)

        norm = _norm_items(xn_ref, h_ref, 0, rows, g_ref, modn_ref, 0)
        proj = _in_proj_items(h_ref, win_ref, z_refs[a])

        def out_item(c0):
            cs = slice(c0, c0 + PROJ_COLS)
            gate_row = modc_ref[0][:, 2 * D_MODEL + c0:2 * D_MODEL + c0 + PROJ_COLS]
            xo_ref[:, cs] = xc_ref[:, cs] + gate_row * _dot(mix_refs[b][...], wout_ref[:, cs])
        outp = [functools.partial(out_item, c0) for c0 in range(0, D_MODEL, PROJ_COLS)]

        zr_ref = z_refs[b]
        for dst, src in zip(kv_refs[b], kv_refs[a]):
            dst[0:WINDOW, :] = src[rows:rows + WINDOW, :]
        for r0 in range(0, rows, CHUNK):
            _fill_kv(kv_refs[b], WINDOW + r0, CHUNK, zr_ref[r0:r0 + CHUNK, OFF_AK:OFF_AK + KV_WIDTH],
                     zr_ref[r0:r0 + CHUNK, OFF_AV:OFF_AV + KV_WIDTH])
        front = []
        for j in range(rows // blk):
            front += _front_items(_MixRefs(zr_ref, None, tabs, dec, sink_ref, kv_refs[b]), cfg.layer,
                                  j * blk, blk, j * blk, j * blk, lk, functools.partial(visible, j),
                                  _RefStash(stash_refs[b], j, blk))

        back = []
        for j in range(rows // blk):
            back += _back_items(_MixRefs(None, mix_refs[a], tabs, dec, sink_ref, kv_refs[a]),
                                j * blk, blk, j * blk, lk, lambda hd: r_scr[hd], r_set,
                                _RefStash(stash_refs[a], j, blk))

        _emit(_interleave(norm, outp))
        _emit(_interleave(_interleave(front, back), proj))
        if cfg.last:
            _final_norm_rows(xo_ref, 0, rows, fg_ref[...])

        @pl.when(t == nt - 1)
        def _():
            ko_ref[...] = zr_ref[rows - WINDOW:rows, OFF_AK:OFF_AK + KV_WIDTH]
            vo_ref[...] = zr_ref[rows - WINDOW:rows, OFF_AV:OFF_AV + KV_WIDTH]

    for parity in (0, 1):
        @pl.when(g % 2 == parity)
        def _():
            step(parity)

    @pl.when((t3 == nt - 1) & (g <= ntiles + 1))
    def _():
        ro_ref[...] = r_scr[...]


def _stash_scratch(rows, blk, lk):
    nblk = rows // blk
    return [
        pltpu.VMEM((rows, 2 * (2 * blk + 2 * RET_DK)), BF16),
        pltpu.VMEM((nblk * 2 * RET_DK, 2 * blk), BF16),
        pltpu.VMEM((rows, RET_WIDTH), BF16),
        pltpu.VMEM((rows, MIX_WIDTH), F32),
        pltpu.VMEM((nblk * 2 * 2 * blk, 2 * lk), BF16),
        pltpu.VMEM((rows, ATT_WIDTH), F32),
    ]


def _kv_scratch(krows):
    return ([pltpu.VMEM((krows, LANES), BF16) for _ in range(4)]
            + [pltpu.VMEM((krows, 2 * LANES), BF16) for _ in range(4)])


def _const_spec(shape):
    nd = len(shape)
    return pl.BlockSpec(shape, lambda g: (0,) * nd)


def _layer_spec(shape, layer):
    nd = len(shape)
    return pl.BlockSpec((None, *shape), lambda g: (layer,) + (0,) * nd, pipeline_mode=pl.Buffered(1))


def _smem_spec():
    return pl.BlockSpec(memory_space=pltpu.SMEM)


N_STATE = 3


def _is_first(acc):
    return all(isinstance(a, jax.ShapeDtypeStruct) for a in acc)


def _own_layer(state_refs, layer, first):
    if not first:
        return state_refs, lambda: None

    def clear():
        for ref in state_refs:
            for l in range(DEPTH):
                if l != layer:
                    ref[l] = jnp.zeros(ref.shape[1:], ref.dtype)
    return [ref.at[layer] for ref in state_refs], clear


def _state_alias(kernel, n_in, acc):
    out_shape = [jax.ShapeDtypeStruct(a.shape, a.dtype) for a in acc]
    if _is_first(acc):
        def first(*refs, **kw):
            return kernel(*refs[:n_in], *([None] * N_STATE), *refs[n_in:], **kw)
        return first, [], [], {}, out_shape
    in_specs = [pl.BlockSpec(memory_space=pl.ANY)] * N_STATE
    aliases = {n_in + i: 1 + i for i in range(N_STATE)}
    return kernel, list(acc), in_specs, aliases, out_shape


def _prompt_layer(layer, last, x2d, mod, g, w_in, w_out, tabs, dec, sink, final_g, acc, batch, seq):
    rows = PROMPT_ROWS
    blk, lk = PROMPT_BLOCK, WINDOW + PROMPT_BLOCK
    nt = seq // rows
    ntiles = batch * nt
    first = _is_first(acc)
    cfg = _PromptCfg(rows=rows, nt=nt, ntiles=ntiles, last=last, layer=layer, first=first)
    cq, sq, ck, sk = tabs
    decay, qdec, kdec, gl = dec

    nxt = lambda g: jnp.minimum(g, ntiles - 1)
    cur = lambda g: jnp.clip(g - 1, 0, ntiles - 1)
    bak = lambda g: jnp.clip(g - 2, 0, ntiles - 1)
    fin = lambda g: jnp.maximum(g - 3, 0)
    tab_spec = pl.BlockSpec((rows, LANES), lambda g: (cur(g) % nt, 0))
    in_specs = [
        pl.BlockSpec((rows, D_MODEL), lambda g: (nxt(g), 0)),
        pl.BlockSpec((rows, D_MODEL), lambda g: (fin(g), 0)),
        pl.BlockSpec((None, 1, 1, 3 * D_MODEL), lambda g: (layer, nxt(g) // nt, 0, 0)),
        pl.BlockSpec((None, 1, 1, 3 * D_MODEL), lambda g: (layer, fin(g) // nt, 0, 0)),
        _layer_spec((1, D_MODEL), layer),
        _layer_spec((D_MODEL, IN_WIDTH), layer),
        _layer_spec((MIX_WIDTH, D_MODEL), layer),
        tab_spec, tab_spec, tab_spec, tab_spec,
        _const_spec(decay.shape), _const_spec(qdec.shape), _const_spec(kdec.shape),
        _smem_spec(), _smem_spec(),
        _const_spec((1, D_MODEL)),
    ]
    body, acc_args, acc_specs, aliases, acc_shape = _state_alias(_prompt_kernel, len(in_specs), acc)
    lyr, lidx = (DEPTH, 0) if first else (None, layer)
    out_specs = [
        pl.BlockSpec((rows, D_MODEL), lambda g: (fin(g), 0)),
        pl.BlockSpec((lyr, None, RET_HEADS, RET_DK, RET_DV), lambda g: (lidx, bak(g) // nt, 0, 0, 0)),
        pl.BlockSpec((lyr, None, WINDOW, KV_WIDTH), lambda g: (lidx, cur(g) // nt, 0, 0)),
        pl.BlockSpec((lyr, None, WINDOW, KV_WIDTH), lambda g: (lidx, cur(g) // nt, 0, 0)),
    ]
    out_shape = [jax.ShapeDtypeStruct((batch * seq, D_MODEL), F32)] + acc_shape
    scratch = [
        pltpu.VMEM((rows, IN_WIDTH), F32),
        pltpu.VMEM((rows, IN_WIDTH), F32),
        pltpu.VMEM((rows, D_MODEL), BF16),
        pltpu.VMEM((rows, MIX_WIDTH), BF16),
        pltpu.VMEM((rows, MIX_WIDTH), BF16),
        pltpu.VMEM((RET_HEADS, RET_DK, RET_DV), F32),
        *_stash_scratch(rows, blk, lk), *_stash_scratch(rows, blk, lk),
        *_kv_scratch(WINDOW + rows), *_kv_scratch(WINDOW + rows),
    ]
    return pl.pallas_call(
        functools.partial(body, cfg=cfg),
        grid=(ntiles + PROMPT_STAGES - 1,),
        in_specs=in_specs + acc_specs, out_specs=out_specs, out_shape=out_shape,
        input_output_aliases=aliases,
        scratch_shapes=scratch,
        compiler_params=pltpu.CompilerParams(
            dimension_semantics=("arbitrary",),
            vmem_limit_bytes=VMEM_LIMIT_BYTES),
        name=f"prompt_layer{layer}",
    )(x2d, x2d, mod, mod, g, w_in, w_out, cq, sq, ck, sk, decay, qdec, kdec, gl, sink, final_g, *acc_args)


class _SampleCfg(NamedTuple):
    nseq: int
    seq: int
    last: bool
    layer: int
    first: bool


def _sample_kernel(x_ref, mod_ref, g_ref, win_ref, wout_ref,
                   cq_ref, sq_ref, ck_ref, sk_ref, decay_ref, qdec_ref, kdec_ref, gl_ref, sink_ref, fg_ref,
                   st_ref, cachek_ref, cachev_ref,
                   _ro_prev, _ko_prev, _vo_prev,
                   xo_ref, ro_ref, ko_ref, vo_ref,
                   z_ref, h_ref, mix_ref, *kv_bufs,
                   cfg: _SampleCfg):
    nseq, seq = cfg.nseq, cfg.seq
    lk = SAMPLE_KEYS
    pad = lk - WINDOW - seq

    @pl.when(pl.program_id(0) == 0)
    def _():
        _init_kv_ones(kv_bufs)

    (ro_ref, ko_ref, vo_ref), clear_other_layers = _own_layer((ro_ref, ko_ref, vo_ref), cfg.layer, cfg.first)
    clear_other_layers()

    for s in range(nseq):
        _emit(_norm_items(x_ref, h_ref, s * seq, seq, g_ref, mod_ref, s))
    _emit(_in_proj_items(h_ref, win_ref, z_ref))

    def visible():
        return lax.broadcasted_iota(jnp.int32, (seq, lk), 1) >= pad

    m = _MixRefs(z_ref, mix_ref, (cq_ref, sq_ref, ck_ref, sk_ref),
                 (decay_ref, qdec_ref, kdec_ref, gl_ref), sink_ref, kv_bufs)
    for s in range(nseq):
        rws = slice(s * seq, (s + 1) * seq)
        k_new = z_ref[rws, OFF_AK:OFF_AK + KV_WIDTH]
        v_new = z_ref[rws, OFF_AV:OFF_AV + KV_WIDTH]
        for b in kv_bufs:
            b[s * lk:s * lk + pad, 0:LANES] = jnp.zeros((pad, LANES), BF16)
        _fill_kv(kv_bufs, s * lk + pad, WINDOW, cachek_ref[s], cachev_ref[s])
        _fill_kv(kv_bufs, s * lk + pad + WINDOW, seq, k_new, v_new)

        def r_set(hd, val, s=s):
            ro_ref[s, hd] = val

        stash = _ValueStash()
        _emit(_front_items(m, cfg.layer, s * seq, seq, 0, s * lk, lk, visible, stash))
        _emit(_back_items(m, s * seq, seq, s * lk, lk, lambda hd, s=s: st_ref[s, hd], r_set, stash))
        ko_ref[s, 0:WINDOW - seq, :] = cachek_ref[s, seq:WINDOW, :]
        vo_ref[s, 0:WINDOW - seq, :] = cachev_ref[s, seq:WINDOW, :]
        ko_ref[s, WINDOW - seq:WINDOW, :] = k_new
        vo_ref[s, WINDOW - seq:WINDOW, :] = v_new

    xo_ref[...] = _dot(mix_ref[...], wout_ref[...])
    for s in range(nseq):
        rws = slice(s * seq, (s + 1) * seq)
        gate_row = mod_ref[s][:, 2 * D_MODEL:3 * D_MODEL]
        xo_ref[rws, :] = x_ref[rws, :] + gate_row * xo_ref[rws, :]
    if cfg.last:
        _final_norm_rows(xo_ref, 0, nseq * seq, fg_ref[...])


def _sample_layer(layer, last, x2d, mod, g, w_in, w_out, tabs, dec, sink, final_g,
                  state, cache_k, cache_v, acc, batch, seq, mod_row0):
    nseq = SAMPLE_SEQS
    rows = nseq * seq
    first = _is_first(acc)
    cfg = _SampleCfg(nseq=nseq, seq=seq, last=last, layer=layer, first=first)
    cq, sq, ck, sk = tabs
    decay, qdec, kdec, gl = dec
    mod_blk0 = mod_row0 // nseq
    in_specs = [
        pl.BlockSpec((rows, D_MODEL), lambda i: (i, 0)),
        pl.BlockSpec((None, nseq, 1, 3 * D_MODEL), lambda i: (layer, mod_blk0 + i, 0, 0)),
        _layer_spec((1, D_MODEL), layer),
        _layer_spec((D_MODEL, IN_WIDTH), layer),
        _layer_spec((MIX_WIDTH, D_MODEL), layer),
        _const_spec(cq.shape), _const_spec(sq.shape), _const_spec(ck.shape), _const_spec(sk.shape),
        _const_spec(decay.shape), _const_spec(qdec.shape), _const_spec(kdec.shape),
        _smem_spec(), _smem_spec(),
        _const_spec((1, D_MODEL)),
        pl.BlockSpec((None, nseq, RET_HEADS, RET_DK, RET_DV), lambda i: (layer, i, 0, 0, 0)),
        pl.BlockSpec((None, nseq, WINDOW, KV_WIDTH), lambda i: (layer, i, 0, 0)),
        pl.BlockSpec((None, nseq, WINDOW, KV_WIDTH), lambda i: (layer, i, 0, 0)),
    ]
    body, acc_args, acc_specs, aliases, acc_shape = _state_alias(_sample_kernel, len(in_specs), acc)
    lyr, lidx = (DEPTH, 0) if first else (None, layer)
    out_specs = [
        pl.BlockSpec((rows, D_MODEL), lambda i: (i, 0)),
        pl.BlockSpec((lyr, nseq, RET_HEADS, RET_DK, RET_DV), lambda i: (lidx, i, 0, 0, 0)),
        pl.BlockSpec((lyr, nseq, WINDOW, KV_WIDTH), lambda i: (lidx, i, 0, 0)),
        pl.BlockSpec((lyr, nseq, WINDOW, KV_WIDTH), lambda i: (lidx, i, 0, 0)),
    ]
    out_shape = [jax.ShapeDtypeStruct((batch * seq, D_MODEL), F32)] + acc_shape
    scratch = [
        pltpu.VMEM((rows, IN_WIDTH), F32),
        pltpu.VMEM((rows, D_MODEL), BF16),
        pltpu.VMEM((rows, MIX_WIDTH), BF16),
        *_kv_scratch(nseq * SAMPLE_KEYS),
    ]
    return pl.pallas_call(
        functools.partial(body, cfg=cfg),
        grid=(batch // nseq,),
        in_specs=in_specs + acc_specs, out_specs=out_specs, out_shape=out_shape,
        input_output_aliases=aliases,
        scratch_shapes=scratch,
        compiler_params=pltpu.CompilerParams(
            dimension_semantics=("arbitrary",),
            vmem_limit_bytes=VMEM_LIMIT_BYTES),
        name=f"sample_layer{layer}",
    )(x2d, mod, g, w_in, w_out, cq, sq, ck, sk, decay, qdec, kdec, gl, sink, final_g,
      state, cache_k, cache_v, *acc_args)


def _rope_tables(start, n):
    d = RET_DK
    inv = 1.0 / (ROPE_BASE ** (jnp.arange(0, d, 2, dtype=F32) / d))
    ang_a = jnp.arange(start, start + n, CHUNK).astype(F32)[:, None] * inv[None, :]
    ang_b = jnp.arange(CHUNK).astype(F32)[:, None] * inv[None, :]
    ca, sa, cb, sb = lax.optimization_barrier((jnp.cos(ang_a), jnp.sin(ang_a), jnp.cos(ang_b), jnp.sin(ang_b)))
    cos = (ca[:, None, :] * cb[None] - sa[:, None, :] * sb[None]).reshape(n, d // 2)
    sin = (sa[:, None, :] * cb[None] + ca[:, None, :] * sb[None]).reshape(n, d // 2)
    cos2 = jnp.concatenate([cos, cos], axis=-1)
    sin2 = jnp.concatenate([-sin, sin], axis=-1)
    kscale = RET_DK ** -0.5
    return cos2, sin2, cos2 * kscale, sin2 * kscale


def _decay_tables(n):
    lg = jnp.log(1.0 - 2.0 ** (-5.0 - jnp.arange(RET_HEADS, dtype=F32)))
    idx = jnp.arange(n, dtype=F32)
    diff = idx[:, None] - idx[None, :]
    decay = jnp.where(diff[None] >= 0, jnp.exp(jnp.maximum(diff, 0.0)[None] * lg[:, None, None]), 0.0)
    qdec = jnp.exp((idx + 1.0)[None, :] * lg[:, None])
    kdec = jnp.exp((n - 1.0 - idx)[None, :] * lg[:, None])
    gl = jnp.exp(n * lg)
    bcast = lambda a: jnp.broadcast_to(a[:, :, None], (RET_HEADS, n, LANES))
    decay2 = jnp.concatenate([decay[0::2], decay[1::2]], axis=-1)
    return decay2, bcast(qdec), bcast(kdec), gl


def kernel(x_prompt, x_sample, c_prompt, c_sample, state_ret, cache_k, cache_v,
           norm_g, w_ada, b_ada, w_in, sink, w_out, final_g):
    batch, seq, _ = x_prompt.shape
    dbatch, dseq, _ = x_sample.shape

    c_all = jnp.concatenate([c_prompt, c_sample], axis=0)
    mod = _adaln(c_all, w_ada, b_ada).reshape(DEPTH, batch + dbatch, 1, 3 * D_MODEL)
    w_in_b = w_in.astype(BF16)
    w_out_b = w_out.astype(BF16)

    tabs_p = _rope_tables(0, seq)
    tabs_s = _rope_tables(PAST_LEN, dseq)
    dec_p = _decay_tables(PROMPT_BLOCK)
    dec_s = _decay_tables(dseq)
    ck4 = cache_k.reshape(DEPTH, dbatch, WINDOW, KV_WIDTH)
    cv4 = cache_v.reshape(DEPTH, dbatch, WINDOW, KV_WIDTH)
    fg = final_g.reshape(1, D_MODEL)
    g_all = norm_g.reshape(DEPTH, 1, D_MODEL)

    xp = x_prompt.reshape(batch * seq, D_MODEL)
    xs = x_sample.reshape(dbatch * dseq, D_MODEL)
    def state_acc(nb):
        return (jax.ShapeDtypeStruct((DEPTH, nb, RET_HEADS, RET_DK, RET_DV), F32),
                jax.ShapeDtypeStruct((DEPTH, nb, WINDOW, KV_WIDTH), F32),
                jax.ShapeDtypeStruct((DEPTH, nb, WINDOW, KV_WIDTH), F32))

    acc_p, acc_s = state_acc(batch), state_acc(dbatch)
    for l in range(DEPTH):
        last = l == DEPTH - 1
        xp, *acc_p = _prompt_layer(l, last, xp, mod, g_all, w_in_b, w_out_b, tabs_p, dec_p,
                                   sink, fg, acc_p, batch, seq)
        xs, *acc_s = _sample_layer(l, last, xs, mod, g_all, w_in_b, w_out_b, tabs_s, dec_s,
                                   sink, fg, state_ret, ck4, cv4, acc_s, dbatch, dseq, mod_row0=batch)

    kv_shape_p = (DEPTH, batch, WINDOW, ATT_KV_HEADS, ATT_HEAD_DIM)
    kv_shape_s = (DEPTH, dbatch, WINDOW, ATT_KV_HEADS, ATT_HEAD_DIM)
    return (xp.reshape(batch, seq, D_MODEL), xs.reshape(dbatch, dseq, D_MODEL),
            acc_p[0], acc_p[1].reshape(kv_shape_p), acc_p[2].reshape(kv_shape_p),
            acc_s[0], acc_s[1].reshape(kv_shape_s), acc_s[2].reshape(kv_shape_s))
```

```python
import functools
from typing import NamedTuple

import jax
import jax.numpy as jnp
from jax import lax
from jax.experimental import pallas as pl
from jax.experimental.pallas import tpu as pltpu

D_MODEL = 1024
DEPTH = 4
CHUNK = 64
PAST_LEN = 4096
RET_HEADS = 4
RET_DK = 128
RET_DV = 128
RET_WIDTH = RET_HEADS * RET_DV
ATT_HEAD_DIM = 64
ATT_Q_HEADS = 8
ATT_KV_HEADS = 2
ATT_WIDTH = ATT_Q_HEADS * ATT_HEAD_DIM
KV_WIDTH = ATT_KV_HEADS * ATT_HEAD_DIM
WINDOW = 128
MIX_WIDTH = RET_WIDTH + ATT_WIDTH
ROPE_BASE = 10000.0
NORM_EPS = 1e-6
GN_EPS = 1e-5

OFF_RQ = 0
OFF_RK = OFF_RQ + RET_HEADS * RET_DK
OFF_RV = OFF_RK + RET_HEADS * RET_DK
OFF_RG = OFF_RV + RET_WIDTH
OFF_AQ = OFF_RG + RET_WIDTH
OFF_AK = OFF_AQ + ATT_WIDTH
OFF_AV = OFF_AK + KV_WIDTH
OFF_AG = OFF_AV + KV_WIDTH
IN_WIDTH = OFF_AG + ATT_WIDTH

LANES = 128
ATT_SLABS = ATT_WIDTH // LANES
MASKED = -1e30
LOG2E = 1.4426950408889634
VMEM_LIMIT_BYTES = 56 * 1024 * 1024
NORM_ROWS = 32
PROJ_COLS = 512

PROMPT_ROWS = 256
PROMPT_BLOCK = 128
PROMPT_STAGES = 4
SAMPLE_SEQS = 4
SAMPLE_KEYS = 2 * LANES

BF16 = jnp.bfloat16
F32 = jnp.float32


def _silu(g):
    h = 0.5 * g
    return h + h * jnp.tanh(h)


def _dot(a, b):
    return jnp.dot(a, b, preferred_element_type=F32)


def _dot_nt(a, b):
    return lax.dot_general(a, b, (((1,), (1,)), ((), ())), preferred_element_type=F32)


def _low_lanes():
    lane = lax.broadcasted_iota(jnp.int32, (1, LANES), 1)
    return lane < ATT_HEAD_DIM


def _adaln_kernel(c_ref, w_ref, b_ref, o_ref):
    s = _silu(c_ref[...]).astype(BF16)
    o_ref[0] = _dot(s, w_ref[0].astype(BF16)) + b_ref[0]


def _adaln(c_all, w_ada, b_ada):
    n = c_all.shape[0]
    tn = D_MODEL
    return pl.pallas_call(
        _adaln_kernel,
        grid=(DEPTH, 3 * D_MODEL // tn),
        in_specs=[
            pl.BlockSpec((n, D_MODEL), lambda l, j: (0, 0)),
            pl.BlockSpec((1, D_MODEL, tn), lambda l, j: (l, 0, j)),
            pl.BlockSpec((1, 1, tn), lambda l, j: (l, 0, j)),
        ],
        out_specs=pl.BlockSpec((1, n, tn), lambda l, j: (l, 0, j)),
        out_shape=jax.ShapeDtypeStruct((DEPTH, n, 3 * D_MODEL), F32),
        compiler_params=pltpu.CompilerParams(
            dimension_semantics=("arbitrary", "arbitrary"),
            vmem_limit_bytes=VMEM_LIMIT_BYTES),
        name="adaln",
    )(c_all, w_ada, b_ada.reshape(DEPTH, 1, 3 * D_MODEL))


def _emit(items):
    for it in items:
        it()


def _interleave(a, b):
    if not a:
        return list(b)
    out, nb = [], 0
    for i, it in enumerate(a):
        out.append(it)
        want = (i + 1) * len(b) // len(a)
        out.extend(b[nb:want])
        nb = want
    return out


def _norm_items(x_ref, h_ref, r0, n, g_ref, mod_ref, s):
    def item(a):
        mod_row = mod_ref[s]
        shift = mod_row[:, 0:D_MODEL]
        gain = g_ref[...] * (1.0 + mod_row[:, D_MODEL:2 * D_MODEL])
        xb = x_ref[a:a + NORM_ROWS, :]
        ms = jnp.mean(xb * xb, axis=-1, keepdims=True)
        h_ref[a:a + NORM_ROWS, :] = (xb * lax.rsqrt(ms + NORM_EPS) * gain + shift).astype(BF16)
    return [functools.partial(item, a) for a in range(r0, r0 + n, NORM_ROWS)]


def _in_proj_items(h_ref, win_ref, z_ref):
    def item(c0):
        c1 = min(c0 + PROJ_COLS, IN_WIDTH)
        z_ref[:, c0:c1] = _dot(h_ref[...], win_ref[:, c0:c1])
    return [functools.partial(item, c0) for c0 in range(0, IN_WIDTH, PROJ_COLS)]


def _fill_kv(kv_bufs, dst0, n, k, v):
    ka_ref, kb_ref, kc_ref, kd_ref, va_ref, vb_ref, vc_ref, vd_ref = kv_bufs
    low = _low_lanes()
    kr = pltpu.roll(k, ATT_HEAD_DIM, 1)
    vr = pltpu.roll(v, ATT_HEAD_DIM, 1)
    d = slice(dst0, dst0 + n)
    ka_ref[d, :] = jnp.where(low, k, 0.0).astype(BF16)
    kb_ref[d, :] = jnp.where(low, 0.0, kr).astype(BF16)
    kc_ref[d, :] = jnp.where(low, kr, 0.0).astype(BF16)
    kd_ref[d, :] = jnp.where(low, 0.0, k).astype(BF16)
    va_ref[d, 0:LANES] = jnp.where(low, v, 0.0).astype(BF16)
    vb_ref[d, 0:LANES] = jnp.where(low, 0.0, vr).astype(BF16)
    vc_ref[d, 0:LANES] = jnp.where(low, vr, 0.0).astype(BF16)
    vd_ref[d, 0:LANES] = jnp.where(low, 0.0, v).astype(BF16)


def _init_kv_ones(kv_bufs):
    low = _low_lanes()
    for i, ref in enumerate(kv_bufs[4:]):
        ones = jnp.where(low, 1.0, 0.0) if i % 2 == 0 else jnp.where(low, 0.0, 1.0)
        ref[:, LANES:] = jnp.broadcast_to(ones, (ref.shape[0], LANES)).astype(BF16)


class _MixRefs(NamedTuple):
    z: object
    mix: object
    tabs: tuple
    dec: tuple
    sink: object
    kv_bufs: tuple


class _ValueStash:
    def __init__(self):
        self.d = {}

    def put(self, key, *vals):
        self.d[key] = vals

    def get(self, key):
        return self.d[key]


class _RefStash:
    def __init__(self, refs, j, blk):
        self.refs, self.j, self.blk = refs, j, blk
        self.rws = slice(j * blk, (j + 1) * blk)

    def _slots(self, key):
        lhs_ref, kdt_ref, v_ref, gate_ref, p_ref, e_ref = self.refs
        kind, i = key
        rws, j, blk = self.rws, self.j, self.blk
        if kind == "ret":
            w = lhs_ref.shape[1] // 2
            r0 = (2 * j + i) * RET_DK
            return [(lhs_ref, rws, slice(i * w, (i + 1) * w)), (kdt_ref, slice(r0, r0 + RET_DK), slice(None)),
                    (v_ref, rws, slice(2 * i * RET_DV, 2 * (i + 1) * RET_DV))]
        if kind == "gate":
            return [(gate_ref, rws, slice(None))]
        r0 = (2 * j + i) * 2 * blk
        return [(p_ref, slice(r0, r0 + 2 * blk), slice(None)),
                (e_ref, rws, slice(2 * i * LANES, (2 * i + 1) * LANES)),
                (e_ref, rws, slice((2 * i + 1) * LANES, (2 * i + 2) * LANES))]

    def put(self, key, *vals):
        for (ref, r, c), v in zip(self._slots(key), vals):
            ref[r, c] = v

    def get(self, key):
        return tuple(ref[r, c] for ref, r, c in self._slots(key))


def _block_diag(a, b):
    za = jnp.zeros(a.shape, a.dtype)
    return jnp.concatenate([jnp.concatenate([a, za], axis=1), jnp.concatenate([za, b], axis=1)], axis=0)


def _front_items(m: _MixRefs, layer, r0, blk, trow0, win0, lk, valid_fn, stash):
    z_ref = m.z
    cq_ref, sq_ref, ck_ref, sk_ref = m.tabs
    decay_ref, qdec_ref, kdec_ref, _ = m.dec
    ka_ref, kb_ref, kc_ref, kd_ref = m.kv_bufs[:4]
    rws = slice(r0, r0 + blk)
    trs = slice(trow0, trow0 + blk)
    win = slice(win0, win0 + lk)

    def ret_front(hp):
        cq, sq, ck, sk = cq_ref[trs, :], sq_ref[trs, :], ck_ref[trs, :], sk_ref[trs, :]
        qs, ks, vs, qds, kds = [], [], [], [], []
        for hd in (2 * hp, 2 * hp + 1):
            c = hd * RET_DK
            q = z_ref[rws, OFF_RQ + c:OFF_RQ + c + RET_DK]
            k = z_ref[rws, OFF_RK + c:OFF_RK + c + RET_DK]
            q = q * cq + pltpu.roll(q, RET_DK // 2, 1) * sq
            k = k * ck + pltpu.roll(k, RET_DK // 2, 1) * sk
            qs.append(q.astype(BF16))
            ks.append(k.astype(BF16))
            vs.append(z_ref[rws, OFF_RV + c:OFF_RV + c + RET_DV].astype(BF16))
            qds.append((q * qdec_ref[hd]).astype(BF16))
            kds.append(k * kdec_ref[hd])
        s2 = _dot_nt(jnp.concatenate(qs, axis=1), _block_diag(*ks)) * decay_ref[hp]
        lhs = jnp.concatenate([s2.astype(BF16)] + qds, axis=1)
        kd_t = jnp.concatenate(kds, axis=0).T.astype(BF16)
        stash.put(("ret", hp), lhs, kd_t, jnp.concatenate(vs, axis=1))

    def gates():
        stash.put(("gate", 0), jnp.concatenate([_silu(z_ref[rws, OFF_RG:OFF_RG + RET_WIDTH]),
                                                _silu(z_ref[rws, OFF_AG:OFF_AG + ATT_WIDTH])], axis=1))

    def att_front(kv):
        low = _low_lanes()
        valid = None if valid_fn is None else valid_fn()
        slabs = (2 * kv, 2 * kv + 1)
        k_lo, k_hi = (ka_ref, kb_ref) if kv == 0 else (kc_ref, kd_ref)
        qf = jnp.concatenate([z_ref[rws, OFF_AQ + sl * LANES:OFF_AQ + (sl + 1) * LANES] for sl in slabs], axis=0)
        qf = (qf * (ATT_HEAD_DIM ** -0.5 * LOG2E)).astype(BF16)
        keys = jnp.concatenate([k_lo[win, :], k_hi[win, :]], axis=0)
        s = _dot_nt(qf, keys)
        ps, es = [], []
        for i, sl in enumerate(slabs):
            row_p, row_m, sinks = [], [], []
            for h in range(2):
                sh = s[i * blk:(i + 1) * blk, h * lk:(h + 1) * lk]
                if valid is not None:
                    sh = jnp.where(valid, sh, MASKED)
                sink = m.sink[layer, 2 * sl + h] * LOG2E
                mh = jnp.maximum(jnp.max(sh, axis=-1, keepdims=True), sink)
                row_p.append(jnp.exp2(sh - mh).astype(BF16))
                row_m.append(mh)
                sinks.append(sink)
            ps.append(jnp.concatenate(row_p, axis=1))
            es.append(jnp.exp2(jnp.where(low, sinks[0], sinks[1]) - jnp.where(low, row_m[0], row_m[1])))
        stash.put(("att", kv), jnp.concatenate(ps, axis=0), *es)

    return ([functools.partial(ret_front, hp) for hp in range(RET_HEADS // 2)] + [gates]
            + [functools.partial(att_front, kv) for kv in range(ATT_KV_HEADS)])


def _back_items(m: _MixRefs, r0, blk, win0, lk, r_get, r_set, stash):
    mix_ref = m.mix
    gl_ref = m.dec[3]
    va_ref, vb_ref, vc_ref, vd_ref = m.kv_bufs[4:]
    rws = slice(r0, r0 + blk)
    win = slice(win0, win0 + lk)

    def ret_back(hp):
        heads = (2 * hp, 2 * hp + 1)
        lhs, kd_t, v2 = stash.get(("ret", hp))
        gate = stash.get(("gate", 0))[0]
        r_old = [r_get(hd) for hd in heads]
        v_bd = _block_diag(v2[:, :RET_DV], v2[:, RET_DV:])
        r_bd = _block_diag(*[r.astype(BF16) for r in r_old])
        o2 = _dot(lhs, jnp.concatenate([v_bd, r_bd], axis=0))
        r_inc = _dot(kd_t, v_bd)
        for i, hd in enumerate(heads):
            c = hd * RET_DK
            r_set(hd, gl_ref[hd] * r_old[i] + r_inc[:, i * RET_DV:(i + 1) * RET_DV])
            o = o2[:, i * RET_DV:(i + 1) * RET_DV]
            mu = jnp.mean(o, axis=-1, keepdims=True)
            oc = o - mu
            var = jnp.mean(oc * oc, axis=-1, keepdims=True)
            on = oc * lax.rsqrt(var + GN_EPS)
            mix_ref[rws, c:c + RET_DV] = (on * gate[:, c:c + RET_DV]).astype(BF16)

    def att_back(kv):
        v_lo, v_hi = (va_ref, vb_ref) if kv == 0 else (vc_ref, vd_ref)
        p, *es = stash.get(("att", kv))
        gate = stash.get(("gate", 0))[0]
        vals = jnp.concatenate([v_lo[win, :], v_hi[win, :]], axis=0)
        acc = _dot(p, vals)
        for i, sl in enumerate((2 * kv, 2 * kv + 1)):
            c = RET_WIDTH + sl * LANES
            a = acc[i * blk:(i + 1) * blk, :]
            den = a[:, LANES:] + es[i]
            mix_ref[rws, c:c + LANES] = (a[:, :LANES] / den * gate[:, c:c + LANES]).astype(BF16)

    return ([functools.partial(ret_back, hp) for hp in range(RET_HEADS // 2)]
            + [functools.partial(att_back, kv) for kv in range(ATT_KV_HEADS)])


def _final_norm_rows(xo_ref, r0, n, fg_row):
    for a in range(r0, r0 + n, NORM_ROWS):
        xn = xo_ref[a:a + NORM_ROWS, :]
        ms = jnp.mean(xn * xn, axis=-1, keepdims=True)
        xo_ref[a:a + NORM_ROWS, :] = xn * lax.rsqrt(ms + NORM_EPS) * fg_row


class _PromptCfg(NamedTuple):
    rows: int
    nt: int
    ntiles: int
    last: bool
    layer: int
    first: bool


N_STASH = 6
N_KV = 8


def _prompt_kernel(xn_ref, xc_ref, modn_ref, modc_ref, g_ref, win_ref, wout_ref,
                   cq_ref, sq_ref, ck_ref, sk_ref, decay_ref, qdec_ref, kdec_ref, gl_ref, sink_ref, fg_ref,
                   _ro_prev, _ko_prev, _vo_prev,
                   xo_ref, ro_ref, ko_ref, vo_ref,
                   za_ref, zb_ref, h_ref, mixa_ref, mixb_ref, r_scr, *bufs,
                   cfg: _PromptCfg):
    rows, nt, ntiles = cfg.rows, cfg.nt, cfg.ntiles
    blk, lk = PROMPT_BLOCK, WINDOW + PROMPT_BLOCK
    stash_refs = (bufs[0:N_STASH], bufs[N_STASH:2 * N_STASH])
    kv_refs = (bufs[2 * N_STASH:2 * N_STASH + N_KV], bufs[2 * N_STASH + N_KV:2 * N_STASH + 2 * N_KV])
    z_refs = (za_ref, zb_ref)
    mix_refs = (mixa_ref, mixb_ref)
    g = pl.program_id(0)
    t = jnp.clip(g - 1, 0, ntiles - 1) % nt
    t3 = jnp.clip(g - 2, 0, ntiles - 1) % nt

    @pl.when(g == 0)
    def _():
        zb_ref[...] = jnp.zeros(zb_ref.shape, F32)
        mixb_ref[...] = jnp.zeros(mixb_ref.shape, BF16)
        for b in stash_refs[0] + kv_refs[0]:
            b[...] = jnp.zeros(b.shape, b.dtype)
        stash_refs[0][-1][...] = jnp.ones(stash_refs[0][-1].shape, F32)
        _init_kv_ones(kv_refs[0])
        _init_kv_ones(kv_refs[1])

    @pl.when(t3 == 0)
    def _():
        r_scr[...] = jnp.zeros(r_scr.shape, F32)

    (ro_ref, ko_ref, vo_ref), clear_other_layers = _own_layer((ro_ref, ko_ref, vo_ref), cfg.layer, cfg.first)

    @pl.when((t == 0) | (t3 == 0))
    def _():
        clear_other_layers()

    def r_set(hd, val):
        r_scr[hd] = val

    def visible(j):
        col_i = lax.broadcasted_iota(jnp.int32, (blk, lk), 1)
        ok = None
        if blk == 2 * CHUNK:
            row_i = lax.broadcasted_iota(jnp.int32, (blk, lk), 0)
            first_key = jnp.where(row_i < CHUNK, 0, CHUNK)
            ok = (col_i >= first_key) & (col_i < first_key + WINDOW + CHUNK)
        if j * blk < WINDOW:
            started = col_i >= jnp.where(t == 0, WINDOW - j * blk, 0)
            ok = started if ok is None else ok & started
        return ok

    def step(p):
        a, b = p, 1 - p
        tabs = (cq_ref, sq_ref, ck_ref, sk_ref)
        dec = (decay_ref, qdec_ref, kdec_ref, gl_ref)

        norm = _norm_items(xn_ref, h_ref, 0, rows, g_ref, modn_ref, 0)
        proj = _in_proj_items(h_ref, win_ref, z_refs[a])

        def out_item(c0):
            cs = slice(c0, c0 + PROJ_COLS)
            gate_row = modc_ref[0][:, 2 * D_MODEL + c0:2 * D_MODEL + c0 + PROJ_COLS]
            xo_ref[:, cs] = xc_ref[:, cs] + gate_row * _dot(mix_refs[b][...], wout_ref[:, cs])
        outp = [functools.partial(out_item, c0) for c0 in range(0, D_MODEL, PROJ_COLS)]

        zr_ref = z_refs[b]
        for dst, src in zip(kv_refs[b], kv_refs[a]):
            dst[0:WINDOW, :] = src[rows:rows + WINDOW, :]
        for r0 in range(0, rows, CHUNK):
            _fill_kv(kv_refs[b], WINDOW + r0, CHUNK, zr_ref[r0:r0 + CHUNK, OFF_AK:OFF_AK + KV_WIDTH],
                     zr_ref[r0:r0 + CHUNK, OFF_AV:OFF_AV + KV_WIDTH])
        front = []
        for j in range(rows // blk):
            front += _front_items(_MixRefs(zr_ref, None, tabs, dec, sink_ref, kv_refs[b]), cfg.layer,
                                  j * blk, blk, j * blk, j * blk, lk, functools.partial(visible, j),
                                  _RefStash(stash_refs[b], j, blk))

        back = []
        for j in range(rows // blk):
            back += _back_items(_MixRefs(None, mix_refs[a], tabs, dec, sink_ref, kv_refs[a]),
                                j * blk, blk, j * blk, lk, lambda hd: r_scr[hd], r_set,
                                _RefStash(stash_refs[a], j, blk))

        _emit(_interleave(norm, outp))
        _emit(_interleave(_interleave(front, back), proj))
        if cfg.last:
            _final_norm_rows(xo_ref, 0, rows, fg_ref[...])

        @pl.when(t == nt - 1)
        def _():
            ko_ref[...] = zr_ref[rows - WINDOW:rows, OFF_AK:OFF_AK + KV_WIDTH]
            vo_ref[...] = zr_ref[rows - WINDOW:rows, OFF_AV:OFF_AV + KV_WIDTH]

    for parity in (0, 1):
        @pl.when(g % 2 == parity)
        def _():
            step(parity)

    @pl.when((t3 == nt - 1) & (g <= ntiles + 1))
    def _():
        ro_ref[...] = r_scr[...]


def _stash_scratch(rows, blk, lk):
    nblk = rows // blk
    return [
        pltpu.VMEM((rows, 2 * (2 * blk + 2 * RET_DK)), BF16),
        pltpu.VMEM((nblk * 2 * RET_DK, 2 * blk), BF16),
        pltpu.VMEM((rows, RET_WIDTH), BF16),
        pltpu.VMEM((rows, MIX_WIDTH), F32),
        pltpu.VMEM((nblk * 2 * 2 * blk, 2 * lk), BF16),
        pltpu.VMEM((rows, ATT_WIDTH), F32),
    ]


def _kv_scratch(krows):
    return ([pltpu.VMEM((krows, LANES), BF16) for _ in range(4)]
            + [pltpu.VMEM((krows, 2 * LANES), BF16) for _ in range(4)])


def _const_spec(shape):
    nd = len(shape)
    return pl.BlockSpec(shape, lambda g: (0,) * nd)


def _layer_spec(shape, layer):
    nd = len(shape)
    return pl.BlockSpec((None, *shape), lambda g: (layer,) + (0,) * nd)


def _smem_spec():
    return pl.BlockSpec(memory_space=pltpu.SMEM)


N_STATE = 3


def _is_first(acc):
    return all(isinstance(a, jax.ShapeDtypeStruct) for a in acc)


def _own_layer(state_refs, layer, first):
    if not first:
        return state_refs, lambda: None

    def clear():
        for ref in state_refs:
            for l in range(DEPTH):
                if l != layer:
                    ref[l] = jnp.zeros(ref.shape[1:], ref.dtype)
    return [ref.at[layer] for ref in state_refs], clear


def _state_alias(kernel, n_in, acc):
    out_shape = [jax.ShapeDtypeStruct(a.shape, a.dtype) for a in acc]
    if _is_first(acc):
        def first(*refs, **kw):
            return kernel(*refs[:n_in], *([None] * N_STATE), *refs[n_in:], **kw)
        return first, [], [], {}, out_shape
    in_specs = [pl.BlockSpec(memory_space=pl.ANY)] * N_STATE
    aliases = {n_in + i: 1 + i for i in range(N_STATE)}
    return kernel, list(acc), in_specs, aliases, out_shape


def _prompt_layer(layer, last, x2d, mod, g, w_in, w_out, tabs, dec, sink, final_g, acc, batch, seq):
    rows = PROMPT_ROWS
    blk, lk = PROMPT_BLOCK, WINDOW + PROMPT_BLOCK
    nt = seq // rows
    ntiles = batch * nt
    first = _is_first(acc)
    cfg = _PromptCfg(rows=rows, nt=nt, ntiles=ntiles, last=last, layer=layer, first=first)
    cq, sq, ck, sk = tabs
    decay, qdec, kdec, gl = dec

    nxt = lambda g: jnp.minimum(g, ntiles - 1)
    cur = lambda g: jnp.clip(g - 1, 0, ntiles - 1)
    bak = lambda g: jnp.clip(g - 2, 0, ntiles - 1)
    fin = lambda g: jnp.maximum(g - 3, 0)
    tab_spec = pl.BlockSpec((rows, LANES), lambda g: (cur(g) % nt, 0))
    in_specs = [
        pl.BlockSpec((rows, D_MODEL), lambda g: (nxt(g), 0)),
        pl.BlockSpec((rows, D_MODEL), lambda g: (fin(g), 0)),
        pl.BlockSpec((None, 1, 1, 3 * D_MODEL), lambda g: (layer, nxt(g) // nt, 0, 0)),
        pl.BlockSpec((None, 1, 1, 3 * D_MODEL), lambda g: (layer, fin(g) // nt, 0, 0)),
        _layer_spec((1, D_MODEL), layer),
        _layer_spec((D_MODEL, IN_WIDTH), layer),
        _layer_spec((MIX_WIDTH, D_MODEL), layer),
        tab_spec, tab_spec, tab_spec, tab_spec,
        _const_spec(decay.shape), _const_spec(qdec.shape), _const_spec(kdec.shape),
        _smem_spec(), _smem_spec(),
        _const_spec((1, D_MODEL)),
    ]
    body, acc_args, acc_specs, aliases, acc_shape = _state_alias(_prompt_kernel, len(in_specs), acc)
    lyr, lidx = (DEPTH, 0) if first else (None, layer)
    out_specs = [
        pl.BlockSpec((rows, D_MODEL), lambda g: (fin(g), 0)),
        pl.BlockSpec((lyr, None, RET_HEADS, RET_DK, RET_DV), lambda g: (lidx, bak(g) // nt, 0, 0, 0)),
        pl.BlockSpec((lyr, None, WINDOW, KV_WIDTH), lambda g: (lidx, cur(g) // nt, 0, 0)),
        pl.BlockSpec((lyr, None, WINDOW, KV_WIDTH), lambda g: (lidx, cur(g) // nt, 0, 0)),
    ]
    out_shape = [jax.ShapeDtypeStruct((batch * seq, D_MODEL), F32)] + acc_shape
    scratch = [
        pltpu.VMEM((rows, IN_WIDTH), F32),
        pltpu.VMEM((rows, IN_WIDTH), F32),
        pltpu.VMEM((rows, D_MODEL), BF16),
        pltpu.VMEM((rows, MIX_WIDTH), BF16),
        pltpu.VMEM((rows, MIX_WIDTH), BF16),
        pltpu.VMEM((RET_HEADS, RET_DK, RET_DV), F32),
        *_stash_scratch(rows, blk, lk), *_stash_scratch(rows, blk, lk),
        *_kv_scratch(WINDOW + rows), *_kv_scratch(WINDOW + rows),
    ]
    return pl.pallas_call(
        functools.partial(body, cfg=cfg),
        grid=(ntiles + PROMPT_STAGES - 1,),
        in_specs=in_specs + acc_specs, out_specs=out_specs, out_shape=out_shape,
        input_output_aliases=aliases,
        scratch_shapes=scratch,
        compiler_params=pltpu.CompilerParams(
            dimension_semantics=("arbitrary",),
            vmem_limit_bytes=VMEM_LIMIT_BYTES),
        name=f"prompt_layer{layer}",
    )(x2d, x2d, mod, mod, g, w_in, w_out, cq, sq, ck, sk, decay, qdec, kdec, gl, sink, final_g, *acc_args)


class _SampleCfg(NamedTuple):
    nseq: int
    seq: int
    last: bool
    layer: int
    first: bool


def _sample_kernel(x_ref, mod_ref, g_ref, win_ref, wout_ref,
                   cq_ref, sq_ref, ck_ref, sk_ref, decay_ref, qdec_ref, kdec_ref, gl_ref, sink_ref, fg_ref,
                   st_ref, cachek_ref, cachev_ref,
                   _ro_prev, _ko_prev, _vo_prev,
                   xo_ref, ro_ref, ko_ref, vo_ref,
                   z_ref, h_ref, mix_ref, *kv_bufs,
                   cfg: _SampleCfg):
    nseq, seq = cfg.nseq, cfg.seq
    lk = SAMPLE_KEYS
    pad = lk - WINDOW - seq

    @pl.when(pl.program_id(0) == 0)
    def _():
        _init_kv_ones(kv_bufs)

    (ro_ref, ko_ref, vo_ref), clear_other_layers = _own_layer((ro_ref, ko_ref, vo_ref), cfg.layer, cfg.first)
    clear_other_layers()

    for s in range(nseq):
        _emit(_norm_items(x_ref, h_ref, s * seq, seq, g_ref, mod_ref, s))
    _emit(_in_proj_items(h_ref, win_ref, z_ref))

    def visible():
        return lax.broadcasted_iota(jnp.int32, (seq, lk), 1) >= pad

    m = _MixRefs(z_ref, mix_ref, (cq_ref, sq_ref, ck_ref, sk_ref),
                 (decay_ref, qdec_ref, kdec_ref, gl_ref), sink_ref, kv_bufs)
    for s in range(nseq):
        rws = slice(s * seq, (s + 1) * seq)
        k_new = z_ref[rws, OFF_AK:OFF_AK + KV_WIDTH]
        v_new = z_ref[rws, OFF_AV:OFF_AV + KV_WIDTH]
        for b in kv_bufs:
            b[s * lk:s * lk + pad, 0:LANES] = jnp.zeros((pad, LANES), BF16)
        _fill_kv(kv_bufs, s * lk + pad, WINDOW, cachek_ref[s], cachev_ref[s])
        _fill_kv(kv_bufs, s * lk + pad + WINDOW, seq, k_new, v_new)

        def r_set(hd, val, s=s):
            ro_ref[s, hd] = val

        stash = _ValueStash()
        _emit(_front_items(m, cfg.layer, s * seq, seq, 0, s * lk, lk, visible, stash))
        _emit(_back_items(m, s * seq, seq, s * lk, lk, lambda hd, s=s: st_ref[s, hd], r_set, stash))
        ko_ref[s, 0:WINDOW - seq, :] = cachek_ref[s, seq:WINDOW, :]
        vo_ref[s, 0:WINDOW - seq, :] = cachev_ref[s, seq:WINDOW, :]
        ko_ref[s, WINDOW - seq:WINDOW, :] = k_new
        vo_ref[s, WINDOW - seq:WINDOW, :] = v_new

    xo_ref[...] = _dot(mix_ref[...], wout_ref[...])
    for s in range(nseq):
        rws = slice(s * seq, (s + 1) * seq)
        gate_row = mod_ref[s][:, 2 * D_MODEL:3 * D_MODEL]
        xo_ref[rws, :] = x_ref[rws, :] + gate_row * xo_ref[rws, :]
    if cfg.last:
        _final_norm_rows(xo_ref, 0, nseq * seq, fg_ref[...])


def _sample_layer(layer, last, x2d, mod, g, w_in, w_out, tabs, dec, sink, final_g,
                  state, cache_k, cache_v, acc, batch, seq, mod_row0):
    nseq = SAMPLE_SEQS
    rows = nseq * seq
    first = _is_first(acc)
    cfg = _SampleCfg(nseq=nseq, seq=seq, last=last, layer=layer, first=first)
    cq, sq, ck, sk = tabs
    decay, qdec, kdec, gl = dec
    mod_blk0 = mod_row0 // nseq
    in_specs = [
        pl.BlockSpec((rows, D_MODEL), lambda i: (i, 0)),
        pl.BlockSpec((None, nseq, 1, 3 * D_MODEL), lambda i: (layer, mod_blk0 + i, 0, 0)),
        _layer_spec((1, D_MODEL), layer),
        _layer_spec((D_MODEL, IN_WIDTH), layer),
        _layer_spec((MIX_WIDTH, D_MODEL), layer),
        _const_spec(cq.shape), _const_spec(sq.shape), _const_spec(ck.shape), _const_spec(sk.shape),
        _const_spec(decay.shape), _const_spec(qdec.shape), _const_spec(kdec.shape),
        _smem_spec(), _smem_spec(),
        _const_spec((1, D_MODEL)),
        pl.BlockSpec((None, nseq, RET_HEADS, RET_DK, RET_DV), lambda i: (layer, i, 0, 0, 0)),
        pl.BlockSpec((None, nseq, WINDOW, KV_WIDTH), lambda i: (layer, i, 0, 0)),
        pl.BlockSpec((None, nseq, WINDOW, KV_WIDTH), lambda i: (layer, i, 0, 0)),
    ]
    body, acc_args, acc_specs, aliases, acc_shape = _state_alias(_sample_kernel, len(in_specs), acc)
    lyr, lidx = (DEPTH, 0) if first else (None, layer)
    out_specs = [
        pl.BlockSpec((rows, D_MODEL), lambda i: (i, 0)),
        pl.BlockSpec((lyr, nseq, RET_HEADS, RET_DK, RET_DV), lambda i: (lidx, i, 0, 0, 0)),
        pl.BlockSpec((lyr, nseq, WINDOW, KV_WIDTH), lambda i: (lidx, i, 0, 0)),
        pl.BlockSpec((lyr, nseq, WINDOW, KV_WIDTH), lambda i: (lidx, i, 0, 0)),
    ]
    out_shape = [jax.ShapeDtypeStruct((batch * seq, D_MODEL), F32)] + acc_shape
    scratch = [
        pltpu.VMEM((rows, IN_WIDTH), F32),
        pltpu.VMEM((rows, D_MODEL), BF16),
        pltpu.VMEM((rows, MIX_WIDTH), BF16),
        *_kv_scratch(nseq * SAMPLE_KEYS),
    ]
    return pl.pallas_call(
        functools.partial(body, cfg=cfg),
        grid=(batch // nseq,),
        in_specs=in_specs + acc_specs, out_specs=out_specs, out_shape=out_shape,
        input_output_aliases=aliases,
        scratch_shapes=scratch,
        compiler_params=pltpu.CompilerParams(
            dimension_semantics=("arbitrary",),
            vmem_limit_bytes=VMEM_LIMIT_BYTES),
        name=f"sample_layer{layer}",
    )(x2d, mod, g, w_in, w_out, cq, sq, ck, sk, decay, qdec, kdec, gl, sink, final_g,
      state, cache_k, cache_v, *acc_args)


def _rope_tables(start, n):
    d = RET_DK
    inv = 1.0 / (ROPE_BASE ** (jnp.arange(0, d, 2, dtype=F32) / d))
    ang_a = jnp.arange(start, start + n, CHUNK).astype(F32)[:, None] * inv[None, :]
    ang_b = jnp.arange(CHUNK).astype(F32)[:, None] * inv[None, :]
    ca, sa, cb, sb = lax.optimization_barrier((jnp.cos(ang_a), jnp.sin(ang_a), jnp.cos(ang_b), jnp.sin(ang_b)))
    cos = (ca[:, None, :] * cb[None] - sa[:, None, :] * sb[None]).reshape(n, d // 2)
    sin = (sa[:, None, :] * cb[None] + ca[:, None, :] * sb[None]).reshape(n, d // 2)
    cos2 = jnp.concatenate([cos, cos], axis=-1)
    sin2 = jnp.concatenate([-sin, sin], axis=-1)
    kscale = RET_DK ** -0.5
    return cos2, sin2, cos2 * kscale, sin2 * kscale


def _decay_tables(n):
    lg = jnp.log(1.0 - 2.0 ** (-5.0 - jnp.arange(RET_HEADS, dtype=F32)))
    idx = jnp.arange(n, dtype=F32)
    diff = idx[:, None] - idx[None, :]
    decay = jnp.where(diff[None] >= 0, jnp.exp(jnp.maximum(diff, 0.0)[None] * lg[:, None, None]), 0.0)
    qdec = jnp.exp((idx + 1.0)[None, :] * lg[:, None])
    kdec = jnp.exp((n - 1.0 - idx)[None, :] * lg[:, None])
    gl = jnp.exp(n * lg)
    bcast = lambda a: jnp.broadcast_to(a[:, :, None], (RET_HEADS, n, LANES))
    decay2 = jnp.concatenate([decay[0::2], decay[1::2]], axis=-1)
    return decay2, bcast(qdec), bcast(kdec), gl


def kernel(x_prompt, x_sample, c_prompt, c_sample, state_ret, cache_k, cache_v,
           norm_g, w_ada, b_ada, w_in, sink, w_out, final_g):
    batch, seq, _ = x_prompt.shape
    dbatch, dseq, _ = x_sample.shape

    c_all = jnp.concatenate([c_prompt, c_sample], axis=0)
    mod = _adaln(c_all, w_ada, b_ada).reshape(DEPTH, batch + dbatch, 1, 3 * D_MODEL)
    w_in_b = w_in.astype(BF16)
    w_out_b = w_out.astype(BF16)

    tabs_p = _rope_tables(0, seq)
    tabs_s = _rope_tables(PAST_LEN, dseq)
    dec_p = _decay_tables(PROMPT_BLOCK)
    dec_s = _decay_tables(dseq)
    ck4 = cache_k.reshape(DEPTH, dbatch, WINDOW, KV_WIDTH)
    cv4 = cache_v.reshape(DEPTH, dbatch, WINDOW, KV_WIDTH)
    fg = final_g.reshape(1, D_MODEL)
    g_all = norm_g.reshape(DEPTH, 1, D_MODEL)

    xp = x_prompt.reshape(batch * seq, D_MODEL)
    xs = x_sample.reshape(dbatch * dseq, D_MODEL)
    def state_acc(nb):
        return (jax.ShapeDtypeStruct((DEPTH, nb, RET_HEADS, RET_DK, RET_DV), F32),
                jax.ShapeDtypeStruct((DEPTH, nb, WINDOW, KV_WIDTH), F32),
                jax.ShapeDtypeStruct((DEPTH, nb, WINDOW, KV_WIDTH), F32))

    acc_p, acc_s = state_acc(batch), state_acc(dbatch)
    for l in range(DEPTH):
        last = l == DEPTH - 1
        xp, *acc_p = _prompt_layer(l, last, xp, mod, g_all, w_in_b, w_out_b, tabs_p, dec_p,
                                   sink, fg, acc_p, batch, seq)
        xs, *acc_s = _sample_layer(l, last, xs, mod, g_all, w_in_b, w_out_b, tabs_s, dec_s,
                                   sink, fg, state_ret, ck4, cv4, acc_s, dbatch, dseq, mod_row0=batch)

    kv_shape_p = (DEPTH, batch, WINDOW, ATT_KV_HEADS, ATT_HEAD_DIM)
    kv_shape_s = (DEPTH, dbatch, WINDOW, ATT_KV_HEADS, ATT_HEAD_DIM)
    return (xp.reshape(batch, seq, D_MODEL), xs.reshape(dbatch, dseq, D_MODEL),
            acc_p[0], acc_p[1].reshape(kv_shape_p), acc_p[2].reshape(kv_shape_p),
            acc_s[0], acc_s[1].reshape(kv_shape_s), acc_s[2].reshape(kv_shape_s))
```

```python
import functools
from typing import NamedTuple

import jax
import jax.numpy as jnp
from jax import lax
from jax.experimental import pallas as pl
from jax.experimental.pallas import tpu as pltpu

D_MODEL = 1024
DEPTH = 4
CHUNK = 64
PAST_LEN = 4096
RET_HEADS = 4
RET_DK = 128
RET_DV = 128
RET_WIDTH = RET_HEADS * RET_DV
ATT_HEAD_DIM = 64
ATT_Q_HEADS = 8
ATT_KV_HEADS = 2
ATT_WIDTH = ATT_Q_HEADS * ATT_HEAD_DIM
KV_WIDTH = ATT_KV_HEADS * ATT_HEAD_DIM
WINDOW = 128
MIX_WIDTH = RET_WIDTH + ATT_WIDTH
ROPE_BASE = 10000.0
NORM_EPS = 1e-6
GN_EPS = 1e-5

OFF_RQ = 0
OFF_RK = OFF_RQ + RET_HEADS * RET_DK
OFF_RV = OFF_RK + RET_HEADS * RET_DK
OFF_RG = OFF_RV + RET_WIDTH
OFF_AQ = OFF_RG + RET_WIDTH
OFF_AK = OFF_AQ + ATT_WIDTH
OFF_AV = OFF_AK + KV_WIDTH
OFF_AG = OFF_AV + KV_WIDTH
IN_WIDTH = OFF_AG + ATT_WIDTH

LANES = 128
ATT_SLABS = ATT_WIDTH // LANES
MASKED = -1e30
LOG2E = 1.4426950408889634
VMEM_LIMIT_BYTES = 56 * 1024 * 1024
NORM_ROWS = 32
PROJ_COLS = 512

PROMPT_ROWS = 256
PROMPT_BLOCK = 128
PROMPT_STAGES = 4
SAMPLE_SEQS = 4
SAMPLE_KEYS = 2 * LANES

BF16 = jnp.bfloat16
F32 = jnp.float32


def _silu(g):
    h = 0.5 * g
    return h + h * jnp.tanh(h)


def _dot(a, b):
    return jnp.dot(a, b, preferred_element_type=F32)


def _dot_nt(a, b):
    return lax.dot_general(a, b, (((1,), (1,)), ((), ())), preferred_element_type=F32)


def _low_lanes():
    lane = lax.broadcasted_iota(jnp.int32, (1, LANES), 1)
    return lane < ATT_HEAD_DIM


def _adaln_kernel(c_ref, w_ref, b_ref, o_ref):
    s = _silu(c_ref[...]).astype(BF16)
    o_ref[0] = _dot(s, w_ref[0].astype(BF16)) + b_ref[0]


def _adaln(c_all, w_ada, b_ada):
    n = c_all.shape[0]
    tn = D_MODEL
    return pl.pallas_call(
        _adaln_kernel,
        grid=(DEPTH, 3 * D_MODEL // tn),
        in_specs=[
            pl.BlockSpec((n, D_MODEL), lambda l, j: (0, 0)),
            pl.BlockSpec((1, D_MODEL, tn), lambda l, j: (l, 0, j)),
            pl.BlockSpec((1, 1, tn), lambda l, j: (l, 0, j)),
        ],
        out_specs=pl.BlockSpec((1, n, tn), lambda l, j: (l, 0, j)),
        out_shape=jax.ShapeDtypeStruct((DEPTH, n, 3 * D_MODEL), F32),
        compiler_params=pltpu.CompilerParams(
            dimension_semantics=("arbitrary", "arbitrary"),
            vmem_limit_bytes=VMEM_LIMIT_BYTES),
        name="adaln",
    )(c_all, w_ada, b_ada.reshape(DEPTH, 1, 3 * D_MODEL))


def _emit(items):
    for it in items:
        it()


def _interleave(a, b):
    if not a:
        return list(b)
    out, nb = [], 0
    for i, it in enumerate(a):
        out.append(it)
        want = (i + 1) * len(b) // len(a)
        out.extend(b[nb:want])
        nb = want
    return out


def _norm_items(x_ref, h_ref, r0, n, g_ref, mod_ref, s):
    def item(a):
        mod_row = mod_ref[s]
        shift = mod_row[:, 0:D_MODEL]
        gain = g_ref[...] * (1.0 + mod_row[:, D_MODEL:2 * D_MODEL])
        xb = x_ref[a:a + NORM_ROWS, :]
        ms = jnp.mean(xb * xb, axis=-1, keepdims=True)
        h_ref[a:a + NORM_ROWS, :] = (xb * lax.rsqrt(ms + NORM_EPS) * gain + shift).astype(BF16)
    return [functools.partial(item, a) for a in range(r0, r0 + n, NORM_ROWS)]


def _in_proj_items(h_ref, win_ref, z_ref):
    def item(c0):
        c1 = min(c0 + PROJ_COLS, IN_WIDTH)
        z_ref[:, c0:c1] = _dot(h_ref[...], win_ref[:, c0:c1])
    return [functools.partial(item, c0) for c0 in range(0, IN_WIDTH, PROJ_COLS)]


def _fill_kv(kv_bufs, dst0, n, k, v):
    ka_ref, kb_ref, kc_ref, kd_ref, va_ref, vb_ref, vc_ref, vd_ref = kv_bufs
    low = _low_lanes()
    kr = pltpu.roll(k, ATT_HEAD_DIM, 1)
    vr = pltpu.roll(v, ATT_HEAD_DIM, 1)
    d = slice(dst0, dst0 + n)
    ka_ref[d, :] = jnp.where(low, k, 0.0).astype(BF16)
    kb_ref[d, :] = jnp.where(low, 0.0, kr).astype(BF16)
    kc_ref[d, :] = jnp.where(low, kr, 0.0).astype(BF16)
    kd_ref[d, :] = jnp.where(low, 0.0, k).astype(BF16)
    va_ref[d, 0:LANES] = jnp.where(low, v, 0.0).astype(BF16)
    vb_ref[d, 0:LANES] = jnp.where(low, 0.0, vr).astype(BF16)
    vc_ref[d, 0:LANES] = jnp.where(low, vr, 0.0).astype(BF16)
    vd_ref[d, 0:LANES] = jnp.where(low, 0.0, v).astype(BF16)


def _init_kv_ones(kv_bufs):
    low = _low_lanes()
    for i, ref in enumerate(kv_bufs[4:]):
        ones = jnp.where(low, 1.0, 0.0) if i % 2 == 0 else jnp.where(low, 0.0, 1.0)
        ref[:, LANES:] = jnp.broadcast_to(ones, (ref.shape[0], LANES)).astype(BF16)


class _MixRefs(NamedTuple):
    z: object
    mix: object
    tabs: tuple
    dec: tuple
    sink: object
    kv_bufs: tuple


class _ValueStash:
    def __init__(self):
        self.d = {}

    def put(self, key, *vals):
        self.d[key] = vals

    def get(self, key):
        return self.d[key]


class _RefStash:
    def __init__(self, refs, j, blk):
        self.refs, self.j, self.blk = refs, j, blk
        self.rws = slice(j * blk, (j + 1) * blk)

    def _slots(self, key):
        lhs_ref, kdt_ref, v_ref, gate_ref, p_ref, e_ref = self.refs
        kind, i = key
        rws, j, blk = self.rws, self.j, self.blk
        if kind == "ret":
            w = lhs_ref.shape[1] // 2
            r0 = (2 * j + i) * RET_DK
            return [(lhs_ref, rws, slice(i * w, (i + 1) * w)), (kdt_ref, slice(r0, r0 + RET_DK), slice(None)),
                    (v_ref, rws, slice(2 * i * RET_DV, 2 * (i + 1) * RET_DV))]
        if kind == "gate":
            return [(gate_ref, rws, slice(None))]
        r0 = (2 * j + i) * 2 * blk
        return [(p_ref, slice(r0, r0 + 2 * blk), slice(None)),
                (e_ref, rws, slice(2 * i * LANES, (2 * i + 1) * LANES)),
                (e_ref, rws, slice((2 * i + 1) * LANES, (2 * i + 2) * LANES))]

    def put(self, key, *vals):
        for (ref, r, c), v in zip(self._slots(key), vals):
            ref[r, c] = v

    def get(self, key):
        return tuple(ref[r, c] for ref, r, c in self._slots(key))


def _block_diag(a, b):
    za = jnp.zeros(a.shape, a.dtype)
    return jnp.concatenate([jnp.concatenate([a, za], axis=1), jnp.concatenate([za, b], axis=1)], axis=0)


def _front_items(m: _MixRefs, layer, r0, blk, trow0, win0, lk, valid_fn, stash):
    z_ref = m.z
    cq_ref, sq_ref, ck_ref, sk_ref = m.tabs
    decay_ref, qdec_ref, kdec_ref, _ = m.dec
    ka_ref, kb_ref, kc_ref, kd_ref = m.kv_bufs[:4]
    rws = slice(r0, r0 + blk)
    trs = slice(trow0, trow0 + blk)
    win = slice(win0, win0 + lk)

    def ret_front(hp):
        cq, sq, ck, sk = cq_ref[trs, :], sq_ref[trs, :], ck_ref[trs, :], sk_ref[trs, :]
        qs, ks, vs, qds, kds = [], [], [], [], []
        for hd in (2 * hp, 2 * hp + 1):
            c = hd * RET_DK
            q = z_ref[rws, OFF_RQ + c:OFF_RQ + c + RET_DK]
            k = z_ref[rws, OFF_RK + c:OFF_RK + c + RET_DK]
            q = q * cq + pltpu.roll(q, RET_DK // 2, 1) * sq
            k = k * ck + pltpu.roll(k, RET_DK // 2, 1) * sk
            qs.append(q.astype(BF16))
            ks.append(k.astype(BF16))
            vs.append(z_ref[rws, OFF_RV + c:OFF_RV + c + RET_DV].astype(BF16))
            qds.append((q * qdec_ref[hd]).astype(BF16))
            kds.append(k * kdec_ref[hd])
        s2 = _dot_nt(jnp.concatenate(qs, axis=1), _block_diag(*ks)) * decay_ref[hp]
        lhs = jnp.concatenate([s2.astype(BF16)] + qds, axis=1)
        kd_t = jnp.concatenate(kds, axis=0).T.astype(BF16)
        stash.put(("ret", hp), lhs, kd_t, jnp.concatenate(vs, axis=1))

    def gates():
        stash.put(("gate", 0), jnp.concatenate([_silu(z_ref[rws, OFF_RG:OFF_RG + RET_WIDTH]),
                                                _silu(z_ref[rws, OFF_AG:OFF_AG + ATT_WIDTH])], axis=1))

    def att_front(kv):
        low = _low_lanes()
        valid = None if valid_fn is None else valid_fn()
        slabs = (2 * kv, 2 * kv + 1)
        k_lo, k_hi = (ka_ref, kb_ref) if kv == 0 else (kc_ref, kd_ref)
        qf = jnp.concatenate([z_ref[rws, OFF_AQ + sl * LANES:OFF_AQ + (sl + 1) * LANES] for sl in slabs], axis=0)
        qf = (qf * (ATT_HEAD_DIM ** -0.5 * LOG2E)).astype(BF16)
        keys = jnp.concatenate([k_lo[win, :], k_hi[win, :]], axis=0)
        s = _dot_nt(qf, keys)
        ps, es = [], []
        for i, sl in enumerate(slabs):
            row_p, row_e = [], []
            for h in range(2):
                sh = s[i * blk:(i + 1) * blk, h * lk:(h + 1) * lk]
                if valid is not None:
                    sh = jnp.where(valid, sh, MASKED)
                sink = m.sink[layer, 2 * sl + h] * LOG2E
                mh = jnp.maximum(jnp.max(sh, axis=-1, keepdims=True), sink)
                row_p.append(jnp.exp2(sh - mh).astype(BF16))
                row_e.append(jnp.exp2(sink - mh))
            ps.append(jnp.concatenate(row_p, axis=1))
            es.append(jnp.broadcast_to(jnp.where(low, row_e[0], row_e[1]), (blk, LANES)))
        stash.put(("att", kv), jnp.concatenate(ps, axis=0), *es)

    return ([functools.partial(ret_front, hp) for hp in range(RET_HEADS // 2)] + [gates]
            + [functools.partial(att_front, kv) for kv in range(ATT_KV_HEADS)])


def _back_items(m: _MixRefs, r0, blk, win0, lk, r_get, r_set, stash):
    mix_ref = m.mix
    gl_ref = m.dec[3]
    va_ref, vb_ref, vc_ref, vd_ref = m.kv_bufs[4:]
    rws = slice(r0, r0 + blk)
    win = slice(win0, win0 + lk)

    def ret_back(hp):
        heads = (2 * hp, 2 * hp + 1)
        lhs, kd_t, v2 = stash.get(("ret", hp))
        gate = stash.get(("gate", 0))[0]
        r_old = [r_get(hd) for hd in heads]
        v_bd = _block_diag(v2[:, :RET_DV], v2[:, RET_DV:])
        r_bd = _block_diag(*[r.astype(BF16) for r in r_old])
        o2 = _dot(lhs, jnp.concatenate([v_bd, r_bd], axis=0))
        r_inc = _dot(kd_t, v_bd)
        for i, hd in enumerate(heads):
            c = hd * RET_DK
            r_set(hd, gl_ref[hd] * r_old[i] + r_inc[:, i * RET_DV:(i + 1) * RET_DV])
            o = o2[:, i * RET_DV:(i + 1) * RET_DV]
            mu = jnp.mean(o, axis=-1, keepdims=True)
            oc = o - mu
            var = jnp.mean(oc * oc, axis=-1, keepdims=True)
            on = oc * lax.rsqrt(var + GN_EPS)
            mix_ref[rws, c:c + RET_DV] = (on * gate[:, c:c + RET_DV]).astype(BF16)

    def att_back(kv):
        v_lo, v_hi = (va_ref, vb_ref) if kv == 0 else (vc_ref, vd_ref)
        p, *es = stash.get(("att", kv))
        gate = stash.get(("gate", 0))[0]
        vals = jnp.concatenate([v_lo[win, :], v_hi[win, :]], axis=0)
        acc = _dot(p, vals)
        for i, sl in enumerate((2 * kv, 2 * kv + 1)):
            c = RET_WIDTH + sl * LANES
            a = acc[i * blk:(i + 1) * blk, :]
            den = a[:, LANES:] + es[i]
            mix_ref[rws, c:c + LANES] = (a[:, :LANES] / den * gate[:, c:c + LANES]).astype(BF16)

    return ([functools.partial(ret_back, hp) for hp in range(RET_HEADS // 2)]
            + [functools.partial(att_back, kv) for kv in range(ATT_KV_HEADS)])


def _final_norm_rows(xo_ref, r0, n, fg_row):
    for a in range(r0, r0 + n, NORM_ROWS):
        xn = xo_ref[a:a + NORM_ROWS, :]
        ms = jnp.mean(xn * xn, axis=-1, keepdims=True)
        xo_ref[a:a + NORM_ROWS, :] = xn * lax.rsqrt(ms + NORM_EPS) * fg_row


class _PromptCfg(NamedTuple):
    rows: int
    nt: int
    ntiles: int
    last: bool
    layer: int
    first: bool


N_STASH = 6
N_KV = 8


def _prompt_kernel(xn_ref, xc_ref, modn_ref, modc_ref, g_ref, win_ref, wout_ref,
                   cq_ref, sq_ref, ck_ref, sk_ref, decay_ref, qdec_ref, kdec_ref, gl_ref, sink_ref, fg_ref,
                   _ro_prev, _ko_prev, _vo_prev,
                   xo_ref, ro_ref, ko_ref, vo_ref,
                   za_ref, zb_ref, h_ref, mixa_ref, mixb_ref, r_scr, *bufs,
                   cfg: _PromptCfg):
    rows, nt, ntiles = cfg.rows, cfg.nt, cfg.ntiles
    blk, lk = PROMPT_BLOCK, WINDOW + PROMPT_BLOCK
    stash_refs = (bufs[0:N_STASH], bufs[N_STASH:2 * N_STASH])
    kv_refs = (bufs[2 * N_STASH:2 * N_STASH + N_KV], bufs[2 * N_STASH + N_KV:2 * N_STASH + 2 * N_KV])
    z_refs = (za_ref, zb_ref)
    mix_refs = (mixa_ref, mixb_ref)
    g = pl.program_id(0)
    t = jnp.clip(g - 1, 0, ntiles - 1) % nt
    t3 = jnp.clip(g - 2, 0, ntiles - 1) % nt

    @pl.when(g == 0)
    def _():
        zb_ref[...] = jnp.zeros(zb_ref.shape, F32)
        mixb_ref[...] = jnp.zeros(mixb_ref.shape, BF16)
        for b in stash_refs[0] + kv_refs[0]:
            b[...] = jnp.zeros(b.shape, b.dtype)
        stash_refs[0][-1][...] = jnp.ones(stash_refs[0][-1].shape, F32)
        _init_kv_ones(kv_refs[0])
        _init_kv_ones(kv_refs[1])

    @pl.when(t3 == 0)
    def _():
        r_scr[...] = jnp.zeros(r_scr.shape, F32)

    (ro_ref, ko_ref, vo_ref), clear_other_layers = _own_layer((ro_ref, ko_ref, vo_ref), cfg.layer, cfg.first)

    @pl.when((t == 0) | (t3 == 0))
    def _():
        clear_other_layers()

    def r_set(hd, val):
        r_scr[hd] = val

    def visible(j):
        col_i = lax.broadcasted_iota(jnp.int32, (blk, lk), 1)
        ok = None
        if blk == 2 * CHUNK:
            row_i = lax.broadcasted_iota(jnp.int32, (blk, lk), 0)
            first_key = jnp.where(row_i < CHUNK, 0, CHUNK)
            ok = (col_i >= first_key) & (col_i < first_key + WINDOW + CHUNK)
        if j * blk < WINDOW:
            started = col_i >= jnp.where(t == 0, WINDOW - j * blk, 0)
            ok = started if ok is None else ok & started
        return ok

    def step(p):
        a, b = p, 1 - p
        tabs = (cq_ref, sq_ref, ck_ref, sk_ref)
        dec = (decay_ref, qdec_ref, kdec_ref, gl_ref)

        norm = _norm_items(xn_ref, h_ref, 0, rows, g_ref, modn_ref, 0)
        proj = _in_proj_items(h_ref, win_ref, z_refs[a])

        def out_item(c0):
            cs = slice(c0, c0 + PROJ_COLS)
            gate_row = modc_ref[0][:, 2 * D_MODEL + c0:2 * D_MODEL + c0 + PROJ_COLS]
            xo_ref[:, cs] = xc_ref[:, cs] + gate_row * _dot(mix_refs[b][...], wout_ref[:, cs])
        outp = [functools.partial(out_item, c0) for c0 in range(0, D_MODEL, PROJ_COLS)]

        zr_ref = z_refs[b]
        for dst, src in zip(kv_refs[b], kv_refs[a]):
            dst[0:WINDOW, :] = src[rows:rows + WINDOW, :]
        for r0 in range(0, rows, CHUNK):
            _fill_kv(kv_refs[b], WINDOW + r0, CHUNK, zr_ref[r0:r0 + CHUNK, OFF_AK:OFF_AK + KV_WIDTH],
                     zr_ref[r0:r0 + CHUNK, OFF_AV:OFF_AV + KV_WIDTH])
        front = []
        for j in range(rows // blk):
            front += _front_items(_MixRefs(zr_ref, None, tabs, dec, sink_ref, kv_refs[b]), cfg.layer,
                                  j * blk, blk, j * blk, j * blk, lk, functools.partial(visible, j),
                                  _RefStash(stash_refs[b], j, blk))

        back = []
        for j in range(rows // blk):
            back += _back_items(_MixRefs(None, mix_refs[a], tabs, dec, sink_ref, kv_refs[a]),
                                j * blk, blk, j * blk, lk, lambda hd: r_scr[hd], r_set,
                                _RefStash(stash_refs[a], j, blk))

        _emit(_interleave(norm, outp))
        _emit(_interleave(_interleave(front, back), proj))
        if cfg.last:
            _final_norm_rows(xo_ref, 0, rows, fg_ref[...])

        @pl.when(t == nt - 1)
        def _():
            ko_ref[...] = zr_ref[rows - WINDOW:rows, OFF_AK:OFF_AK + KV_WIDTH]
            vo_ref[...] = zr_ref[rows - WINDOW:rows, OFF_AV:OFF_AV + KV_WIDTH]

    for parity in (0, 1):
        @pl.when(g % 2 == parity)
        def _():
            step(parity)

    @pl.when((t3 == nt - 1) & (g <= ntiles + 1))
    def _():
        ro_ref[...] = r_scr[...]


def _stash_scratch(rows, blk, lk):
    nblk = rows // blk
    return [
        pltpu.VMEM((rows, 2 * (2 * blk + 2 * RET_DK)), BF16),
        pltpu.VMEM((nblk * 2 * RET_DK, 2 * blk), BF16),
        pltpu.VMEM((rows, RET_WIDTH), BF16),
        pltpu.VMEM((rows, MIX_WIDTH), F32),
        pltpu.VMEM((nblk * 2 * 2 * blk, 2 * lk), BF16),
        pltpu.VMEM((rows, ATT_WIDTH), F32),
    ]


def _kv_scratch(krows):
    return ([pltpu.VMEM((krows, LANES), BF16) for _ in range(4)]
            + [pltpu.VMEM((krows, 2 * LANES), BF16) for _ in range(4)])


def _const_spec(shape):
    nd = len(shape)
    return pl.BlockSpec(shape, lambda g: (0,) * nd)


def _layer_spec(shape, layer):
    nd = len(shape)
    return pl.BlockSpec((None, *shape), lambda g: (layer,) + (0,) * nd)


def _smem_spec():
    return pl.BlockSpec(memory_space=pltpu.SMEM)


N_STATE = 3


def _is_first(acc):
    return all(isinstance(a, jax.ShapeDtypeStruct) for a in acc)


def _own_layer(state_refs, layer, first):
    if not first:
        return state_refs, lambda: None

    def clear():
        for ref in state_refs:
            for l in range(DEPTH):
                if l != layer:
                    ref[l] = jnp.zeros(ref.shape[1:], ref.dtype)
    return [ref.at[layer] for ref in state_refs], clear


def _state_alias(kernel, n_in, acc):
    out_shape = [jax.ShapeDtypeStruct(a.shape, a.dtype) for a in acc]
    if _is_first(acc):
        def first(*refs, **kw):
            return kernel(*refs[:n_in], *([None] * N_STATE), *refs[n_in:], **kw)
        return first, [], [], {}, out_shape
    in_specs = [pl.BlockSpec(memory_space=pl.ANY)] * N_STATE
    aliases = {n_in + i: 1 + i for i in range(N_STATE)}
    return kernel, list(acc), in_specs, aliases, out_shape


def _prompt_layer(layer, last, x2d, mod, g, w_in, w_out, tabs, dec, sink, final_g, acc, batch, seq):
    rows = PROMPT_ROWS
    blk, lk = PROMPT_BLOCK, WINDOW + PROMPT_BLOCK
    nt = seq // rows
    ntiles = batch * nt
    first = _is_first(acc)
    cfg = _PromptCfg(rows=rows, nt=nt, ntiles=ntiles, last=last, layer=layer, first=first)
    cq, sq, ck, sk = tabs
    decay, qdec, kdec, gl = dec

    nxt = lambda g: jnp.minimum(g, ntiles - 1)
    cur = lambda g: jnp.clip(g - 1, 0, ntiles - 1)
    bak = lambda g: jnp.clip(g - 2, 0, ntiles - 1)
    fin = lambda g: jnp.maximum(g - 3, 0)
    tab_spec = pl.BlockSpec((rows, LANES), lambda g: (cur(g) % nt, 0))
    in_specs = [
        pl.BlockSpec((rows, D_MODEL), lambda g: (nxt(g), 0)),
        pl.BlockSpec((rows, D_MODEL), lambda g: (fin(g), 0)),
        pl.BlockSpec((None, 1, 1, 3 * D_MODEL), lambda g: (layer, nxt(g) // nt, 0, 0)),
        pl.BlockSpec((None, 1, 1, 3 * D_MODEL), lambda g: (layer, fin(g) // nt, 0, 0)),
        _layer_spec((1, D_MODEL), layer),
        _layer_spec((D_MODEL, IN_WIDTH), layer),
        _layer_spec((MIX_WIDTH, D_MODEL), layer),
        tab_spec, tab_spec, tab_spec, tab_spec,
        _const_spec(decay.shape), _const_spec(qdec.shape), _const_spec(kdec.shape),
        _smem_spec(), _smem_spec(),
        _const_spec((1, D_MODEL)),
    ]
    body, acc_args, acc_specs, aliases, acc_shape = _state_alias(_prompt_kernel, len(in_specs), acc)
    lyr, lidx = (DEPTH, 0) if first else (None, layer)
    out_specs = [
        pl.BlockSpec((rows, D_MODEL), lambda g: (fin(g), 0)),
        pl.BlockSpec((lyr, None, RET_HEADS, RET_DK, RET_DV), lambda g: (lidx, bak(g) // nt, 0, 0, 0)),
        pl.BlockSpec((lyr, None, WINDOW, KV_WIDTH), lambda g: (lidx, cur(g) // nt, 0, 0)),
        pl.BlockSpec((lyr, None, WINDOW, KV_WIDTH), lambda g: (lidx, cur(g) // nt, 0, 0)),
    ]
    out_shape = [jax.ShapeDtypeStruct((batch * seq, D_MODEL), F32)] + acc_shape
    scratch = [
        pltpu.VMEM((rows, IN_WIDTH), F32),
        pltpu.VMEM((rows, IN_WIDTH), F32),
        pltpu.VMEM((rows, D_MODEL), BF16),
        pltpu.VMEM((rows, MIX_WIDTH), BF16),
        pltpu.VMEM((rows, MIX_WIDTH), BF16),
        pltpu.VMEM((RET_HEADS, RET_DK, RET_DV), F32),
        *_stash_scratch(rows, blk, lk), *_stash_scratch(rows, blk, lk),
        *_kv_scratch(WINDOW + rows), *_kv_scratch(WINDOW + rows),
    ]
    return pl.pallas_call(
        functools.partial(body, cfg=cfg),
        grid=(ntiles + PROMPT_STAGES - 1,),
        in_specs=in_specs + acc_specs, out_specs=out_specs, out_shape=out_shape,
        input_output_aliases=aliases,
        scratch_shapes=scratch,
        compiler_params=pltpu.CompilerParams(
            dimension_semantics=("arbitrary",),
            vmem_limit_bytes=VMEM_LIMIT_BYTES),
        name=f"prompt_layer{layer}",
    )(x2d, x2d, mod, mod, g, w_in, w_out, cq, sq, ck, sk, decay, qdec, kdec, gl, sink, final_g, *acc_args)


class _SampleCfg(NamedTuple):
    nseq: int
    seq: int
    last: bool
    layer: int
    first: bool


def _sample_kernel(x_ref, mod_ref, g_ref, win_ref, wout_ref,
                   cq_ref, sq_ref, ck_ref, sk_ref, decay_ref, qdec_ref, kdec_ref, gl_ref, sink_ref, fg_ref,
                   st_ref, cachek_ref, cachev_ref,
                   _ro_prev, _ko_prev, _vo_prev,
                   xo_ref, ro_ref, ko_ref, vo_ref,
                   z_ref, h_ref, mix_ref, *kv_bufs,
                   cfg: _SampleCfg):
    nseq, seq = cfg.nseq, cfg.seq
    lk = SAMPLE_KEYS
    pad = lk - WINDOW - seq

    @pl.when(pl.program_id(0) == 0)
    def _():
        _init_kv_ones(kv_bufs)

    (ro_ref, ko_ref, vo_ref), clear_other_layers = _own_layer((ro_ref, ko_ref, vo_ref), cfg.layer, cfg.first)
    clear_other_layers()

    for s in range(nseq):
        _emit(_norm_items(x_ref, h_ref, s * seq, seq, g_ref, mod_ref, s))
    _emit(_in_proj_items(h_ref, win_ref, z_ref))

    def visible():
        return lax.broadcasted_iota(jnp.int32, (seq, lk), 1) >= pad

    m = _MixRefs(z_ref, mix_ref, (cq_ref, sq_ref, ck_ref, sk_ref),
                 (decay_ref, qdec_ref, kdec_ref, gl_ref), sink_ref, kv_bufs)
    fronts, backs = [], []
    for s in range(nseq):
        rws = slice(s * seq, (s + 1) * seq)
        k_new = z_ref[rws, OFF_AK:OFF_AK + KV_WIDTH]
        v_new = z_ref[rws, OFF_AV:OFF_AV + KV_WIDTH]
        for b in kv_bufs:
            b[s * lk:s * lk + pad, 0:LANES] = jnp.zeros((pad, LANES), BF16)
        _fill_kv(kv_bufs, s * lk + pad, WINDOW, cachek_ref[s], cachev_ref[s])
        _fill_kv(kv_bufs, s * lk + pad + WINDOW, seq, k_new, v_new)

        def r_set(hd, val, s=s):
            ro_ref[s, hd] = val

        stash = _ValueStash()
        fronts.append(_front_items(m, cfg.layer, s * seq, seq, 0, s * lk, lk, visible, stash))
        backs.append(_back_items(m, s * seq, seq, s * lk, lk, lambda hd, s=s: st_ref[s, hd], r_set, stash))
        ko_ref[s, 0:WINDOW - seq, :] = cachek_ref[s, seq:WINDOW, :]
        vo_ref[s, 0:WINDOW - seq, :] = cachev_ref[s, seq:WINDOW, :]
        ko_ref[s, WINDOW - seq:WINDOW, :] = k_new
        vo_ref[s, WINDOW - seq:WINDOW, :] = v_new

    _emit(fronts[0])
    for s in range(1, nseq):
        _emit(_interleave(fronts[s], backs[s - 1]))
    _emit(backs[nseq - 1])

    xo_ref[...] = _dot(mix_ref[...], wout_ref[...])
    for s in range(nseq):
        rws = slice(s * seq, (s + 1) * seq)
        gate_row = mod_ref[s][:, 2 * D_MODEL:3 * D_MODEL]
        xo_ref[rws, :] = x_ref[rws, :] + gate_row * xo_ref[rws, :]
    if cfg.last:
        _final_norm_rows(xo_ref, 0, nseq * seq, fg_ref[...])


def _sample_layer(layer, last, x2d, mod, g, w_in, w_out, tabs, dec, sink, final_g,
                  state, cache_k, cache_v, acc, batch, seq, mod_row0):
    nseq = SAMPLE_SEQS
    rows = nseq * seq
    first = _is_first(acc)
    cfg = _SampleCfg(nseq=nseq, seq=seq, last=last, layer=layer, first=first)
    cq, sq, ck, sk = tabs
    decay, qdec, kdec, gl = dec
    mod_blk0 = mod_row0 // nseq
    in_specs = [
        pl.BlockSpec((rows, D_MODEL), lambda i: (i, 0)),
        pl.BlockSpec((None, nseq, 1, 3 * D_MODEL), lambda i: (layer, mod_blk0 + i, 0, 0)),
        _layer_spec((1, D_MODEL), layer),
        _layer_spec((D_MODEL, IN_WIDTH), layer),
        _layer_spec((MIX_WIDTH, D_MODEL), layer),
        _const_spec(cq.shape), _const_spec(sq.shape), _const_spec(ck.shape), _const_spec(sk.shape),
        _const_spec(decay.shape), _const_spec(qdec.shape), _const_spec(kdec.shape),
        _smem_spec(), _smem_spec(),
        _const_spec((1, D_MODEL)),
        pl.BlockSpec((None, nseq, RET_HEADS, RET_DK, RET_DV), lambda i: (layer, i, 0, 0, 0)),
        pl.BlockSpec((None, nseq, WINDOW, KV_WIDTH), lambda i: (layer, i, 0, 0)),
        pl.BlockSpec((None, nseq, WINDOW, KV_WIDTH), lambda i: (layer, i, 0, 0)),
    ]
    body, acc_args, acc_specs, aliases, acc_shape = _state_alias(_sample_kernel, len(in_specs), acc)
    lyr, lidx = (DEPTH, 0) if first else (None, layer)
    out_specs = [
        pl.BlockSpec((rows, D_MODEL), lambda i: (i, 0)),
        pl.BlockSpec((lyr, nseq, RET_HEADS, RET_DK, RET_DV), lambda i: (lidx, i, 0, 0, 0)),
        pl.BlockSpec((lyr, nseq, WINDOW, KV_WIDTH), lambda i: (lidx, i, 0, 0)),
        pl.BlockSpec((lyr, nseq, WINDOW, KV_WIDTH), lambda i: (lidx, i, 0, 0)),
    ]
    out_shape = [jax.ShapeDtypeStruct((batch * seq, D_MODEL), F32)] + acc_shape
    scratch = [
        pltpu.VMEM((rows, IN_WIDTH), F32),
        pltpu.VMEM((rows, D_MODEL), BF16),
        pltpu.VMEM((rows, MIX_WIDTH), BF16),
        *_kv_scratch(nseq * SAMPLE_KEYS),
    ]
    return pl.pallas_call(
        functools.partial(body, cfg=cfg),
        grid=(batch // nseq,),
        in_specs=in_specs + acc_specs, out_specs=out_specs, out_shape=out_shape,
        input_output_aliases=aliases,
        scratch_shapes=scratch,
        compiler_params=pltpu.CompilerParams(
            dimension_semantics=("arbitrary",),
            vmem_limit_bytes=VMEM_LIMIT_BYTES),
        name=f"sample_layer{layer}",
    )(x2d, mod, g, w_in, w_out, cq, sq, ck, sk, decay, qdec, kdec, gl, sink, final_g,
      state, cache_k, cache_v, *acc_args)


def _rope_tables(start, n):
    d = RET_DK
    inv = 1.0 / (ROPE_BASE ** (jnp.arange(0, d, 2, dtype=F32) / d))
    ang_a = jnp.arange(start, start + n, CHUNK).astype(F32)[:, None] * inv[None, :]
    ang_b = jnp.arange(CHUNK).astype(F32)[:, None] * inv[None, :]
    ca, sa, cb, sb = lax.optimization_barrier((jnp.cos(ang_a), jnp.sin(ang_a), jnp.cos(ang_b), jnp.sin(ang_b)))
    cos = (ca[:, None, :] * cb[None] - sa[:, None, :] * sb[None]).reshape(n, d // 2)
    sin = (sa[:, None, :] * cb[None] + ca[:, None, :] * sb[None]).reshape(n, d // 2)
    cos2 = jnp.concatenate([cos, cos], axis=-1)
    sin2 = jnp.concatenate([-sin, sin], axis=-1)
    kscale = RET_DK ** -0.5
    return cos2, sin2, cos2 * kscale, sin2 * kscale


def _decay_tables(n):
    lg = jnp.log(1.0 - 2.0 ** (-5.0 - jnp.arange(RET_HEADS, dtype=F32)))
    idx = jnp.arange(n, dtype=F32)
    diff = idx[:, None] - idx[None, :]
    decay = jnp.where(diff[None] >= 0, jnp.exp(jnp.maximum(diff, 0.0)[None] * lg[:, None, None]), 0.0)
    qdec = jnp.exp((idx + 1.0)[None, :] * lg[:, None])
    kdec = jnp.exp((n - 1.0 - idx)[None, :] * lg[:, None])
    gl = jnp.exp(n * lg)
    bcast = lambda a: jnp.broadcast_to(a[:, :, None], (RET_HEADS, n, LANES))
    decay2 = jnp.concatenate([decay[0::2], decay[1::2]], axis=-1)
    return decay2, bcast(qdec), bcast(kdec), gl


def kernel(x_prompt, x_sample, c_prompt, c_sample, state_ret, cache_k, cache_v,
           norm_g, w_ada, b_ada, w_in, sink, w_out, final_g):
    batch, seq, _ = x_prompt.shape
    dbatch, dseq, _ = x_sample.shape

    c_all = jnp.concatenate([c_prompt, c_sample], axis=0)
    mod = _adaln(c_all, w_ada, b_ada).reshape(DEPTH, batch + dbatch, 1, 3 * D_MODEL)
    w_in_b = w_in.astype(BF16)
    w_out_b = w_out.astype(BF16)

    tabs_p = _rope_tables(0, seq)
    tabs_s = _rope_tables(PAST_LEN, dseq)
    dec_p = _decay_tables(PROMPT_BLOCK)
    dec_s = _decay_tables(dseq)
    ck4 = cache_k.reshape(DEPTH, dbatch, WINDOW, KV_WIDTH)
    cv4 = cache_v.reshape(DEPTH, dbatch, WINDOW, KV_WIDTH)
    fg = final_g.reshape(1, D_MODEL)
    g_all = norm_g.reshape(DEPTH, 1, D_MODEL)

    xp = x_prompt.reshape(batch * seq, D_MODEL)
    xs = x_sample.reshape(dbatch * dseq, D_MODEL)
    def state_acc(nb):
        return (jax.ShapeDtypeStruct((DEPTH, nb, RET_HEADS, RET_DK, RET_DV), F32),
                jax.ShapeDtypeStruct((DEPTH, nb, WINDOW, KV_WIDTH), F32),
                jax.ShapeDtypeStruct((DEPTH, nb, WINDOW, KV_WIDTH), F32))

    acc_p, acc_s = state_acc(batch), state_acc(dbatch)
    for l in range(DEPTH):
        last = l == DEPTH - 1
        xp, *acc_p = _prompt_layer(l, last, xp, mod, g_all, w_in_b, w_out_b, tabs_p, dec_p,
                                   sink, fg, acc_p, batch, seq)
        xs, *acc_s = _sample_layer(l, last, xs, mod, g_all, w_in_b, w_out_b, tabs_s, dec_s,
                                   sink, fg, state_ret, ck4, cv4, acc_s, dbatch, dseq, mod_row0=batch)

    kv_shape_p = (DEPTH, batch, WINDOW, ATT_KV_HEADS, ATT_HEAD_DIM)
    kv_shape_s = (DEPTH, dbatch, WINDOW, ATT_KV_HEADS, ATT_HEAD_DIM)
    return (xp.reshape(batch, seq, D_MODEL), xs.reshape(dbatch, dseq, D_MODEL),
            acc_p[0], acc_p[1].reshape(kv_shape_p), acc_p[2].reshape(kv_shape_p),
            acc_s[0], acc_s[1].reshape(kv_shape_s), acc_s[2].reshape(kv_shape_s))
```
